```python
import jax
import jax.numpy as jnp
from jax import lax
import numpy as np

D_MODEL = 1024
BATCH = 8
SEQ = 2048
DEPTH = 2
DEC_BATCH = 128
DEC_SEQ = 8
PAST_LEN = 16384
PAGE_SIZE = 128

N_META = 16
MIX_WIDTH = D_MODEL
POOL_WIDTH = MIX_WIDTH // 2
CONV_WIDTH = MIX_WIDTH - POOL_WIDTH
POOL_WINDOWS = (2, 4, 8, 16)
N_POOL_GROUPS = len(POOL_WINDOWS)
POOL_GROUP = POOL_WIDTH // N_POOL_GROUPS
POOL_BUF = max(POOL_WINDOWS) - 1
CONV_KERNEL = 31
CONV_BUF = CONV_KERNEL - 1
IN_COLS = POOL_WIDTH + 2 * CONV_WIDTH
D_FF = ((8 * D_MODEL // 3 + 255) // 256) * 256
N_EXPERTS = 8
TOP_K = 2
EXPERT_FF = 7 * D_MODEL // 2
N_DENSE = (DEPTH + 1) // 2
N_MOE = DEPTH // 2
RMS_EPS = 1e-6
LN_EPS = 1e-5

kernel_name = 'hybrid_pool_conformer_conv_decoder_step'


def _rmsnorm(x, g):
    xf = x.astype(jnp.float32)
    y = xf * lax.rsqrt(jnp.mean(xf * xf, axis=-1, keepdims=True) + RMS_EPS)
    return (y * g.astype(jnp.float32)).astype(x.dtype)


def _layernorm(x, g, b):
    xf = x.astype(jnp.float32)
    mu = jnp.mean(xf, axis=-1, keepdims=True)
    var = jnp.mean(jnp.square(xf - mu), axis=-1, keepdims=True)
    y = (xf - mu) * lax.rsqrt(var + LN_EPS)
    return (y * g.astype(jnp.float32) + b.astype(jnp.float32)).astype(x.dtype)


def _pool_mixer(u_past, u_new, start_pos, w_grp, scale):
    n_past = u_past.shape[1]
    ext = jnp.concatenate([u_past.astype(u_new.dtype), u_new], axis=1)
    length = ext.shape[1]
    cs = jnp.cumsum(ext.astype(jnp.float32), axis=1)
    cs0 = jnp.pad(cs, ((0, 0), (1, 0), (0, 0)))
    rows = jnp.arange(n_past, length)
    pos = start_pos + rows
    pooled = []
    for g, win in enumerate(POOL_WINDOWS):
        ch = slice(g * POOL_GROUP, (g + 1) * POOL_GROUP)
        lo = jnp.maximum(rows + 1 - win, 0)
        s = cs0[:, rows + 1, ch] - cs0[:, lo, ch]
        cnt = jnp.minimum(pos + 1, win).astype(jnp.float32)
        pooled.append(s / cnt[None, :, None])
    pooled = jnp.concatenate(pooled, axis=-1) - u_new.astype(jnp.float32)
    b, t, _ = u_new.shape
    pg = pooled.astype(u_new.dtype).reshape(b, t, N_POOL_GROUPS, POOL_GROUP)
    out = jnp.einsum('btgc,gcd->btgd', pg, w_grp).reshape(b, t, POOL_WIDTH) * scale
    return out, ext[:, length - POOL_BUF:]


def _conv_module(a_past, z, dw_w, dw_b, ln_g, ln_b, pw_w):
    v = z[..., :CONV_WIDTH] * jax.nn.sigmoid(z[..., CONV_WIDTH:])
    ext = jnp.concatenate([a_past.astype(v.dtype), v], axis=1)
    y = lax.conv_general_dilated(
        ext, dw_w[:, None, :].astype(v.dtype), window_strides=(1,), padding='VALID',
        dimension_numbers=('NWC', 'WIO', 'NWC'), feature_group_count=CONV_WIDTH) + dw_b
    y = jax.nn.silu(_layernorm(y, ln_g, ln_b))
    return y @ pw_w, ext[:, ext.shape[1] - CONV_BUF:]


def _swiglu(x, wg, wu, wd):
    return (jax.nn.silu(x @ wg) * (x @ wu)) @ wd


def _moe(x, router_w, wg, wu, wd):
    b, t, d = x.shape
    xf = x.reshape(b * t, d)
    logits = (xf @ router_w).astype(jnp.float32)
    top_val, top_idx = lax.top_k(logits, TOP_K)
    gates = jax.nn.softmax(top_val, axis=-1)
    combine = jnp.sum(jax.nn.one_hot(top_idx, N_EXPERTS, dtype=jnp.float32) * gates[..., None], axis=1)
    y = jnp.zeros((b * t, d), jnp.float32)
    for e in range(N_EXPERTS):
        y = y + combine[:, e:e + 1] * _swiglu(xf, wg[e], wu[e], wd[e]).astype(jnp.float32)
    return y.astype(x.dtype).reshape(b, t, d)


def _trunk(h, pool_pasts, conv_pasts, pool_start, norm1_g, w_in, pool_w, pool_scale, conv_dw_w, conv_dw_b,
           conv_ln_g, conv_ln_b, conv_pw_w, w_out, norm2_g, ffn_w_gate, ffn_w_up, ffn_w_down,
           router_w, moe_w_gate, moe_w_up, moe_w_down):
    new_pool, new_conv = [], []
    for l in range(DEPTH):
        hn = _rmsnorm(h, norm1_g[l])
        u = hn @ w_in[l]
        pool_out, pool_buf = _pool_mixer(pool_pasts[l], u[..., :POOL_WIDTH], pool_start, pool_w[l], pool_scale[l])
        conv_out, conv_buf = _conv_module(conv_pasts[l], u[..., POOL_WIDTH:], conv_dw_w[l], conv_dw_b[l],
                                          conv_ln_g[l], conv_ln_b[l], conv_pw_w[l])
        h = h + jnp.concatenate([pool_out, conv_out], axis=-1) @ w_out[l]
        hn = _rmsnorm(h, norm2_g[l])
        if l % 2 == 0:
            i = l // 2
            h = h + _swiglu(hn, ffn_w_gate[i], ffn_w_up[i], ffn_w_down[i])
        else:
            i = l // 2
            h = h + _moe(hn, router_w[i], moe_w_gate[i], moe_w_up[i], moe_w_down[i])
        new_pool.append(pool_buf)
        new_conv.append(conv_buf)
    return h, jnp.stack(new_pool), jnp.stack(new_conv)


def setup_inputs(seed: int = 0) -> dict:
    key = jax.random.key(seed)
    ks = jax.random.split(key, 24)
    f32 = jnp.float32

    def nrm(k, shape, scale):
        return jax.random.normal(k, shape, f32) * scale

    return {
        'x_prompt': nrm(ks[0], (BATCH, SEQ, D_MODEL), 1.0),
        'x_sample': nrm(ks[1], (DEC_BATCH, DEC_SEQ, D_MODEL), 1.0),
        'state_pool': nrm(ks[2], (DEPTH, DEC_BATCH, POOL_BUF, POOL_WIDTH), 1.0),
        'state_conv': nrm(ks[3], (DEPTH, DEC_BATCH, CONV_BUF, CONV_WIDTH), 0.5),
        'meta_tokens': nrm(ks[4], (N_META, D_MODEL), 1.0),
        'norm1_g': 1.0 + nrm(ks[5], (DEPTH, D_MODEL), 0.02),
        'w_in': nrm(ks[6], (DEPTH, D_MODEL, IN_COLS), D_MODEL ** -0.5),
        'pool_w': nrm(ks[7], (DEPTH, N_POOL_GROUPS, POOL_GROUP, POOL_GROUP), POOL_GROUP ** -0.5),
        'pool_scale': 1.0 + nrm(ks[8], (DEPTH, POOL_WIDTH), 0.02),
        'conv_dw_w': nrm(ks[9], (DEPTH, CONV_KERNEL, CONV_WIDTH), CONV_KERNEL ** -0.5),
        'conv_dw_b': nrm(ks[10], (DEPTH, CONV_WIDTH), 0.02),
        'conv_ln_g': 1.0 + nrm(ks[11], (DEPTH, CONV_WIDTH), 0.02),
        'conv_ln_b': nrm(ks[12], (DEPTH, CONV_WIDTH), 0.02),
        'conv_pw_w': nrm(ks[13], (DEPTH, CONV_WIDTH, CONV_WIDTH), CONV_WIDTH ** -0.5),
        'w_out': nrm(ks[14], (DEPTH, MIX_WIDTH, D_MODEL), MIX_WIDTH ** -0.5),
        'norm2_g': 1.0 + nrm(ks[15], (DEPTH, D_MODEL), 0.02),
        'ffn_w_gate': nrm(ks[16], (N_DENSE, D_MODEL, D_FF), D_MODEL ** -0.5),
        'ffn_w_up': nrm(ks[17], (N_DENSE, D_MODEL, D_FF), D_MODEL ** -0.5),
        'ffn_w_down': nrm(ks[18], (N_DENSE, D_FF, D_MODEL), D_FF ** -0.5),
        'router_w': nrm(ks[19], (N_MOE, D_MODEL, N_EXPERTS), D_MODEL ** -0.5),
        'moe_w_gate': nrm(ks[20], (N_MOE, N_EXPERTS, D_MODEL, EXPERT_FF), D_MODEL ** -0.5),
        'moe_w_up': nrm(ks[21], (N_MOE, N_EXPERTS, D_MODEL, EXPERT_FF), D_MODEL ** -0.5),
        'moe_w_down': nrm(ks[22], (N_MOE, N_EXPERTS, EXPERT_FF, D_MODEL), EXPERT_FF ** -0.5),
        'final_norm_g': 1.0 + nrm(ks[23], (D_MODEL,), 0.02),
    }


def reference(x_prompt, x_sample, state_pool, state_conv, meta_tokens, norm1_g, w_in, pool_w, pool_scale,
              conv_dw_w, conv_dw_b, conv_ln_g, conv_ln_b, conv_pw_w, w_out, norm2_g, ffn_w_gate, ffn_w_up,
              ffn_w_down, router_w, moe_w_gate, moe_w_up, moe_w_down, final_norm_g):
    weights = (norm1_g, w_in, pool_w, pool_scale, conv_dw_w, conv_dw_b, conv_ln_g, conv_ln_b, conv_pw_w,
               w_out, norm2_g, ffn_w_gate, ffn_w_up, ffn_w_down, router_w, moe_w_gate, moe_w_up, moe_w_down)
    b = x_prompt.shape[0]
    meta = jnp.broadcast_to(meta_tokens[None].astype(x_prompt.dtype), (b, N_META, D_MODEL))
    hp = jnp.concatenate([meta, x_prompt], axis=1)
    zp = jnp.zeros((DEPTH, b, 0, POOL_WIDTH), x_prompt.dtype)
    zc = jnp.zeros((DEPTH, b, CONV_BUF, CONV_WIDTH), x_prompt.dtype)
    hp, new_pool_p, new_conv_p = _trunk(hp, zp, zc, 0, *weights)
    y_prompt = _rmsnorm(hp[:, N_META:], final_norm_g)
    hs, new_pool_s, new_conv_s = _trunk(x_sample, state_pool, state_conv, PAST_LEN - POOL_BUF, *weights)
    y_sample = _rmsnorm(hs, final_norm_g)
    return (y_prompt, y_sample, new_pool_p, new_conv_p, new_pool_s, new_conv_s)
```

```python
import functools

import jax
import jax.numpy as jnp
from jax import lax
from jax.experimental import pallas as pl
from jax.experimental.pallas import tpu as pltpu

D_MODEL = 1024
N_META = 16
POOL_WIDTH = 512
CONV_WIDTH = 512
POOL_WINDOWS = (2, 4, 8, 16)
POOL_GROUP = 128
POOL_BUF = 15
CONV_KERNEL = 31
CONV_BUF = 30
IN_COLS = POOL_WIDTH + 2 * CONV_WIDTH
N_EXPERTS = 8
RMS_EPS = 1e-6
LN_EPS = 1e-5

POOL_HALO = 16
CONV_HALO = 32
ROUTE_LANES = 128

ROW_TILE = 512
DEC_ROW_TILE = 256
MIX_CHUNK = 64
FFN_CHUNK = 512

MOE_ROW_TILE = 1024
MOE_SUB = 256
MOE_FF_TILE = 512

VMEM_LIMIT = 56 * 1024 * 1024

_BF16 = jnp.bfloat16
_F32 = jnp.float32


def _dot(a, b):
    return jnp.dot(a, b, preferred_element_type=_F32)


def _sigmoid(x):
    return 1.0 / (1.0 + jnp.exp(-x))


def _rmsnorm(x, g):
    return x * lax.rsqrt(jnp.mean(x * x, axis=-1, keepdims=True) + RMS_EPS) * g


def _mix_chunk(ep_ref, ec_ref, r0, rows, dw_ref, dwb_ref, lng_ref, lnb_ref, pos0):
    pooled_cols = []
    for gi, win in enumerate(POOL_WINDOWS):
        c0 = gi * POOL_GROUP
        base = POOL_HALO + r0
        x0 = ep_ref[base:base + rows, c0:c0 + POOL_GROUP]
        s = x0
        for k in range(1, win):
            s = s + ep_ref[base - k:base - k + rows, c0:c0 + POOL_GROUP]
        if pos0 is None:
            pooled = s * (1.0 / win) - x0
        else:
            pos = pos0 + r0 + lax.broadcasted_iota(jnp.int32, (rows, POOL_GROUP), 0)
            cnt = jnp.minimum(pos + 1, win).astype(_F32)
            pooled = s / cnt - x0
        pooled_cols.append(pooled)
    pooled = jnp.concatenate(pooled_cols, axis=-1)

    conv_cols = []
    for cg in range(CONV_WIDTH // 128):
        c0 = cg * 128
        base = CONV_HALO - CONV_BUF + r0
        acc = ec_ref[base:base + rows, c0:c0 + 128] * dw_ref[0:1, c0:c0 + 128]
        for k in range(1, CONV_KERNEL):
            acc = acc + ec_ref[base + k:base + k + rows, c0:c0 + 128] * dw_ref[k:k + 1, c0:c0 + 128]
        conv_cols.append(acc + dwb_ref[:, c0:c0 + 128])
    y = jnp.concatenate(conv_cols, axis=-1)
    mu = jnp.mean(y, axis=-1, keepdims=True)
    d = y - mu
    var = jnp.mean(d * d, axis=-1, keepdims=True)
    yn = d * lax.rsqrt(var + LN_EPS) * lng_ref[...] + lnb_ref[...]
    return pooled, yn * _sigmoid(yn)


def _front_kernel(*refs, mode, ramp, tail, tile, n_t):
    it = iter(refs)
    h_ref, ppast_ref, cpast_ref = next(it), next(it), next(it)
    g1_ref, win_ref, wpa_ref, wpb_ref, psc_ref = next(it), next(it), next(it), next(it), next(it)
    dw_ref, dwb_ref, lng_ref, lnb_ref, pw_ref, wout_ref, g2_ref = (next(it) for _ in range(7))
    if tail == "ffn":
        wg_ref, wu_ref, wd_ref = next(it), next(it), next(it)
    elif tail == "route":
        rw_ref = next(it)
    hout_ref = next(it)
    if mode == "seq":
        pst_ref, cst_ref = next(it), next(it)
    else:
        up_ref, v_ref = next(it), next(it)
    if tail == "route":
        hn2_ref, route_ref = next(it), next(it)
    ep_ref, ec_ref, mixed_ref = next(it), next(it), next(it)
    if mode == "dec":
        sp_ref, sc_ref = next(it), next(it)

    x = h_ref[...]
    hn = _rmsnorm(x, g1_ref[...]).astype(_BF16)
    u = _dot(hn, win_ref[...])
    up = u[:, :POOL_WIDTH]
    v = u[:, POOL_WIDTH:POOL_WIDTH + CONV_WIDTH] * _sigmoid(u[:, POOL_WIDTH + CONV_WIDTH:])
    ep_ref[POOL_HALO:, :] = up
    ec_ref[CONV_HALO:, :] = v

    if mode == "seq":
        t = pl.program_id(1)

        @pl.when(t == 0)
        def _():
            ep_ref[0:POOL_HALO, :] = ppast_ref[0]
            ec_ref[0:CONV_HALO, :] = cpast_ref[0]

        chunk = min(MIX_CHUNK, tile)
        for r0 in range(0, tile, chunk):
            pos0 = t * tile if ramp else None
            pooled, act = _mix_chunk(ep_ref, ec_ref, r0, chunk, dw_ref, dwb_ref, lng_ref, lnb_ref, pos0)
            mixed_ref[r0:r0 + chunk, 0:POOL_WIDTH] = pooled
            mixed_ref[r0:r0 + chunk, POOL_WIDTH:] = act
        ptail = ep_ref[tile:tile + POOL_HALO, :]
        ctail = ec_ref[tile:tile + CONV_HALO, :]
        ep_ref[0:POOL_HALO, :] = ptail
        ec_ref[0:CONV_HALO, :] = ctail
        pst_ref[0] = ptail
        cst_ref[0] = ctail
    else:
        up_ref[...] = up
        v_ref[...] = v

        def seq_body(s, carry):
            r = pl.multiple_of(s * 8, 8)
            sp_ref[0:POOL_HALO, :] = ppast_ref[s]
            sp_ref[POOL_HALO:, :] = ep_ref[pl.ds(POOL_HALO + r, 8), :]
            sc_ref[0:CONV_HALO, :] = cpast_ref[s]
            sc_ref[CONV_HALO:, :] = ec_ref[pl.ds(CONV_HALO + r, 8), :]
            pooled, act = _mix_chunk(sp_ref, sc_ref, 0, 8, dw_ref, dwb_ref, lng_ref, lnb_ref, None)
            mixed_ref[pl.ds(r, 8), 0:POOL_WIDTH] = pooled
            mixed_ref[pl.ds(r, 8), POOL_WIDTH:] = act
            return carry

        lax.fori_loop(0, tile // 8, seq_body, 0)

    pa = _dot(mixed_ref[:, 0:256].astype(_BF16), wpa_ref[...])
    pb = _dot(mixed_ref[:, 256:512].astype(_BF16), wpb_ref[...])
    pool_out = jnp.concatenate([pa, pb], axis=-1) * psc_ref[...]
    conv_out = _dot(mixed_ref[:, POOL_WIDTH:].astype(_BF16), pw_ref[...])
    h1 = (x + _dot(pool_out.astype(_BF16), wout_ref[0:POOL_WIDTH, :])
          + _dot(conv_out.astype(_BF16), wout_ref[POOL_WIDTH:, :]))

    if tail == "none":
        hout_ref[...] = h1
        return
    hn2 = _rmsnorm(h1, g2_ref[...])
    if tail == "ffn":
        hb = hn2.astype(_BF16)
        d_ff = wg_ref.shape[1]
        hout_ref[...] = h1
        for c0 in range(0, d_ff, FFN_CHUNK):
            cs = min(FFN_CHUNK, d_ff - c0)
            g = _dot(hb, wg_ref[:, c0:c0 + cs])
            a = (g * _sigmoid(g) * _dot(hb, wu_ref[:, c0:c0 + cs])).astype(_BF16)
            hout_ref[...] += _dot(a, wd_ref[c0:c0 + cs, :])
    else:
        hout_ref[...] = h1
        hn2_ref[...] = hn2
        logits = _dot(hn2.astype(_BF16), rw_ref[...])
        lane = lax.broadcasted_iota(jnp.int32, logits.shape, 1).astype(_F32)
        neg = jnp.float32(-jnp.inf)
        lg = jnp.where(lane < N_EXPERTS, logits, neg)
        m1 = jnp.max(lg, axis=-1, keepdims=True)
        i1 = jnp.min(jnp.where(lg == m1, lane, float(ROUTE_LANES)), axis=-1, keepdims=True)
        lg2 = jnp.where(lane == i1, neg, lg)
        m2 = jnp.max(lg2, axis=-1, keepdims=True)
        i2 = jnp.min(jnp.where(lg2 == m2, lane, float(ROUTE_LANES)), axis=-1, keepdims=True)
        e21 = jnp.exp(m2 - m1)
        gate1 = 1.0 / (1.0 + e21)
        gate2 = e21 * gate1
        route_ref[...] = jnp.where(lane == 0, i1, jnp.where(lane == 1, i2, jnp.where(
            lane == 2, gate1, jnp.where(lane == 3, gate2, 0.0))))


def _const_spec(shape):
    nd = len(shape)
    return pl.BlockSpec(shape, lambda *_: (0,) * nd, pipeline_mode=pl.Buffered(1))


def _front_call(h, ppast, cpast, lw, *, mode, ramp, tail, tile, n_seq, n_t):
    rows = h.shape[0]
    weights = [lw["g1"], lw["w_in"], lw["wpa"], lw["wpb"], lw["pscale"], lw["dw"], lw["dwb"], lw["lng"],
               lw["lnb"], lw["pw"], lw["w_out"], lw["g2"]]
    if tail == "ffn":
        weights += [lw["wg"], lw["wu"], lw["wd"]]
    elif tail == "route":
        weights += [lw["rw"]]
    w_specs = [_const_spec(w.shape) for w in weights]

    if mode == "seq":
        grid = (n_seq, n_t)
        row_map = lambda b, t: (b * n_t + t, 0)
        shared = ppast.shape[0] == 1
        past_map = (lambda b, t: (0, 0, 0)) if shared else (lambda b, t: (b, 0, 0))
        past_specs = [pl.BlockSpec((1, POOL_HALO, POOL_WIDTH), past_map),
                      pl.BlockSpec((1, CONV_HALO, CONV_WIDTH), past_map)]
        out_shape = [jax.ShapeDtypeStruct((rows, D_MODEL), _F32),
                     jax.ShapeDtypeStruct((n_seq, POOL_HALO, POOL_WIDTH), _F32),
                     jax.ShapeDtypeStruct((n_seq, CONV_HALO, CONV_WIDTH), _F32)]
        out_specs = [pl.BlockSpec((tile, D_MODEL), row_map),
                     pl.BlockSpec((1, POOL_HALO, POOL_WIDTH), lambda b, t: (b, 0, 0)),
                     pl.BlockSpec((1, CONV_HALO, CONV_WIDTH), lambda b, t: (b, 0, 0))]
        semantics = ("arbitrary", "arbitrary")
    else:
        grid = (rows // tile,)
        row_map = lambda i: (i, 0)
        spt = tile // 8
        past_specs = [pl.BlockSpec((spt, POOL_HALO, POOL_WIDTH), lambda i: (i, 0, 0)),
                      pl.BlockSpec((spt, CONV_HALO, CONV_WIDTH), lambda i: (i, 0, 0))]
        out_shape = [jax.ShapeDtypeStruct((rows, D_MODEL), _F32),
                     jax.ShapeDtypeStruct((rows, POOL_WIDTH), _F32),
                     jax.ShapeDtypeStruct((rows, CONV_WIDTH), _F32)]
        out_specs = [pl.BlockSpec((tile, D_MODEL), row_map),
                     pl.BlockSpec((tile, POOL_WIDTH), row_map),
                     pl.BlockSpec((tile, CONV_WIDTH), row_map)]
        semantics = ("arbitrary",)
    if tail == "route":
        out_shape += [jax.ShapeDtypeStruct((rows, D_MODEL), _F32),
                      jax.ShapeDtypeStruct((rows, ROUTE_LANES), _F32)]
        out_specs += [pl.BlockSpec((tile, D_MODEL), row_map), pl.BlockSpec((tile, ROUTE_LANES), row_map)]

    scratch = [pltpu.VMEM((POOL_HALO + tile, POOL_WIDTH), _F32),
               pltpu.VMEM((CONV_HALO + tile, CONV_WIDTH), _F32),
               pltpu.VMEM((tile, POOL_WIDTH + CONV_WIDTH), _F32)]
    if mode == "dec":
        scratch += [pltpu.VMEM((POOL_HALO + 8, POOL_WIDTH), _F32), pltpu.VMEM((CONV_HALO + 8, CONV_WIDTH), _F32)]

    body = functools.partial(_front_kernel, mode=mode, ramp=ramp, tail=tail, tile=tile, n_t=n_t)
    return pl.pallas_call(
        body,
        grid=grid,
        in_specs=[pl.BlockSpec((tile, D_MODEL), row_map)] + past_specs + w_specs,
        out_specs=out_specs,
        out_shape=out_shape,
        scratch_shapes=scratch,
        compiler_params=pltpu.CompilerParams(dimension_semantics=semantics, vmem_limit_bytes=VMEM_LIMIT),
        name=f"front_{mode}_{tail}",
    )(h, ppast, cpast, *weights)


def _dispatch_kernel(pos_ref, src_hbm, dst_in_hbm, dst_hbm, sem, *, tile):
    del dst_in_hbm
    base = pl.program_id(0) * tile

    def row_copy(r, k):
        p = pos_ref[(base + r) * 2 + k]
        return pltpu.make_async_copy(src_hbm.at[pl.ds(base + r, 1)], dst_hbm.at[pl.ds(p, 1)], sem)

    def start(r, c):
        row_copy(r, 0).start()
        row_copy(r, 1).start()
        return c

    def wait(r, c):
        row_copy(r, 0).wait()
        row_copy(r, 1).wait()
        return c

    lax.fori_loop(0, tile, start, 0)
    lax.fori_loop(0, tile, wait, 0)


def _dispatch_call(pos_flat, src, dst, *, tile):
    rows = src.shape[0]
    return pl.pallas_call(
        functools.partial(_dispatch_kernel, tile=tile),
        grid_spec=pltpu.PrefetchScalarGridSpec(
            num_scalar_prefetch=1,
            grid=(rows // tile,),
            in_specs=[pl.BlockSpec(memory_space=pl.ANY), pl.BlockSpec(memory_space=pl.ANY)],
            out_specs=pl.BlockSpec(memory_space=pl.ANY),
            scratch_shapes=[pltpu.SemaphoreType.DMA(())],
        ),
        out_shape=jax.ShapeDtypeStruct(dst.shape, dst.dtype),
        input_output_aliases={2: 0},
        compiler_params=pltpu.CompilerParams(dimension_semantics=("arbitrary",), has_side_effects=True),
        name="moe_dispatch",
    )(pos_flat, src, dst)


def _moe_kernel(te_ref, nv_ref, nu_ref, x_ref, wg_ref, wu_ref, wd_ref, o_ref, xb_ref, wgb_ref, wub_ref, wdb_ref):
    del te_ref, nu_ref
    i = pl.program_id(0)
    j = pl.program_id(1)
    nv = nv_ref[i]

    @pl.when(nv > 0)
    def _():
        @pl.when(j == 0)
        def _():
            xb_ref[...] = x_ref[...].astype(_BF16)

        wgb_ref[...] = wg_ref[0].astype(_BF16)
        wub_ref[...] = wu_ref[0].astype(_BF16)
        wdb_ref[...] = wd_ref[0].astype(_BF16)

    for s in range(MOE_ROW_TILE // MOE_SUB):
        rows = slice(s * MOE_SUB, (s + 1) * MOE_SUB)

        @pl.when(nv > s * MOE_SUB)
        def _():
            xs = xb_ref[rows, :]
            g = _dot(xs, wgb_ref[...])
            a = (g * _sigmoid(g) * _dot(xs, wub_ref[...])).astype(_BF16)
            d = _dot(a, wdb_ref[...])

            @pl.when(j == 0)
            def _():
                o_ref[rows, :] = d

            @pl.when(j > 0)
            def _():
                o_ref[rows, :] += d

        @pl.when(jnp.logical_and(nv <= s * MOE_SUB, j == 0))
        def _():
            o_ref[rows, :] = jnp.zeros((MOE_SUB, D_MODEL), _F32)


def _moe_call(tile_expert, tile_valid, n_used, xs, wg, wu, wd):
    n_tiles = xs.shape[0] // MOE_ROW_TILE
    eff = wg.shape[2]
    n_j = eff // MOE_FF_TILE

    def row_map(i, j, te, nv, nu):
        return (jnp.maximum(jnp.minimum(i, nu[0] - 1), 0), 0)

    def col_of(i, j, nu):
        return jnp.where(i < nu[0], j, n_j - 1)

    return pl.pallas_call(
        _moe_kernel,
        grid_spec=pltpu.PrefetchScalarGridSpec(
            num_scalar_prefetch=3,
            grid=(n_tiles, n_j),
            in_specs=[
                pl.BlockSpec((MOE_ROW_TILE, D_MODEL), row_map),
                pl.BlockSpec((1, D_MODEL, MOE_FF_TILE), lambda i, j, te, nv, nu: (te[i], 0, col_of(i, j, nu))),
                pl.BlockSpec((1, D_MODEL, MOE_FF_TILE), lambda i, j, te, nv, nu: (te[i], 0, col_of(i, j, nu))),
                pl.BlockSpec((1, MOE_FF_TILE, D_MODEL), lambda i, j, te, nv, nu: (te[i], col_of(i, j, nu), 0)),
            ],
            out_specs=pl.BlockSpec((MOE_ROW_TILE, D_MODEL), lambda i, j, te, nv, nu: (i, 0)),
            scratch_shapes=[
                pltpu.VMEM((MOE_ROW_TILE, D_MODEL), _BF16),
                pltpu.VMEM((D_MODEL, MOE_FF_TILE), _BF16),
                pltpu.VMEM((D_MODEL, MOE_FF_TILE), _BF16),
                pltpu.VMEM((MOE_FF_TILE, D_MODEL), _BF16),
            ],
        ),
        out_shape=jax.ShapeDtypeStruct(xs.shape, _F32),
        compiler_params=pltpu.CompilerParams(
            dimension_semantics=("arbitrary", "arbitrary"), vmem_limit_bytes=VMEM_LIMIT),
        name="moe_grouped",
    )(tile_expert, tile_valid, n_used, xs, wg, wu, wd)


def _combine_kernel(pos_ref, h_ref, route_ref, fg_ref, ys_hbm, o_ref, buf_ref, sem, *, tile):
    i = pl.program_id(0)
    n = pl.num_programs(0)

    def row_copy(step, slot, r, k):
        p = pos_ref[(step * tile + r) * 2 + k]
        return pltpu.make_async_copy(ys_hbm.at[pl.ds(p, 1)], buf_ref.at[slot, k, pl.ds(r, 1)], sem.at[slot])

    def start_tile(step, slot):
        def body(r, c):
            row_copy(step, slot, r, 0).start()
            row_copy(step, slot, r, 1).start()
            return c
        lax.fori_loop(0, tile, body, 0)

    def wait_tile(step, slot):
        def body(r, c):
            row_copy(step, slot, r, 0).wait()
            row_copy(step, slot, r, 1).wait()
            return c
        lax.fori_loop(0, tile, body, 0)

    slot = i % 2

    @pl.when(i == 0)
    def _():
        start_tile(0, 0)

    @pl.when(i + 1 < n)
    def _():
        start_tile(i + 1, 1 - slot)

    wait_tile(i, slot)
    g1 = route_ref[:, 2:3]
    g2 = route_ref[:, 3:4]
    h2 = h_ref[...] + (g1 * buf_ref[slot, 0] + g2 * buf_ref[slot, 1])
    o_ref[...] = _rmsnorm(h2, fg_ref[...])


def _combine_call(pos_flat, h, route, fg, ys, *, tile):
    rows = h.shape[0]
    return pl.pallas_call(
        functools.partial(_combine_kernel, tile=tile),
        grid_spec=pltpu.PrefetchScalarGridSpec(
            num_scalar_prefetch=1,
            grid=(rows // tile,),
            in_specs=[
                pl.BlockSpec((tile, D_MODEL), lambda i, pos: (i, 0)),
                pl.BlockSpec((tile, ROUTE_LANES), lambda i, pos: (i, 0)),
                pl.BlockSpec((1, D_MODEL), lambda i, pos: (0, 0)),
                pl.BlockSpec(memory_space=pl.ANY),
            ],
            out_specs=pl.BlockSpec((tile, D_MODEL), lambda i, pos: (i, 0)),
            scratch_shapes=[pltpu.VMEM((2, 2, tile, D_MODEL), _F32), pltpu.SemaphoreType.DMA((2,))],
        ),
        out_shape=jax.ShapeDtypeStruct((rows, D_MODEL), _F32),
        compiler_params=pltpu.CompilerParams(
            dimension_semantics=("arbitrary",), vmem_limit_bytes=VMEM_LIMIT),
        name="moe_combine",
    )(pos_flat, h, route, fg, ys)


def _layer_weights(l, norm1_g, w_in, pool_w, pool_scale, conv_dw_w, conv_dw_b, conv_ln_g, conv_ln_b,
                   conv_pw_w, w_out, norm2_g):
    zero = jnp.zeros((POOL_GROUP, POOL_GROUP), _F32)
    pwl = pool_w[l]
    wpa = jnp.block([[pwl[0], zero], [zero, pwl[1]]])
    wpb = jnp.block([[pwl[2], zero], [zero, pwl[3]]])
    return {
        "g1": norm1_g[l][None, :], "w_in": w_in[l].astype(_BF16),
        "wpa": wpa.astype(_BF16), "wpb": wpb.astype(_BF16), "pscale": pool_scale[l][None, :],
        "dw": conv_dw_w[l], "dwb": conv_dw_b[l][None, :], "lng": conv_ln_g[l][None, :],
        "lnb": conv_ln_b[l][None, :], "pw": conv_pw_w[l].astype(_BF16), "w_out": w_out[l].astype(_BF16),
        "g2": norm2_g[l][None, :],
    }


def _routing_plan(experts):
    n = experts.shape[0]
    flat = experts.reshape(-1)
    onehot = (flat[:, None] == jnp.arange(N_EXPERTS, dtype=jnp.int32)[None, :]).astype(jnp.int32)
    csum = jnp.cumsum(onehot, axis=0)
    counts = csum[-1]
    padded = ((counts + MOE_ROW_TILE - 1) // MOE_ROW_TILE) * MOE_ROW_TILE
    ends = jnp.cumsum(padded)
    offsets = ends - padded
    pos = jnp.sum(onehot * (offsets[None, :] + csum - 1), axis=1).astype(jnp.int32)
    n_tiles = (2 * n) // MOE_ROW_TILE + N_EXPERTS
    starts = jnp.arange(n_tiles, dtype=jnp.int32) * MOE_ROW_TILE
    te = jnp.sum((starts[:, None] >= ends[None, :]).astype(jnp.int32), axis=1)
    n_used = (ends[-1] // MOE_ROW_TILE).astype(jnp.int32)
    te_c = jnp.minimum(te, N_EXPERTS - 1)
    valid = jnp.clip(offsets[te_c] + counts[te_c] - starts, 0, MOE_ROW_TILE)
    valid = jnp.where(te < N_EXPERTS, valid, 0).astype(jnp.int32)
    last_e = te_c[jnp.maximum(n_used - 1, 0)]
    te_c = jnp.where(te < N_EXPERTS, te_c, last_e).astype(jnp.int32)
    return pos, te_c, valid, n_used.reshape(1), n_tiles


def kernel(x_prompt, x_sample, state_pool, state_conv, meta_tokens, norm1_g, w_in, pool_w, pool_scale,
           conv_dw_w, conv_dw_b, conv_ln_g, conv_ln_b, conv_pw_w, w_out, norm2_g, ffn_w_gate, ffn_w_up,
           ffn_w_down, router_w, moe_w_gate, moe_w_up, moe_w_down, final_norm_g):
    batch, seq, _ = x_prompt.shape
    dec_batch, dec_seq, _ = x_sample.shape
    assert dec_seq == 8 and seq % ROW_TILE == 0 and (dec_batch * dec_seq) % ROW_TILE == 0
    assert (dec_batch * dec_seq) % DEC_ROW_TILE == 0
    lws = [_layer_weights(l, norm1_g, w_in, pool_w, pool_scale, conv_dw_w, conv_dw_b, conv_ln_g, conv_ln_b,
                          conv_pw_w, w_out, norm2_g) for l in range(2)]
    lws[0].update(wg=ffn_w_gate[0].astype(_BF16), wu=ffn_w_up[0].astype(_BF16), wd=ffn_w_down[0].astype(_BF16))
    rw = jnp.zeros((D_MODEL, ROUTE_LANES), _F32).at[:, :N_EXPERTS].set(router_w[0])
    lws[1].update(rw=rw.astype(_BF16))

    n_p = batch * seq
    n_s = dec_batch * dec_seq
    n_t = seq // ROW_TILE
    hp = x_prompt.reshape(n_p, D_MODEL)
    hs = x_sample.reshape(n_s, D_MODEL)
    hm = meta_tokens.astype(_F32)
    zero_pp = jnp.zeros((1, POOL_HALO, POOL_WIDTH), _F32)
    zero_cp = jnp.zeros((1, CONV_HALO, CONV_WIDTH), _F32)
    sp_pad = jnp.pad(state_pool, ((0, 0), (0, 0), (POOL_HALO - POOL_BUF, 0), (0, 0)))
    sc_pad = jnp.pad(state_conv, ((0, 0), (0, 0), (CONV_HALO - CONV_BUF, 0), (0, 0)))

    seq_kw = dict(mode="seq", tile=ROW_TILE, n_seq=batch, n_t=n_t)
    meta_kw = dict(mode="seq", ramp=True, tile=N_META, n_seq=1, n_t=1)
    dec_kw = dict(mode="dec", ramp=False, tile=DEC_ROW_TILE, n_seq=0, n_t=0)

    hm, pst_m0, cst_m0 = _front_call(hm, zero_pp, zero_cp, lws[0], tail="ffn", **meta_kw)
    hp, pst_p0, cst_p0 = _front_call(hp, pst_m0, cst_m0, lws[0], ramp=False, tail="ffn", **seq_kw)
    hs, up_s0, v_s0 = _front_call(hs, sp_pad[0], sc_pad[0], lws[0], tail="ffn", **dec_kw)

    _, pst_m1, cst_m1 = _front_call(hm, zero_pp, zero_cp, lws[1], tail="none", **meta_kw)
    hp, pst_p1, cst_p1, hn_p, route_p = _front_call(hp, pst_m1, cst_m1, lws[1], ramp=False, tail="route", **seq_kw)
    hs, up_s1, v_s1, hn_s, route_s = _front_call(hs, sp_pad[1], sc_pad[1], lws[1], tail="route", **dec_kw)

    experts = jnp.concatenate([route_p[:, :2], route_s[:, :2]], axis=0).astype(jnp.int32)
    experts = jnp.clip(experts, 0, N_EXPERTS - 1)
    pos, tile_expert, tile_valid, n_used, n_tiles = _routing_plan(experts)
    pos_p, pos_s = pos[:2 * n_p], pos[2 * n_p:]

    xs = jnp.zeros((n_tiles * MOE_ROW_TILE, D_MODEL), _F32)
    xs = _dispatch_call(pos_p, hn_p, xs, tile=ROW_TILE)
    xs = _dispatch_call(pos_s, hn_s, xs, tile=ROW_TILE)
    ys = _moe_call(tile_expert, tile_valid, n_used, xs, moe_w_gate[0], moe_w_up[0], moe_w_down[0])

    fg = final_norm_g[None, :]
    y_p = _combine_call(pos_p, hp, route_p, fg, ys, tile=ROW_TILE)
    y_s = _combine_call(pos_s, hs, route_s, fg, ys, tile=ROW_TILE)

    new_pool_p = jnp.stack([pst_p0, pst_p1])[:, :, POOL_HALO - POOL_BUF:, :]
    new_conv_p = jnp.stack([cst_p0, cst_p1])[:, :, CONV_HALO - CONV_BUF:, :]
    up_s = jnp.stack([up_s0, up_s1]).reshape(2, dec_batch, dec_seq, POOL_WIDTH)
    v_s = jnp.stack([v_s0, v_s1]).reshape(2, dec_batch, dec_seq, CONV_WIDTH)
    new_pool_s = jnp.concatenate([state_pool, up_s], axis=2)[:, :, -POOL_BUF:, :]
    new_conv_s = jnp.concatenate([state_conv, v_s], axis=2)[:, :, -CONV_BUF:, :]
    return (y_p.reshape(batch, seq, D_MODEL), y_s.reshape(dec_batch, dec_seq, D_MODEL),
            new_pool_p, new_conv_p, new_pool_s, new_conv_s)
```

```python
import functools

import jax
import jax.numpy as jnp
from jax import lax
from jax.experimental import pallas as pl
from jax.experimental.pallas import tpu as pltpu

D_MODEL = 1024
N_META = 16
POOL_WIDTH = 512
CONV_WIDTH = 512
POOL_WINDOWS = (2, 4, 8, 16)
POOL_GROUP = 128
POOL_BUF = 15
CONV_KERNEL = 31
CONV_BUF = 30
IN_COLS = POOL_WIDTH + 2 * CONV_WIDTH
N_EXPERTS = 8
RMS_EPS = 1e-6
LN_EPS = 1e-5

POOL_HALO = 16
CONV_HALO = 32
ROUTE_LANES = 128

ROW_TILE = 512
DEC_ROW_TILE = 256
MIX_CHUNK = 64
FFN_CHUNK = 512

MOE_ROW_TILE = 1024
MOE_SUB = 256
MOE_FF_TILE = 512
SEG_ALIGN = 8
LOCAL_ROWS = -(-(2 * ROW_TILE + (SEG_ALIGN - 1) * N_EXPERTS) // 128) * 128

VMEM_LIMIT = 56 * 1024 * 1024

_BF16 = jnp.bfloat16
_F32 = jnp.float32


def _dot(a, b):
    return jnp.dot(a, b, preferred_element_type=_F32)


def _sigmoid(x):
    return 1.0 / (1.0 + jnp.exp(-x))


def _rmsnorm(x, g):
    return x * lax.rsqrt(jnp.mean(x * x, axis=-1, keepdims=True) + RMS_EPS) * g


def _mix_chunk(ep_ref, ec_ref, r0, rows, dw_ref, dwb_ref, lng_ref, lnb_ref, pos0):
    pooled_cols = []
    for gi, win in enumerate(POOL_WINDOWS):
        c0 = gi * POOL_GROUP
        base = POOL_HALO + r0
        x0 = ep_ref[base:base + rows, c0:c0 + POOL_GROUP]
        s = x0
        for k in range(1, win):
            s = s + ep_ref[base - k:base - k + rows, c0:c0 + POOL_GROUP]
        if pos0 is None:
            pooled = s * (1.0 / win) - x0
        else:
            pos = pos0 + r0 + lax.broadcasted_iota(jnp.int32, (rows, POOL_GROUP), 0)
            cnt = jnp.minimum(pos + 1, win).astype(_F32)
            pooled = s / cnt - x0
        pooled_cols.append(pooled)
    pooled = jnp.concatenate(pooled_cols, axis=-1)

    conv_cols = []
    for cg in range(CONV_WIDTH // 128):
        c0 = cg * 128
        base = CONV_HALO - CONV_BUF + r0
        acc = ec_ref[base:base + rows, c0:c0 + 128] * dw_ref[0:1, c0:c0 + 128]
        for k in range(1, CONV_KERNEL):
            acc = acc + ec_ref[base + k:base + k + rows, c0:c0 + 128] * dw_ref[k:k + 1, c0:c0 + 128]
        conv_cols.append(acc + dwb_ref[:, c0:c0 + 128])
    y = jnp.concatenate(conv_cols, axis=-1)
    mu = jnp.mean(y, axis=-1, keepdims=True)
    d = y - mu
    var = jnp.mean(d * d, axis=-1, keepdims=True)
    yn = d * lax.rsqrt(var + LN_EPS) * lng_ref[...] + lnb_ref[...]
    return pooled, yn * _sigmoid(yn)


def _front_kernel(*refs, mode, ramp, tail, tile, n_t):
    it = iter(refs)
    h_ref, ppast_ref, cpast_ref = next(it), next(it), next(it)
    g1_ref, win_ref, wpa_ref, wpb_ref, psc_ref = next(it), next(it), next(it), next(it), next(it)
    dw_ref, dwb_ref, lng_ref, lnb_ref, pw_ref, wout_ref, g2_ref = (next(it) for _ in range(7))
    if tail == "ffn":
        wg_ref, wu_ref, wd_ref = next(it), next(it), next(it)
    elif tail == "route":
        rw_ref = next(it)
    hout_ref = next(it)
    if mode == "seq":
        pst_ref, cst_ref = next(it), next(it)
    else:
        up_ref, v_ref = next(it), next(it)
    if tail == "route":
        hn2_ref, route_ref = next(it), next(it)
    ep_ref, ec_ref, mixed_ref = next(it), next(it), next(it)
    if mode == "dec":
        sp_ref, sc_ref = next(it), next(it)

    x = h_ref[...]
    hn = _rmsnorm(x, g1_ref[...]).astype(_BF16)
    u = _dot(hn, win_ref[...])
    up = u[:, :POOL_WIDTH]
    v = u[:, POOL_WIDTH:POOL_WIDTH + CONV_WIDTH] * _sigmoid(u[:, POOL_WIDTH + CONV_WIDTH:])
    ep_ref[POOL_HALO:, :] = up
    ec_ref[CONV_HALO:, :] = v

    if mode == "seq":
        t = pl.program_id(1)

        @pl.when(t == 0)
        def _():
            ep_ref[0:POOL_HALO, :] = ppast_ref[0]
            ec_ref[0:CONV_HALO, :] = cpast_ref[0]

        chunk = min(MIX_CHUNK, tile)
        for r0 in range(0, tile, chunk):
            pos0 = t * tile if ramp else None
            pooled, act = _mix_chunk(ep_ref, ec_ref, r0, chunk, dw_ref, dwb_ref, lng_ref, lnb_ref, pos0)
            mixed_ref[r0:r0 + chunk, 0:POOL_WIDTH] = pooled
            mixed_ref[r0:r0 + chunk, POOL_WIDTH:] = act
        ptail = ep_ref[tile:tile + POOL_HALO, :]
        ctail = ec_ref[tile:tile + CONV_HALO, :]
        ep_ref[0:POOL_HALO, :] = ptail
        ec_ref[0:CONV_HALO, :] = ctail
        pst_ref[0] = ptail
        cst_ref[0] = ctail
    else:
        up_ref[...] = up
        v_ref[...] = v

        def seq_body(s, carry):
            r = pl.multiple_of(s * 8, 8)
            sp_ref[0:POOL_HALO, :] = ppast_ref[s]
            sp_ref[POOL_HALO:, :] = ep_ref[pl.ds(POOL_HALO + r, 8), :]
            sc_ref[0:CONV_HALO, :] = cpast_ref[s]
            sc_ref[CONV_HALO:, :] = ec_ref[pl.ds(CONV_HALO + r, 8), :]
            pooled, act = _mix_chunk(sp_ref, sc_ref, 0, 8, dw_ref, dwb_ref, lng_ref, lnb_ref, None)
            mixed_ref[pl.ds(r, 8), 0:POOL_WIDTH] = pooled
            mixed_ref[pl.ds(r, 8), POOL_WIDTH:] = act
            return carry

        lax.fori_loop(0, tile // 8, seq_body, 0)

    pa = _dot(mixed_ref[:, 0:256].astype(_BF16), wpa_ref[...])
    pb = _dot(mixed_ref[:, 256:512].astype(_BF16), wpb_ref[...])
    pool_out = jnp.concatenate([pa, pb], axis=-1) * psc_ref[...]
    conv_out = _dot(mixed_ref[:, POOL_WIDTH:].astype(_BF16), pw_ref[...])
    h1 = (x + _dot(pool_out.astype(_BF16), wout_ref[0:POOL_WIDTH, :])
          + _dot(conv_out.astype(_BF16), wout_ref[POOL_WIDTH:, :]))

    if tail == "none":
        hout_ref[...] = h1
        return
    hn2 = _rmsnorm(h1, g2_ref[...])
    if tail == "ffn":
        hb = hn2.astype(_BF16)
        d_ff = wg_ref.shape[1]
        hout_ref[...] = h1
        for c0 in range(0, d_ff, FFN_CHUNK):
            cs = min(FFN_CHUNK, d_ff - c0)
            g = _dot(hb, wg_ref[:, c0:c0 + cs])
            a = (g * _sigmoid(g) * _dot(hb, wu_ref[:, c0:c0 + cs])).astype(_BF16)
            hout_ref[...] += _dot(a, wd_ref[c0:c0 + cs, :])
    else:
        hout_ref[...] = h1
        hn2_ref[...] = hn2
        logits = _dot(hn2.astype(_BF16), rw_ref[...])
        lane = lax.broadcasted_iota(jnp.int32, logits.shape, 1).astype(_F32)
        neg = jnp.float32(-jnp.inf)
        lg = jnp.where(lane < N_EXPERTS, logits, neg)
        m1 = jnp.max(lg, axis=-1, keepdims=True)
        i1 = jnp.min(jnp.where(lg == m1, lane, float(ROUTE_LANES)), axis=-1, keepdims=True)
        lg2 = jnp.where(lane == i1, neg, lg)
        m2 = jnp.max(lg2, axis=-1, keepdims=True)
        i2 = jnp.min(jnp.where(lg2 == m2, lane, float(ROUTE_LANES)), axis=-1, keepdims=True)
        e21 = jnp.exp(m2 - m1)
        gate1 = 1.0 / (1.0 + e21)
        gate2 = e21 * gate1
        route_ref[...] = jnp.where(lane == 0, i1, jnp.where(lane == 1, i2, jnp.where(
            lane == 2, gate1, jnp.where(lane == 3, gate2, 0.0))))


def _const_spec(shape):
    nd = len(shape)
    return pl.BlockSpec(shape, lambda *_: (0,) * nd, pipeline_mode=pl.Buffered(1))


def _front_call(h, ppast, cpast, lw, *, mode, ramp, tail, tile, n_seq, n_t):
    rows = h.shape[0]
    weights = [lw["g1"], lw["w_in"], lw["wpa"], lw["wpb"], lw["pscale"], lw["dw"], lw["dwb"], lw["lng"],
               lw["lnb"], lw["pw"], lw["w_out"], lw["g2"]]
    if tail == "ffn":
        weights += [lw["wg"], lw["wu"], lw["wd"]]
    elif tail == "route":
        weights += [lw["rw"]]
    w_specs = [_const_spec(w.shape) for w in weights]

    if mode == "seq":
        grid = (n_seq, n_t)
        row_map = lambda b, t: (b * n_t + t, 0)
        shared = ppast.shape[0] == 1
        past_map = (lambda b, t: (0, 0, 0)) if shared else (lambda b, t: (b, 0, 0))
        past_specs = [pl.BlockSpec((1, POOL_HALO, POOL_WIDTH), past_map),
                      pl.BlockSpec((1, CONV_HALO, CONV_WIDTH), past_map)]
        out_shape = [jax.ShapeDtypeStruct((rows, D_MODEL), _F32),
                     jax.ShapeDtypeStruct((n_seq, POOL_HALO, POOL_WIDTH), _F32),
                     jax.ShapeDtypeStruct((n_seq, CONV_HALO, CONV_WIDTH), _F32)]
        out_specs = [pl.BlockSpec((tile, D_MODEL), row_map),
                     pl.BlockSpec((1, POOL_HALO, POOL_WIDTH), lambda b, t: (b, 0, 0)),
                     pl.BlockSpec((1, CONV_HALO, CONV_WIDTH), lambda b, t: (b, 0, 0))]
        semantics = ("arbitrary", "arbitrary")
    else:
        grid = (rows // tile,)
        row_map = lambda i: (i, 0)
        spt = tile // 8
        past_specs = [pl.BlockSpec((spt, POOL_HALO, POOL_WIDTH), lambda i: (i, 0, 0)),
                      pl.BlockSpec((spt, CONV_HALO, CONV_WIDTH), lambda i: (i, 0, 0))]
        out_shape = [jax.ShapeDtypeStruct((rows, D_MODEL), _F32),
                     jax.ShapeDtypeStruct((rows, POOL_WIDTH), _F32),
                     jax.ShapeDtypeStruct((rows, CONV_WIDTH), _F32)]
        out_specs = [pl.BlockSpec((tile, D_MODEL), row_map),
                     pl.BlockSpec((tile, POOL_WIDTH), row_map),
                     pl.BlockSpec((tile, CONV_WIDTH), row_map)]
        semantics = ("arbitrary",)
    if tail == "route":
        out_shape += [jax.ShapeDtypeStruct((rows, D_MODEL), _F32),
                      jax.ShapeDtypeStruct((rows, ROUTE_LANES), _F32)]
        out_specs += [pl.BlockSpec((tile, D_MODEL), row_map), pl.BlockSpec((tile, ROUTE_LANES), row_map)]

    scratch = [pltpu.VMEM((POOL_HALO + tile, POOL_WIDTH), _F32),
               pltpu.VMEM((CONV_HALO + tile, CONV_WIDTH), _F32),
               pltpu.VMEM((tile, POOL_WIDTH + CONV_WIDTH), _F32)]
    if mode == "dec":
        scratch += [pltpu.VMEM((POOL_HALO + 8, POOL_WIDTH), _F32), pltpu.VMEM((CONV_HALO + 8, CONV_WIDTH), _F32)]

    body = functools.partial(_front_kernel, mode=mode, ramp=ramp, tail=tail, tile=tile, n_t=n_t)
    return pl.pallas_call(
        body,
        grid=grid,
        in_specs=[pl.BlockSpec((tile, D_MODEL), row_map)] + past_specs + w_specs,
        out_specs=out_specs,
        out_shape=out_shape,
        scratch_shapes=scratch,
        compiler_params=pltpu.CompilerParams(dimension_semantics=semantics, vmem_limit_bytes=VMEM_LIMIT),
        name=f"front_{mode}_{tail}",
    )(h, ppast, cpast, *weights)


def _segment_copies(c8_ref, lo_ref, segoff_ref, tidx, make_copy, act):
    for e in range(N_EXPERTS):
        k = tidx * N_EXPERTS + e
        lo_e = lo_ref[k]
        so_e = segoff_ref[k]

        def body(c, carry):
            act(make_copy(pl.multiple_of(lo_e + c * SEG_ALIGN, SEG_ALIGN),
                          pl.multiple_of(so_e + c * SEG_ALIGN, SEG_ALIGN)))
            return carry

        lax.fori_loop(0, c8_ref[k] // SEG_ALIGN, body, 0)


def _dispatch_kernel(segoff_ref, c8_ref, lo_ref, hn_ref, route_ref, xs_in_hbm, xs_hbm, lpos_ref, xl_ref, sem,
                     *, tile, tile0):
    del xs_in_hbm
    t = pl.program_id(0)
    n = pl.num_programs(0)
    slot = t % 2
    tt = tile0 + t
    lane = lax.broadcasted_iota(jnp.int32, (tile, ROUTE_LANES), 1).astype(_F32)
    oh0 = (lane == route_ref[:, 0:1]).astype(_F32)
    oh1 = (lane == route_ref[:, 1:2]).astype(_F32)
    earlier = (lax.broadcasted_iota(jnp.int32, (tile, tile), 1)
               < lax.broadcasted_iota(jnp.int32, (tile, tile), 0)).astype(_BF16)
    rank = _dot(earlier, (oh0 + oh1).astype(_BF16))
    lane1 = lax.broadcasted_iota(jnp.int32, (1, ROUTE_LANES), 1)
    lo_vec = jnp.zeros((1, ROUTE_LANES), _F32)
    for e in range(N_EXPERTS):
        lo_vec = jnp.where(lane1 == e, lo_ref[tt * N_EXPERTS + e].astype(_F32), lo_vec)
    posv = rank + lo_vec
    p0 = jnp.sum(oh0 * posv, axis=-1, keepdims=True)
    p1 = jnp.sum(oh1 * posv, axis=-1, keepdims=True)
    slab = jnp.where(lane == 0, p0, jnp.where(lane == 1, p1, 0.0))
    lpos_ref[...] = slab
    slab_t = slab.T
    rowi = lax.broadcasted_iota(jnp.int32, (LOCAL_ROWS, tile), 0).astype(_F32)
    place = jnp.where(jnp.logical_or(rowi == slab_t[0:1, :], rowi == slab_t[1:2, :]), 1.0, 0.0).astype(_BF16)
    xl_ref[slot] = _dot(place, hn_ref[...].astype(_BF16))

    def copies(tidx, s):
        def make_copy(lo, so):
            return pltpu.make_async_copy(xl_ref.at[s, pl.ds(lo, SEG_ALIGN)], xs_hbm.at[pl.ds(so, SEG_ALIGN)],
                                         sem.at[s])
        return functools.partial(_segment_copies, c8_ref, lo_ref, segoff_ref, tidx, make_copy)

    copies(tt, slot)(lambda c: c.start())

    @pl.when(t > 0)
    def _():
        copies(tt - 1, 1 - slot)(lambda c: c.wait())

    @pl.when(t == n - 1)
    def _():
        copies(tt, slot)(lambda c: c.wait())


def _dispatch_call(plan, hn, route, xs, *, tile, tile0):
    rows = hn.shape[0]
    smap = lambda i, *_: (i, 0)
    return pl.pallas_call(
        functools.partial(_dispatch_kernel, tile=tile, tile0=tile0),
        grid_spec=pltpu.PrefetchScalarGridSpec(
            num_scalar_prefetch=3,
            grid=(rows // tile,),
            in_specs=[pl.BlockSpec((tile, D_MODEL), smap), pl.BlockSpec((tile, ROUTE_LANES), smap),
                      pl.BlockSpec(memory_space=pl.ANY)],
            out_specs=[pl.BlockSpec(memory_space=pl.ANY), pl.BlockSpec((tile, ROUTE_LANES), smap)],
            scratch_shapes=[pltpu.VMEM((2, LOCAL_ROWS, D_MODEL), _F32), pltpu.SemaphoreType.DMA((2,))],
        ),
        out_shape=[jax.ShapeDtypeStruct(xs.shape, xs.dtype), jax.ShapeDtypeStruct((rows, ROUTE_LANES), _F32)],
        input_output_aliases={5: 0},
        compiler_params=pltpu.CompilerParams(
            dimension_semantics=("arbitrary",), vmem_limit_bytes=VMEM_LIMIT, has_side_effects=True),
        name="moe_dispatch",
    )(plan["seg_off"], plan["c8"], plan["lo"], hn, route, xs)


def _moe_kernel(te_ref, nv_ref, nu_ref, x_ref, wg_ref, wu_ref, wd_ref, o_ref, xb_ref, wgb_ref, wub_ref, wdb_ref):
    del te_ref, nu_ref
    i = pl.program_id(0)
    j = pl.program_id(1)
    nv = nv_ref[i]

    @pl.when(nv > 0)
    def _():
        @pl.when(j == 0)
        def _():
            xb_ref[...] = x_ref[...].astype(_BF16)

        wgb_ref[...] = wg_ref[0].astype(_BF16)
        wub_ref[...] = wu_ref[0].astype(_BF16)
        wdb_ref[...] = wd_ref[0].astype(_BF16)

    for s in range(MOE_ROW_TILE // MOE_SUB):
        rows = slice(s * MOE_SUB, (s + 1) * MOE_SUB)

        @pl.when(nv > s * MOE_SUB)
        def _():
            xs = xb_ref[rows, :]
            g = _dot(xs, wgb_ref[...])
            a = (g * _sigmoid(g) * _dot(xs, wub_ref[...])).astype(_BF16)
            d = _dot(a, wdb_ref[...])

            @pl.when(j == 0)
            def _():
                o_ref[rows, :] = d

            @pl.when(j > 0)
            def _():
                o_ref[rows, :] += d

        @pl.when(jnp.logical_and(nv <= s * MOE_SUB, j == 0))
        def _():
            o_ref[rows, :] = jnp.zeros((MOE_SUB, D_MODEL), _F32)


def _moe_call(tile_expert, tile_valid, n_used, xs, wg, wu, wd):
    n_tiles = xs.shape[0] // MOE_ROW_TILE
    eff = wg.shape[2]
    n_j = eff // MOE_FF_TILE

    def row_map(i, j, te, nv, nu):
        return (jnp.maximum(jnp.minimum(i, nu[0] - 1), 0), 0)

    def col_of(i, j, nu):
        return jnp.where(i < nu[0], j, n_j - 1)

    return pl.pallas_call(
        _moe_kernel,
        grid_spec=pltpu.PrefetchScalarGridSpec(
            num_scalar_prefetch=3,
            grid=(n_tiles, n_j),
            in_specs=[
                pl.BlockSpec((MOE_ROW_TILE, D_MODEL), row_map),
                pl.BlockSpec((1, D_MODEL, MOE_FF_TILE), lambda i, j, te, nv, nu: (te[i], 0, col_of(i, j, nu))),
                pl.BlockSpec((1, D_MODEL, MOE_FF_TILE), lambda i, j, te, nv, nu: (te[i], 0, col_of(i, j, nu))),
                pl.BlockSpec((1, MOE_FF_TILE, D_MODEL), lambda i, j, te, nv, nu: (te[i], col_of(i, j, nu), 0)),
            ],
            out_specs=pl.BlockSpec((MOE_ROW_TILE, D_MODEL), lambda i, j, te, nv, nu: (i, 0)),
            scratch_shapes=[
                pltpu.VMEM((MOE_ROW_TILE, D_MODEL), _BF16),
                pltpu.VMEM((D_MODEL, MOE_FF_TILE), _BF16),
                pltpu.VMEM((D_MODEL, MOE_FF_TILE), _BF16),
                pltpu.VMEM((MOE_FF_TILE, D_MODEL), _BF16),
            ],
        ),
        out_shape=jax.ShapeDtypeStruct(xs.shape, _F32),
        compiler_params=pltpu.CompilerParams(
            dimension_semantics=("arbitrary", "arbitrary"), vmem_limit_bytes=VMEM_LIMIT),
        name="moe_grouped",
    )(tile_expert, tile_valid, n_used, xs, wg, wu, wd)


def _combine_kernel(segoff_ref, c8_ref, lo_ref, h_ref, route_ref, lpos_ref, fg_ref, ys_hbm, o_ref, yl_ref, sem,
                    *, tile, tile0):
    t = pl.program_id(0)
    n = pl.num_programs(0)
    slot = t % 2
    tt = tile0 + t

    def copies(tidx, s):
        def make_copy(lo, so):
            return pltpu.make_async_copy(ys_hbm.at[pl.ds(so, SEG_ALIGN)], yl_ref.at[s, pl.ds(lo, SEG_ALIGN)],
                                         sem.at[s])
        return functools.partial(_segment_copies, c8_ref, lo_ref, segoff_ref, tidx, make_copy)

    @pl.when(t == 0)
    def _():
        yl_ref[...] = jnp.zeros(yl_ref.shape, _F32)
        copies(tt, slot)(lambda c: c.start())

    @pl.when(t + 1 < n)
    def _():
        copies(tt + 1, 1 - slot)(lambda c: c.start())

    copies(tt, slot)(lambda c: c.wait())
    last = tt * N_EXPERTS + N_EXPERTS - 1
    used = lo_ref[last] + c8_ref[last]
    rowi = lax.broadcasted_iota(jnp.int32, (LOCAL_ROWS, 1), 0)
    ylb = jnp.where(rowi < used, yl_ref[slot], 0.0).astype(_BF16)
    lanel = lax.broadcasted_iota(jnp.int32, (tile, LOCAL_ROWS), 1).astype(_F32)
    take = (jnp.where(lanel == lpos_ref[:, 0:1], route_ref[:, 2:3], 0.0)
            + jnp.where(lanel == lpos_ref[:, 1:2], route_ref[:, 3:4], 0.0)).astype(_BF16)
    o_ref[...] = _rmsnorm(h_ref[...] + _dot(take, ylb), fg_ref[...])


def _combine_call(plan, h, route, lpos, fg, ys, *, tile, tile0):
    rows = h.shape[0]
    smap = lambda i, *_: (i, 0)
    return pl.pallas_call(
        functools.partial(_combine_kernel, tile=tile, tile0=tile0),
        grid_spec=pltpu.PrefetchScalarGridSpec(
            num_scalar_prefetch=3,
            grid=(rows // tile,),
            in_specs=[
                pl.BlockSpec((tile, D_MODEL), smap),
                pl.BlockSpec((tile, ROUTE_LANES), smap),
                pl.BlockSpec((tile, ROUTE_LANES), smap),
                pl.BlockSpec((1, D_MODEL), lambda i, *_: (0, 0)),
                pl.BlockSpec(memory_space=pl.ANY),
            ],
            out_specs=pl.BlockSpec((tile, D_MODEL), smap),
            scratch_shapes=[pltpu.VMEM((2, LOCAL_ROWS, D_MODEL), _F32), pltpu.SemaphoreType.DMA((2,))],
        ),
        out_shape=jax.ShapeDtypeStruct((rows, D_MODEL), _F32),
        compiler_params=pltpu.CompilerParams(
            dimension_semantics=("arbitrary",), vmem_limit_bytes=VMEM_LIMIT),
        name="moe_combine",
    )(plan["seg_off"], plan["c8"], plan["lo"], h, route, lpos, fg, ys)


def _layer_weights(l, norm1_g, w_in, pool_w, pool_scale, conv_dw_w, conv_dw_b, conv_ln_g, conv_ln_b,
                   conv_pw_w, w_out, norm2_g):
    zero = jnp.zeros((POOL_GROUP, POOL_GROUP), _F32)
    pwl = pool_w[l]
    wpa = jnp.block([[pwl[0], zero], [zero, pwl[1]]])
    wpb = jnp.block([[pwl[2], zero], [zero, pwl[3]]])
    return {
        "g1": norm1_g[l][None, :], "w_in": w_in[l].astype(_BF16),
        "wpa": wpa.astype(_BF16), "wpb": wpb.astype(_BF16), "pscale": pool_scale[l][None, :],
        "dw": conv_dw_w[l], "dwb": conv_dw_b[l][None, :], "lng": conv_ln_g[l][None, :],
        "lnb": conv_ln_b[l][None, :], "pw": conv_pw_w[l].astype(_BF16), "w_out": w_out[l].astype(_BF16),
        "g2": norm2_g[l][None, :],
    }


def _routing_plan(experts, tok_tile):
    n = experts.shape[0]
    n_tok_tiles = n // tok_tile
    ids = jnp.arange(N_EXPERTS, dtype=jnp.int32)
    chosen = jnp.sum((experts[:, :, None] == ids[None, None, :]).astype(jnp.int32), axis=1)
    cnt = jnp.sum(chosen.reshape(n_tok_tiles, tok_tile, N_EXPERTS), axis=1)
    c8 = ((cnt + SEG_ALIGN - 1) // SEG_ALIGN) * SEG_ALIGN
    lo = jnp.cumsum(c8, axis=1) - c8
    within = jnp.cumsum(c8, axis=0) - c8
    glen = jnp.sum(c8, axis=0)
    gpad = ((glen + MOE_ROW_TILE - 1) // MOE_ROW_TILE) * MOE_ROW_TILE
    ends = jnp.cumsum(gpad)
    goff = ends - gpad
    seg_off = goff[None, :] + within
    max_rows = 2 * n + (SEG_ALIGN - 1) * N_EXPERTS * n_tok_tiles
    n_tiles = -(-max_rows // MOE_ROW_TILE) + N_EXPERTS
    starts = jnp.arange(n_tiles, dtype=jnp.int32) * MOE_ROW_TILE
    te = jnp.sum((starts[:, None] >= ends[None, :]).astype(jnp.int32), axis=1)
    n_used = (ends[-1] // MOE_ROW_TILE).astype(jnp.int32)
    te_c = jnp.minimum(te, N_EXPERTS - 1)
    valid = jnp.clip(goff[te_c] + glen[te_c] - starts, 0, MOE_ROW_TILE)
    valid = jnp.where(te < N_EXPERTS, valid, 0).astype(jnp.int32)
    last_e = te_c[jnp.maximum(n_used - 1, 0)]
    te_c = jnp.where(te < N_EXPERTS, te_c, last_e).astype(jnp.int32)
    flat = lambda a: a.reshape(-1).astype(jnp.int32)
    return {"seg_off": flat(seg_off), "c8": flat(c8), "lo": flat(lo), "tile_expert": te_c,
            "tile_valid": valid, "n_used": n_used.reshape(1), "n_tiles": n_tiles}


def kernel(x_prompt, x_sample, state_pool, state_conv, meta_tokens, norm1_g, w_in, pool_w, pool_scale,
           conv_dw_w, conv_dw_b, conv_ln_g, conv_ln_b, conv_pw_w, w_out, norm2_g, ffn_w_gate, ffn_w_up,
           ffn_w_down, router_w, moe_w_gate, moe_w_up, moe_w_down, final_norm_g):
    batch, seq, _ = x_prompt.shape
    dec_batch, dec_seq, _ = x_sample.shape
    assert dec_seq == 8 and seq % ROW_TILE == 0 and (dec_batch * dec_seq) % ROW_TILE == 0
    assert (dec_batch * dec_seq) % DEC_ROW_TILE == 0
    lws = [_layer_weights(l, norm1_g, w_in, pool_w, pool_scale, conv_dw_w, conv_dw_b, conv_ln_g, conv_ln_b,
                          conv_pw_w, w_out, norm2_g) for l in range(2)]
    lws[0].update(wg=ffn_w_gate[0].astype(_BF16), wu=ffn_w_up[0].astype(_BF16), wd=ffn_w_down[0].astype(_BF16))
    rw = jnp.zeros((D_MODEL, ROUTE_LANES), _F32).at[:, :N_EXPERTS].set(router_w[0])
    lws[1].update(rw=rw.astype(_BF16))

    n_p = batch * seq
    n_s = dec_batch * dec_seq
    n_t = seq // ROW_TILE
    hp = x_prompt.reshape(n_p, D_MODEL)
    hs = x_sample.reshape(n_s, D_MODEL)
    hm = meta_tokens.astype(_F32)
    zero_pp = jnp.zeros((1, POOL_HALO, POOL_WIDTH), _F32)
    zero_cp = jnp.zeros((1, CONV_HALO, CONV_WIDTH), _F32)
    sp_pad = jnp.pad(state_pool, ((0, 0), (0, 0), (POOL_HALO - POOL_BUF, 0), (0, 0)))
    sc_pad = jnp.pad(state_conv, ((0, 0), (0, 0), (CONV_HALO - CONV_BUF, 0), (0, 0)))

    seq_kw = dict(mode="seq", tile=ROW_TILE, n_seq=batch, n_t=n_t)
    meta_kw = dict(mode="seq", ramp=True, tile=N_META, n_seq=1, n_t=1)
    dec_kw = dict(mode="dec", ramp=False, tile=DEC_ROW_TILE, n_seq=0, n_t=0)

    hm, pst_m0, cst_m0 = _front_call(hm, zero_pp, zero_cp, lws[0], tail="ffn", **meta_kw)
    hp, pst_p0, cst_p0 = _front_call(hp, pst_m0, cst_m0, lws[0], ramp=False, tail="ffn", **seq_kw)
    hs, up_s0, v_s0 = _front_call(hs, sp_pad[0], sc_pad[0], lws[0], tail="ffn", **dec_kw)

    _, pst_m1, cst_m1 = _front_call(hm, zero_pp, zero_cp, lws[1], tail="none", **meta_kw)
    hp, pst_p1, cst_p1, hn_p, route_p = _front_call(hp, pst_m1, cst_m1, lws[1], ramp=False, tail="route", **seq_kw)
    hs, up_s1, v_s1, hn_s, route_s = _front_call(hs, sp_pad[1], sc_pad[1], lws[1], tail="route", **dec_kw)

    experts = jnp.concatenate([route_p[:, :2], route_s[:, :2]], axis=0).astype(jnp.int32)
    experts = jnp.clip(experts, 0, N_EXPERTS - 1)
    plan = _routing_plan(experts, ROW_TILE)
    s_tile0 = n_p // ROW_TILE

    xs = jnp.zeros((plan["n_tiles"] * MOE_ROW_TILE, D_MODEL), _F32)
    xs, lpos_p = _dispatch_call(plan, hn_p, route_p, xs, tile=ROW_TILE, tile0=0)
    xs, lpos_s = _dispatch_call(plan, hn_s, route_s, xs, tile=ROW_TILE, tile0=s_tile0)
    ys = _moe_call(plan["tile_expert"], plan["tile_valid"], plan["n_used"], xs,
                   moe_w_gate[0], moe_w_up[0], moe_w_down[0])

    fg = final_norm_g[None, :]
    y_p = _combine_call(plan, hp, route_p, lpos_p, fg, ys, tile=ROW_TILE, tile0=0)
    y_s = _combine_call(plan, hs, route_s, lpos_s, fg, ys, tile=ROW_TILE, tile0=s_tile0)

    new_pool_p = jnp.stack([pst_p0, pst_p1])[:, :, POOL_HALO - POOL_BUF:, :]
    new_conv_p = jnp.stack([cst_p0, cst_p1])[:, :, CONV_HALO - CONV_BUF:, :]
    up_s = jnp.stack([up_s0, up_s1]).reshape(2, dec_batch, dec_seq, POOL_WIDTH)
    v_s = jnp.stack([v_s0, v_s1]).reshape(2, dec_batch, dec_seq, CONV_WIDTH)
    new_pool_s = jnp.concatenate([state_pool, up_s], axis=2)[:, :, -POOL_BUF:, :]
    new_conv_s = jnp.concatenate([state_conv, v_s], axis=2)[:, :, -CONV_BUF:, :]
    return (y_p.reshape(batch, seq, D_MODEL), y_s.reshape(dec_batch, dec_seq, D_MODEL),
            new_pool_p, new_conv_p, new_pool_s, new_conv_s)
```

```python
import functools

import jax
import jax.numpy as jnp
from jax import lax
from jax.experimental import pallas as pl
from jax.experimental.pallas import tpu as pltpu

D_MODEL = 1024
N_META = 16
POOL_WIDTH = 512
CONV_WIDTH = 512
POOL_WINDOWS = (2, 4, 8, 16)
POOL_GROUP = 128
POOL_BUF = 15
CONV_KERNEL = 31
CONV_BUF = 30
IN_COLS = POOL_WIDTH + 2 * CONV_WIDTH
N_EXPERTS = 8
RMS_EPS = 1e-6
LN_EPS = 1e-5

POOL_HALO = 16
CONV_HALO = 32
CONV_SHIFT_PAD = CONV_HALO - 8
ROUTE_LANES = 128

ROW_TILE = 512
DEC_ROW_TILE = 256
MIX_CHUNK = 64
FFN_CHUNK = 512

MOE_ROW_TILE = 1024
MOE_SUB = 256
MOE_FF_TILE = 512
SEG_ALIGN = 8
LOCAL_ROWS = -(-(2 * ROW_TILE + (SEG_ALIGN - 1) * N_EXPERTS) // 128) * 128

VMEM_LIMIT = 56 * 1024 * 1024

_BF16 = jnp.bfloat16
_F32 = jnp.float32


def _dot(a, b):
    return jnp.dot(a, b, preferred_element_type=_F32)


def _sigmoid(x):
    return 1.0 / (1.0 + jnp.exp(-x))


def _rmsnorm(x, g):
    return x * lax.rsqrt(jnp.mean(x * x, axis=-1, keepdims=True) + RMS_EPS) * g


def _mix_chunk(ep_ref, ec_ref, r0, rows, dw_ref, dwb_ref, lng_ref, lnb_ref, pos0, sh_ref=None):
    def conv_rows(start, c0):
        if sh_ref is None or start % 8 == 0:
            return ec_ref[start:start + rows, c0:c0 + 128]
        q, r = divmod(start, 8)
        return sh_ref[r - 1, 8 * q:8 * q + rows, c0:c0 + 128]

    pooled_cols = []
    for gi, win in enumerate(POOL_WINDOWS):
        c0 = gi * POOL_GROUP
        base = POOL_HALO + r0
        x0 = ep_ref[base:base + rows, c0:c0 + POOL_GROUP]
        s = x0
        for k in range(1, win):
            s = s + ep_ref[base - k:base - k + rows, c0:c0 + POOL_GROUP]
        if pos0 is None:
            pooled = s * (1.0 / win) - x0
        else:
            pos = pos0 + r0 + lax.broadcasted_iota(jnp.int32, (rows, POOL_GROUP), 0)
            cnt = jnp.minimum(pos + 1, win).astype(_F32)
            pooled = s / cnt - x0
        pooled_cols.append(pooled)
    pooled = jnp.concatenate(pooled_cols, axis=-1)

    conv_cols = []
    for cg in range(CONV_WIDTH // 128):
        c0 = cg * 128
        base = CONV_HALO - CONV_BUF + r0
        acc = conv_rows(base, c0) * dw_ref[0:1, c0:c0 + 128]
        for k in range(1, CONV_KERNEL):
            acc = acc + conv_rows(base + k, c0) * dw_ref[k:k + 1, c0:c0 + 128]
        conv_cols.append(acc + dwb_ref[:, c0:c0 + 128])
    y = jnp.concatenate(conv_cols, axis=-1)
    mu = jnp.mean(y, axis=-1, keepdims=True)
    d = y - mu
    var = jnp.mean(d * d, axis=-1, keepdims=True)
    yn = d * lax.rsqrt(var + LN_EPS) * lng_ref[...] + lnb_ref[...]
    return pooled, yn * _sigmoid(yn)


def _front_kernel(*refs, mode, ramp, tail, tile, n_t):
    it = iter(refs)
    h_ref, ppast_ref, cpast_ref = next(it), next(it), next(it)
    g1_ref, win_ref, wpa_ref, wpb_ref, psc_ref = next(it), next(it), next(it), next(it), next(it)
    dw_ref, dwb_ref, lng_ref, lnb_ref, pw_ref, wout_ref, g2_ref = (next(it) for _ in range(7))
    if tail == "ffn":
        wg_ref, wu_ref, wd_ref = next(it), next(it), next(it)
    elif tail == "route":
        rw_ref = next(it)
    hout_ref = next(it)
    if mode == "seq":
        pst_ref, cst_ref = next(it), next(it)
    else:
        up_ref, v_ref = next(it), next(it)
    if tail == "route":
        hn2_ref, route_ref = next(it), next(it)
    ep_ref, ec_ref, mixed_ref = next(it), next(it), next(it)
    if mode == "dec":
        sp_ref, sc_ref = next(it), next(it)
    else:
        sh_ref = next(it)

    x = h_ref[...]
    hn = _rmsnorm(x, g1_ref[...]).astype(_BF16)
    u = _dot(hn, win_ref[...])
    up = u[:, :POOL_WIDTH]
    v = u[:, POOL_WIDTH:POOL_WIDTH + CONV_WIDTH] * _sigmoid(u[:, POOL_WIDTH + CONV_WIDTH:])
    ep_ref[POOL_HALO:, :] = up
    ec_ref[CONV_HALO:, :] = v

    if mode == "seq":
        t = pl.program_id(1)

        @pl.when(t == 0)
        def _():
            ep_ref[0:POOL_HALO, :] = ppast_ref[0]
            ec_ref[0:CONV_HALO, :] = cpast_ref[0]

        for r in range(1, 8):
            sh_ref[r - 1] = ec_ref[r:r + tile + CONV_SHIFT_PAD, :]
        chunk = min(MIX_CHUNK, tile)
        for r0 in range(0, tile, chunk):
            pos0 = t * tile if ramp else None
            pooled, act = _mix_chunk(ep_ref, ec_ref, r0, chunk, dw_ref, dwb_ref, lng_ref, lnb_ref, pos0, sh_ref)
            mixed_ref[r0:r0 + chunk, 0:POOL_WIDTH] = pooled
            mixed_ref[r0:r0 + chunk, POOL_WIDTH:] = act
        ptail = ep_ref[tile:tile + POOL_HALO, :]
        ctail = ec_ref[tile:tile + CONV_HALO, :]
        ep_ref[0:POOL_HALO, :] = ptail
        ec_ref[0:CONV_HALO, :] = ctail
        pst_ref[0] = ptail
        cst_ref[0] = ctail
    else:
        up_ref[...] = up
        v_ref[...] = v

        def seq_body(s, carry):
            r = pl.multiple_of(s * 8, 8)
            sp_ref[0:POOL_HALO, :] = ppast_ref[s]
            sp_ref[POOL_HALO:, :] = ep_ref[pl.ds(POOL_HALO + r, 8), :]
            sc_ref[0:CONV_HALO, :] = cpast_ref[s]
            sc_ref[CONV_HALO:, :] = ec_ref[pl.ds(CONV_HALO + r, 8), :]
            pooled, act = _mix_chunk(sp_ref, sc_ref, 0, 8, dw_ref, dwb_ref, lng_ref, lnb_ref, None)
            mixed_ref[pl.ds(r, 8), 0:POOL_WIDTH] = pooled
            mixed_ref[pl.ds(r, 8), POOL_WIDTH:] = act
            return carry

        lax.fori_loop(0, tile // 8, seq_body, 0)

    pa = _dot(mixed_ref[:, 0:256].astype(_BF16), wpa_ref[...])
    pb = _dot(mixed_ref[:, 256:512].astype(_BF16), wpb_ref[...])
    pool_out = jnp.concatenate([pa, pb], axis=-1) * psc_ref[...]
    conv_out = _dot(mixed_ref[:, POOL_WIDTH:].astype(_BF16), pw_ref[...])
    h1 = (x + _dot(pool_out.astype(_BF16), wout_ref[0:POOL_WIDTH, :])
          + _dot(conv_out.astype(_BF16), wout_ref[POOL_WIDTH:, :]))

    if tail == "none":
        hout_ref[...] = h1
        return
    hn2 = _rmsnorm(h1, g2_ref[...])
    if tail == "ffn":
        hb = hn2.astype(_BF16)
        d_ff = wg_ref.shape[1]
        hout_ref[...] = h1
        for c0 in range(0, d_ff, FFN_CHUNK):
            cs = min(FFN_CHUNK, d_ff - c0)
            g = _dot(hb, wg_ref[:, c0:c0 + cs])
            a = (g * _sigmoid(g) * _dot(hb, wu_ref[:, c0:c0 + cs])).astype(_BF16)
            hout_ref[...] += _dot(a, wd_ref[c0:c0 + cs, :])
    else:
        hout_ref[...] = h1
        hn2_ref[...] = hn2
        logits = _dot(hn2.astype(_BF16), rw_ref[...])
        lane = lax.broadcasted_iota(jnp.int32, logits.shape, 1).astype(_F32)
        neg = jnp.float32(-jnp.inf)
        lg = jnp.where(lane < N_EXPERTS, logits, neg)
        m1 = jnp.max(lg, axis=-1, keepdims=True)
        i1 = jnp.min(jnp.where(lg == m1, lane, float(ROUTE_LANES)), axis=-1, keepdims=True)
        lg2 = jnp.where(lane == i1, neg, lg)
        m2 = jnp.max(lg2, axis=-1, keepdims=True)
        i2 = jnp.min(jnp.where(lg2 == m2, lane, float(ROUTE_LANES)), axis=-1, keepdims=True)
        e21 = jnp.exp(m2 - m1)
        gate1 = 1.0 / (1.0 + e21)
        gate2 = e21 * gate1
        route_ref[...] = jnp.where(lane == 0, i1, jnp.where(lane == 1, i2, jnp.where(
            lane == 2, gate1, jnp.where(lane == 3, gate2, 0.0))))


def _const_spec(shape):
    nd = len(shape)
    return pl.BlockSpec(shape, lambda *_: (0,) * nd, pipeline_mode=pl.Buffered(1))


def _front_call(h, ppast, cpast, lw, *, mode, ramp, tail, tile, n_seq, n_t):
    rows = h.shape[0]
    weights = [lw["g1"], lw["w_in"], lw["wpa"], lw["wpb"], lw["pscale"], lw["dw"], lw["dwb"], lw["lng"],
               lw["lnb"], lw["pw"], lw["w_out"], lw["g2"]]
    if tail == "ffn":
        weights += [lw["wg"], lw["wu"], lw["wd"]]
    elif tail == "route":
        weights += [lw["rw"]]
    w_specs = [_const_spec(w.shape) for w in weights]

    if mode == "seq":
        grid = (n_seq, n_t)
        row_map = lambda b, t: (b * n_t + t, 0)
        shared = ppast.shape[0] == 1
        past_map = (lambda b, t: (0, 0, 0)) if shared else (lambda b, t: (b, 0, 0))
        past_specs = [pl.BlockSpec((1, POOL_HALO, POOL_WIDTH), past_map),
                      pl.BlockSpec((1, CONV_HALO, CONV_WIDTH), past_map)]
        out_shape = [jax.ShapeDtypeStruct((rows, D_MODEL), _F32),
                     jax.ShapeDtypeStruct((n_seq, POOL_HALO, POOL_WIDTH), _F32),
                     jax.ShapeDtypeStruct((n_seq, CONV_HALO, CONV_WIDTH), _F32)]
        out_specs = [pl.BlockSpec((tile, D_MODEL), row_map),
                     pl.BlockSpec((1, POOL_HALO, POOL_WIDTH), lambda b, t: (b, 0, 0)),
                     pl.BlockSpec((1, CONV_HALO, CONV_WIDTH), lambda b, t: (b, 0, 0))]
        semantics = ("arbitrary", "arbitrary")
    else:
        grid = (rows // tile,)
        row_map = lambda i: (i, 0)
        spt = tile // 8
        past_specs = [pl.BlockSpec((spt, POOL_HALO, POOL_WIDTH), lambda i: (i, 0, 0)),
                      pl.BlockSpec((spt, CONV_HALO, CONV_WIDTH), lambda i: (i, 0, 0))]
        out_shape = [jax.ShapeDtypeStruct((rows, D_MODEL), _F32),
                     jax.ShapeDtypeStruct((rows, POOL_WIDTH), _F32),
                     jax.ShapeDtypeStruct((rows, CONV_WIDTH), _F32)]
        out_specs = [pl.BlockSpec((tile, D_MODEL), row_map),
                     pl.BlockSpec((tile, POOL_WIDTH), row_map),
                     pl.BlockSpec((tile, CONV_WIDTH), row_map)]
        semantics = ("arbitrary",)
    if tail == "route":
        out_shape += [jax.ShapeDtypeStruct((rows, D_MODEL), _F32),
                      jax.ShapeDtypeStruct((rows, ROUTE_LANES), _F32)]
        out_specs += [pl.BlockSpec((tile, D_MODEL), row_map), pl.BlockSpec((tile, ROUTE_LANES), row_map)]

    scratch = [pltpu.VMEM((POOL_HALO + tile, POOL_WIDTH), _F32),
               pltpu.VMEM((CONV_HALO + tile, CONV_WIDTH), _F32),
               pltpu.VMEM((tile, POOL_WIDTH + CONV_WIDTH), _F32)]
    if mode == "dec":
        scratch += [pltpu.VMEM((POOL_HALO + 8, POOL_WIDTH), _F32), pltpu.VMEM((CONV_HALO + 8, CONV_WIDTH), _F32)]
    else:
        scratch += [pltpu.VMEM((7, tile + CONV_SHIFT_PAD, CONV_WIDTH), _F32)]

    body = functools.partial(_front_kernel, mode=mode, ramp=ramp, tail=tail, tile=tile, n_t=n_t)
    return pl.pallas_call(
        body,
        grid=grid,
        in_specs=[pl.BlockSpec((tile, D_MODEL), row_map)] + past_specs + w_specs,
        out_specs=out_specs,
        out_shape=out_shape,
        scratch_shapes=scratch,
        compiler_params=pltpu.CompilerParams(dimension_semantics=semantics, vmem_limit_bytes=VMEM_LIMIT),
        name=f"front_{mode}_{tail}",
    )(h, ppast, cpast, *weights)


def _segment_copies(c8_ref, lo_ref, segoff_ref, tidx, make_copy, act):
    for e in range(N_EXPERTS):
        k = tidx * N_EXPERTS + e
        lo_e = lo_ref[k]
        so_e = segoff_ref[k]

        def body(c, carry):
            act(make_copy(pl.multiple_of(lo_e + c * SEG_ALIGN, SEG_ALIGN),
                          pl.multiple_of(so_e + c * SEG_ALIGN, SEG_ALIGN)))
            return carry

        lax.fori_loop(0, c8_ref[k] // SEG_ALIGN, body, 0)


def _dispatch_kernel(segoff_ref, c8_ref, lo_ref, hn_ref, route_ref, xs_in_hbm, xs_hbm, lpos_ref, xl_ref, sem,
                     *, tile, tile0):
    del xs_in_hbm
    t = pl.program_id(0)
    n = pl.num_programs(0)
    slot = t % 2
    tt = tile0 + t
    lane = lax.broadcasted_iota(jnp.int32, (tile, ROUTE_LANES), 1).astype(_F32)
    oh0 = (lane == route_ref[:, 0:1]).astype(_F32)
    oh1 = (lane == route_ref[:, 1:2]).astype(_F32)
    earlier = (lax.broadcasted_iota(jnp.int32, (tile, tile), 1)
               < lax.broadcasted_iota(jnp.int32, (tile, tile), 0)).astype(_BF16)
    rank = _dot(earlier, (oh0 + oh1).astype(_BF16))
    lane1 = lax.broadcasted_iota(jnp.int32, (1, ROUTE_LANES), 1)
    lo_vec = jnp.zeros((1, ROUTE_LANES), _F32)
    for e in range(N_EXPERTS):
        lo_vec = jnp.where(lane1 == e, lo_ref[tt * N_EXPERTS + e].astype(_F32), lo_vec)
    posv = rank + lo_vec
    p0 = jnp.sum(oh0 * posv, axis=-1, keepdims=True)
    p1 = jnp.sum(oh1 * posv, axis=-1, keepdims=True)
    slab = jnp.where(lane == 0, p0, jnp.where(lane == 1, p1, 0.0))
    lpos_ref[...] = slab
    slab_t = slab.T
    rowi = lax.broadcasted_iota(jnp.int32, (LOCAL_ROWS, tile), 0).astype(_F32)
    place = jnp.where(jnp.logical_or(rowi == slab_t[0:1, :], rowi == slab_t[1:2, :]), 1.0, 0.0).astype(_BF16)
    xl_ref[slot] = _dot(place, hn_ref[...].astype(_BF16))

    def copies(tidx, s):
        def make_copy(lo, so):
            return pltpu.make_async_copy(xl_ref.at[s, pl.ds(lo, SEG_ALIGN)], xs_hbm.at[pl.ds(so, SEG_ALIGN)],
                                         sem.at[s])
        return functools.partial(_segment_copies, c8_ref, lo_ref, segoff_ref, tidx, make_copy)

    copies(tt, slot)(lambda c: c.start())

    @pl.when(t > 0)
    def _():
        copies(tt - 1, 1 - slot)(lambda c: c.wait())

    @pl.when(t == n - 1)
    def _():
        copies(tt, slot)(lambda c: c.wait())


def _dispatch_call(plan, hn, route, xs, *, tile, tile0):
    rows = hn.shape[0]
    smap = lambda i, *_: (i, 0)
    return pl.pallas_call(
        functools.partial(_dispatch_kernel, tile=tile, tile0=tile0),
        grid_spec=pltpu.PrefetchScalarGridSpec(
            num_scalar_prefetch=3,
            grid=(rows // tile,),
            in_specs=[pl.BlockSpec((tile, D_MODEL), smap), pl.BlockSpec((tile, ROUTE_LANES), smap),
                      pl.BlockSpec(memory_space=pl.ANY)],
            out_specs=[pl.BlockSpec(memory_space=pl.ANY), pl.BlockSpec((tile, ROUTE_LANES), smap)],
            scratch_shapes=[pltpu.VMEM((2, LOCAL_ROWS, D_MODEL), _F32), pltpu.SemaphoreType.DMA((2,))],
        ),
        out_shape=[jax.ShapeDtypeStruct(xs.shape, xs.dtype), jax.ShapeDtypeStruct((rows, ROUTE_LANES), _F32)],
        input_output_aliases={5: 0},
        compiler_params=pltpu.CompilerParams(
            dimension_semantics=("arbitrary",), vmem_limit_bytes=VMEM_LIMIT, has_side_effects=True),
        name="moe_dispatch",
    )(plan["seg_off"], plan["c8"], plan["lo"], hn, route, xs)


def _moe_kernel(te_ref, nv_ref, nu_ref, x_ref, wg_ref, wu_ref, wd_ref, o_ref, xb_ref, wgb_ref, wub_ref, wdb_ref):
    del te_ref, nu_ref
    i = pl.program_id(0)
    j = pl.program_id(1)
    nv = nv_ref[i]

    @pl.when(j == 0)
    def _():
        o_ref[...] = jnp.zeros(o_ref.shape, _F32)
        xb_ref[...] = x_ref[...].astype(_BF16)

    def expert_ffn(xs, wg, wu, wd):
        g = _dot(xs, wg)
        a = (g * _sigmoid(g) * _dot(xs, wu)).astype(_BF16)
        return _dot(a, wd)

    @pl.when(nv == MOE_ROW_TILE)
    def _():
        o_ref[...] += expert_ffn(xb_ref[...], wg_ref[0].astype(_BF16), wu_ref[0].astype(_BF16),
                                 wd_ref[0].astype(_BF16))

    @pl.when(jnp.logical_and(nv > 0, nv < MOE_ROW_TILE))
    def _():
        wgb_ref[...] = wg_ref[0].astype(_BF16)
        wub_ref[...] = wu_ref[0].astype(_BF16)
        wdb_ref[...] = wd_ref[0].astype(_BF16)
        for s in range(MOE_ROW_TILE // MOE_SUB):
            rows = slice(s * MOE_SUB, (s + 1) * MOE_SUB)

            @pl.when(nv > s * MOE_SUB)
            def _():
                o_ref[rows, :] += expert_ffn(xb_ref[rows, :], wgb_ref[...], wub_ref[...], wdb_ref[...])


def _moe_call(tile_expert, tile_valid, n_used, xs, wg, wu, wd):
    n_tiles = xs.shape[0] // MOE_ROW_TILE
    eff = wg.shape[2]
    n_j = eff // MOE_FF_TILE

    def row_map(i, j, te, nv, nu):
        return (jnp.maximum(jnp.minimum(i, nu[0] - 1), 0), 0)

    def col_of(i, j, nu):
        return jnp.where(i < nu[0], j, n_j - 1)

    return pl.pallas_call(
        _moe_kernel,
        grid_spec=pltpu.PrefetchScalarGridSpec(
            num_scalar_prefetch=3,
            grid=(n_tiles, n_j),
            in_specs=[
                pl.BlockSpec((MOE_ROW_TILE, D_MODEL), row_map),
                pl.BlockSpec((1, D_MODEL, MOE_FF_TILE), lambda i, j, te, nv, nu: (te[i], 0, col_of(i, j, nu))),
                pl.BlockSpec((1, D_MODEL, MOE_FF_TILE), lambda i, j, te, nv, nu: (te[i], 0, col_of(i, j, nu))),
                pl.BlockSpec((1, MOE_FF_TILE, D_MODEL), lambda i, j, te, nv, nu: (te[i], col_of(i, j, nu), 0)),
            ],
            out_specs=pl.BlockSpec((MOE_ROW_TILE, D_MODEL), lambda i, j, te, nv, nu: (i, 0)),
            scratch_shapes=[
                pltpu.VMEM((MOE_ROW_TILE, D_MODEL), _BF16),
                pltpu.VMEM((D_MODEL, MOE_FF_TILE), _BF16),
                pltpu.VMEM((D_MODEL, MOE_FF_TILE), _BF16),
                pltpu.VMEM((MOE_FF_TILE, D_MODEL), _BF16),
            ],
        ),
        out_shape=jax.ShapeDtypeStruct(xs.shape, _F32),
        compiler_params=pltpu.CompilerParams(
            dimension_semantics=("arbitrary", "arbitrary"), vmem_limit_bytes=VMEM_LIMIT),
        name="moe_grouped",
    )(tile_expert, tile_valid, n_used, xs, wg, wu, wd)


def _combine_kernel(segoff_ref, c8_ref, lo_ref, h_ref, route_ref, lpos_ref, fg_ref, ys_hbm, o_ref, yl_ref, sem,
                    *, tile, tile0):
    t = pl.program_id(0)
    n = pl.num_programs(0)
    slot = t % 2
    tt = tile0 + t

    def copies(tidx, s):
        def make_copy(lo, so):
            return pltpu.make_async_copy(ys_hbm.at[pl.ds(so, SEG_ALIGN)], yl_ref.at[s, pl.ds(lo, SEG_ALIGN)],
                                         sem.at[s])
        return functools.partial(_segment_copies, c8_ref, lo_ref, segoff_ref, tidx, make_copy)

    @pl.when(t == 0)
    def _():
        yl_ref[...] = jnp.zeros(yl_ref.shape, _F32)
        copies(tt, slot)(lambda c: c.start())

    @pl.when(t + 1 < n)
    def _():
        copies(tt + 1, 1 - slot)(lambda c: c.start())

    copies(tt, slot)(lambda c: c.wait())
    last = tt * N_EXPERTS + N_EXPERTS - 1
    used = lo_ref[last] + c8_ref[last]
    rowi = lax.broadcasted_iota(jnp.int32, (LOCAL_ROWS, 1), 0)
    ylb = jnp.where(rowi < used, yl_ref[slot], 0.0).astype(_BF16)
    lanel = lax.broadcasted_iota(jnp.int32, (tile, LOCAL_ROWS), 1).astype(_F32)
    take = (jnp.where(lanel == lpos_ref[:, 0:1], route_ref[:, 2:3], 0.0)
            + jnp.where(lanel == lpos_ref[:, 1:2], route_ref[:, 3:4], 0.0)).astype(_BF16)
    o_ref[...] = _rmsnorm(h_ref[...] + _dot(take, ylb), fg_ref[...])


def _combine_call(plan, h, route, lpos, fg, ys, *, tile, tile0):
    rows = h.shape[0]
    smap = lambda i, *_: (i, 0)
    return pl.pallas_call(
        functools.partial(_combine_kernel, tile=tile, tile0=tile0),
        grid_spec=pltpu.PrefetchScalarGridSpec(
            num_scalar_prefetch=3,
            grid=(rows // tile,),
            in_specs=[
                pl.BlockSpec((tile, D_MODEL), smap),
                pl.BlockSpec((tile, ROUTE_LANES), smap),
                pl.BlockSpec((tile, ROUTE_LANES), smap),
                pl.BlockSpec((1, D_MODEL), lambda i, *_: (0, 0)),
                pl.BlockSpec(memory_space=pl.ANY),
            ],
            out_specs=pl.BlockSpec((tile, D_MODEL), smap),
            scratch_shapes=[pltpu.VMEM((2, LOCAL_ROWS, D_MODEL), _F32), pltpu.SemaphoreType.DMA((2,))],
        ),
        out_shape=jax.ShapeDtypeStruct((rows, D_MODEL), _F32),
        compiler_params=pltpu.CompilerParams(
            dimension_semantics=("arbitrary",), vmem_limit_bytes=VMEM_LIMIT),
        name="moe_combine",
    )(plan["seg_off"], plan["c8"], plan["lo"], h, route, lpos, fg, ys)


def _layer_weights(l, norm1_g, w_in, pool_w, pool_scale, conv_dw_w, conv_dw_b, conv_ln_g, conv_ln_b,
                   conv_pw_w, w_out, norm2_g):
    zero = jnp.zeros((POOL_GROUP, POOL_GROUP), _F32)
    pwl = pool_w[l]
    wpa = jnp.block([[pwl[0], zero], [zero, pwl[1]]])
    wpb = jnp.block([[pwl[2], zero], [zero, pwl[3]]])
    return {
        "g1": norm1_g[l][None, :], "w_in": w_in[l].astype(_BF16),
        "wpa": wpa.astype(_BF16), "wpb": wpb.astype(_BF16), "pscale": pool_scale[l][None, :],
        "dw": conv_dw_w[l], "dwb": conv_dw_b[l][None, :], "lng": conv_ln_g[l][None, :],
        "lnb": conv_ln_b[l][None, :], "pw": conv_pw_w[l].astype(_BF16), "w_out": w_out[l].astype(_BF16),
        "g2": norm2_g[l][None, :],
    }


def _routing_plan(experts, tok_tile):
    n = experts.shape[0]
    n_tok_tiles = n // tok_tile
    ids = jnp.arange(N_EXPERTS, dtype=jnp.int32)
    chosen = jnp.sum((experts[:, :, None] == ids[None, None, :]).astype(jnp.int32), axis=1)
    cnt = jnp.sum(chosen.reshape(n_tok_tiles, tok_tile, N_EXPERTS), axis=1)
    c8 = ((cnt + SEG_ALIGN - 1) // SEG_ALIGN) * SEG_ALIGN
    lo = jnp.cumsum(c8, axis=1) - c8
    within = jnp.cumsum(c8, axis=0) - c8
    glen = jnp.sum(c8, axis=0)
    gpad = ((glen + MOE_ROW_TILE - 1) // MOE_ROW_TILE) * MOE_ROW_TILE
    ends = jnp.cumsum(gpad)
    goff = ends - gpad
    seg_off = goff[None, :] + within
    max_rows = 2 * n + (SEG_ALIGN - 1) * N_EXPERTS * n_tok_tiles
    n_tiles = -(-max_rows // MOE_ROW_TILE) + N_EXPERTS
    starts = jnp.arange(n_tiles, dtype=jnp.int32) * MOE_ROW_TILE
    te = jnp.sum((starts[:, None] >= ends[None, :]).astype(jnp.int32), axis=1)
    n_used = (ends[-1] // MOE_ROW_TILE).astype(jnp.int32)
    te_c = jnp.minimum(te, N_EXPERTS - 1)
    valid = jnp.clip(goff[te_c] + glen[te_c] - starts, 0, MOE_ROW_TILE)
    valid = jnp.where(te < N_EXPERTS, valid, 0).astype(jnp.int32)
    last_e = te_c[jnp.maximum(n_used - 1, 0)]
    te_c = jnp.where(te < N_EXPERTS, te_c, last_e).astype(jnp.int32)
    flat = lambda a: a.reshape(-1).astype(jnp.int32)
    return {"seg_off": flat(seg_off), "c8": flat(c8), "lo": flat(lo), "tile_expert": te_c,
            "tile_valid": valid, "n_used": n_used.reshape(1), "n_tiles": n_tiles}


def kernel(x_prompt, x_sample, state_pool, state_conv, meta_tokens, norm1_g, w_in, pool_w, pool_scale,
           conv_dw_w, conv_dw_b, conv_ln_g, conv_ln_b, conv_pw_w, w_out, norm2_g, ffn_w_gate, ffn_w_up,
           ffn_w_down, router_w, moe_w_gate, moe_w_up, moe_w_down, final_norm_g):
    batch, seq, _ = x_prompt.shape
    dec_batch, dec_seq, _ = x_sample.shape
    assert dec_seq == 8 and seq % ROW_TILE == 0 and (dec_batch * dec_seq) % ROW_TILE == 0
    assert (dec_batch * dec_seq) % DEC_ROW_TILE == 0
    lws = [_layer_weights(l, norm1_g, w_in, pool_w, pool_scale, conv_dw_w, conv_dw_b, conv_ln_g, conv_ln_b,
                          conv_pw_w, w_out, norm2_g) for l in range(2)]
    lws[0].update(wg=ffn_w_gate[0].astype(_BF16), wu=ffn_w_up[0].astype(_BF16), wd=ffn_w_down[0].astype(_BF16))
    rw = jnp.zeros((D_MODEL, ROUTE_LANES), _F32).at[:, :N_EXPERTS].set(router_w[0])
    lws[1].update(rw=rw.astype(_BF16))

    n_p = batch * seq
    n_s = dec_batch * dec_seq
    n_t = seq // ROW_TILE
    hp = x_prompt.reshape(n_p, D_MODEL)
    hs = x_sample.reshape(n_s, D_MODEL)
    hm = meta_tokens.astype(_F32)
    zero_pp = jnp.zeros((1, POOL_HALO, POOL_WIDTH), _F32)
    zero_cp = jnp.zeros((1, CONV_HALO, CONV_WIDTH), _F32)
    sp_pad = jnp.pad(state_pool, ((0, 0), (0, 0), (POOL_HALO - POOL_BUF, 0), (0, 0)))
    sc_pad = jnp.pad(state_conv, ((0, 0), (0, 0), (CONV_HALO - CONV_BUF, 0), (0, 0)))

    seq_kw = dict(mode="seq", tile=ROW_TILE, n_seq=batch, n_t=n_t)
    meta_kw = dict(mode="seq", ramp=True, tile=N_META, n_seq=1, n_t=1)
    dec_kw = dict(mode="dec", ramp=False, tile=DEC_ROW_TILE, n_seq=0, n_t=0)

    hm, pst_m0, cst_m0 = _front_call(hm, zero_pp, zero_cp, lws[0], tail="ffn", **meta_kw)
    hp, pst_p0, cst_p0 = _front_call(hp, pst_m0, cst_m0, lws[0], ramp=False, tail="ffn", **seq_kw)
    hs, up_s0, v_s0 = _front_call(hs, sp_pad[0], sc_pad[0], lws[0], tail="ffn", **dec_kw)

    _, pst_m1, cst_m1 = _front_call(hm, zero_pp, zero_cp, lws[1], tail="none", **meta_kw)
    hp, pst_p1, cst_p1, hn_p, route_p = _front_call(hp, pst_m1, cst_m1, lws[1], ramp=False, tail="route", **seq_kw)
    hs, up_s1, v_s1, hn_s, route_s = _front_call(hs, sp_pad[1], sc_pad[1], lws[1], tail="route", **dec_kw)

    experts = jnp.concatenate([route_p[:, :2], route_s[:, :2]], axis=0).astype(jnp.int32)
    experts = jnp.clip(experts, 0, N_EXPERTS - 1)
    plan = _routing_plan(experts, ROW_TILE)
    s_tile0 = n_p // ROW_TILE

    xs = jnp.zeros((plan["n_tiles"] * MOE_ROW_TILE, D_MODEL), _F32)
    xs, lpos_p = _dispatch_call(plan, hn_p, route_p, xs, tile=ROW_TILE, tile0=0)
    xs, lpos_s = _dispatch_call(plan, hn_s, route_s, xs, tile=ROW_TILE, tile0=s_tile0)
    ys = _moe_call(plan["tile_expert"], plan["tile_valid"], plan["n_used"], xs,
                   moe_w_gate[0], moe_w_up[0], moe_w_down[0])

    fg = final_norm_g[None, :]
    y_p = _combine_call(plan, hp, route_p, lpos_p, fg, ys, tile=ROW_TILE, tile0=0)
    y_s = _combine_call(plan, hs, route_s, lpos_s, fg, ys, tile=ROW_TILE, tile0=s_tile0)

    new_pool_p = jnp.stack([pst_p0, pst_p1])[:, :, POOL_HALO - POOL_BUF:, :]
    new_conv_p = jnp.stack([cst_p0, cst_p1])[:, :, CONV_HALO - CONV_BUF:, :]
    up_s = jnp.stack([up_s0, up_s1]).reshape(2, dec_batch, dec_seq, POOL_WIDTH)
    v_s = jnp.stack([v_s0, v_s1]).reshape(2, dec_batch, dec_seq, CONV_WIDTH)
    new_pool_s = jnp.concatenate([state_pool, up_s], axis=2)[:, :, -POOL_BUF:, :]
    new_conv_s = jnp.concatenate([state_conv, v_s], axis=2)[:, :, -CONV_BUF:, :]
    return (y_p.reshape(batch, seq, D_MODEL), y_s.reshape(dec_batch, dec_seq, D_MODEL),
            new_pool_p, new_conv_p, new_pool_s, new_conv_s)
```

```python
import functools

import jax
import jax.numpy as jnp
from jax import lax
from jax.experimental import pallas as pl
from jax.experimental.pallas import tpu as pltpu

D_MODEL = 1024
N_META = 16
POOL_WIDTH = 512
CONV_WIDTH = 512
POOL_WINDOWS = (2, 4, 8, 16)
POOL_GROUP = 128
POOL_BUF = 15
CONV_KERNEL = 31
CONV_BUF = 30
IN_COLS = POOL_WIDTH + 2 * CONV_WIDTH
N_EXPERTS = 8
RMS_EPS = 1e-6
LN_EPS = 1e-5

POOL_HALO = 16
CONV_HALO = 32
CONV_SHIFT_PAD = CONV_HALO - 8
ROUTE_LANES = 128

ROW_TILE = 512
DEC_ROW_TILE = 256
MIX_CHUNK = 64
FFN_CHUNK = 512

MOE_ROW_TILE = 1024
MOE_SUB = 256
MOE_FF_TILE = 512
SEG_ALIGN = 8
LOCAL_ROWS = -(-(2 * ROW_TILE + (SEG_ALIGN - 1) * N_EXPERTS) // 128) * 128

VMEM_LIMIT = 56 * 1024 * 1024

_BF16 = jnp.bfloat16
_F32 = jnp.float32


def _dot(a, b):
    return jnp.dot(a, b, preferred_element_type=_F32)


def _sigmoid(x):
    return 1.0 / (1.0 + jnp.exp(-x))


def _rmsnorm(x, g):
    return x * lax.rsqrt(jnp.mean(x * x, axis=-1, keepdims=True) + RMS_EPS) * g


def _mix_chunk(ep_ref, ec_ref, r0, rows, dw_ref, dwb_ref, lng_ref, lnb_ref, pos0, sh_ref=None):
    def conv_rows(start, c0):
        if sh_ref is None or start % 8 == 0:
            return ec_ref[start:start + rows, c0:c0 + 128]
        q, r = divmod(start, 8)
        return sh_ref[r - 1, 8 * q:8 * q + rows, c0:c0 + 128]

    pooled_cols = []
    for gi, win in enumerate(POOL_WINDOWS):
        c0 = gi * POOL_GROUP
        base = POOL_HALO + r0
        x0 = ep_ref[base:base + rows, c0:c0 + POOL_GROUP]
        s = x0
        for k in range(1, win):
            s = s + ep_ref[base - k:base - k + rows, c0:c0 + POOL_GROUP]
        if pos0 is None:
            pooled = s * (1.0 / win) - x0
        else:
            pos = pos0 + r0 + lax.broadcasted_iota(jnp.int32, (rows, POOL_GROUP), 0)
            cnt = jnp.minimum(pos + 1, win).astype(_F32)
            pooled = s / cnt - x0
        pooled_cols.append(pooled)
    pooled = jnp.concatenate(pooled_cols, axis=-1)

    conv_cols = []
    for cg in range(CONV_WIDTH // 128):
        c0 = cg * 128
        base = CONV_HALO - CONV_BUF + r0
        acc = conv_rows(base, c0) * dw_ref[0:1, c0:c0 + 128]
        for k in range(1, CONV_KERNEL):
            acc = acc + conv_rows(base + k, c0) * dw_ref[k:k + 1, c0:c0 + 128]
        conv_cols.append(acc + dwb_ref[:, c0:c0 + 128])
    y = jnp.concatenate(conv_cols, axis=-1)
    mu = jnp.mean(y, axis=-1, keepdims=True)
    d = y - mu
    var = jnp.mean(d * d, axis=-1, keepdims=True)
    yn = d * lax.rsqrt(var + LN_EPS) * lng_ref[...] + lnb_ref[...]
    return pooled, yn * _sigmoid(yn)


def _front_kernel(*refs, mode, ramp, tail, tile, n_t, skew):
    it = iter(refs)
    h_ref, ppast_ref, cpast_ref = next(it), next(it), next(it)
    g1_ref, win_ref, wpa_ref, wpb_ref, psc_ref = next(it), next(it), next(it), next(it), next(it)
    dw_ref, dwb_ref, lng_ref, lnb_ref, pw_ref, wout_ref, g2_ref = (next(it) for _ in range(7))
    if tail == "ffn":
        wg_ref, wu_ref, wd_ref = next(it), next(it), next(it)
    elif tail == "route":
        rw_ref = next(it)
    hout_ref = next(it)
    if mode == "seq":
        pst_ref, cst_ref = next(it), next(it)
    else:
        up_ref, v_ref = next(it), next(it)
    if tail == "route":
        hn2_ref, route_ref = next(it), next(it)
    ep_ref, ec_ref, mixed_ref = next(it), next(it), next(it)
    if mode == "dec":
        sp_ref, sc_ref = next(it), next(it)
    else:
        sh_ref = next(it)
    if skew:
        h1_ref = next(it)

    def ffn_tail(h_in):
        hb = _rmsnorm(h_in, g2_ref[...]).astype(_BF16)
        d_ff = wg_ref.shape[1]
        hout_ref[...] = h_in
        for c0 in range(0, d_ff, FFN_CHUNK):
            cs = min(FFN_CHUNK, d_ff - c0)
            g = _dot(hb, wg_ref[:, c0:c0 + cs])
            a = (g * _sigmoid(g) * _dot(hb, wu_ref[:, c0:c0 + cs])).astype(_BF16)
            hout_ref[...] += _dot(a, wd_ref[c0:c0 + cs, :])

    if mode == "seq":
        if skew:
            step = pl.program_id(0)
            t = lax.rem(step, n_t)

            @pl.when(step == 0)
            def _():
                h1_ref[...] = jnp.zeros(h1_ref.shape, _F32)
        else:
            t = pl.program_id(1)

        @pl.when(t == 0)
        def _():
            ep_ref[0:POOL_HALO, :] = ppast_ref[0]
            ec_ref[0:CONV_HALO, :] = cpast_ref[0]

    if skew:
        ffn_tail(h1_ref[...])

    x = h_ref[...]
    hn = _rmsnorm(x, g1_ref[...]).astype(_BF16)
    u = _dot(hn, win_ref[...])
    up = u[:, :POOL_WIDTH]
    v = u[:, POOL_WIDTH:POOL_WIDTH + CONV_WIDTH] * _sigmoid(u[:, POOL_WIDTH + CONV_WIDTH:])
    ep_ref[POOL_HALO:, :] = up
    ec_ref[CONV_HALO:, :] = v

    if mode == "seq":
        for r in range(1, 8):
            sh_ref[r - 1] = ec_ref[r:r + tile + CONV_SHIFT_PAD, :]
        chunk = min(MIX_CHUNK, tile)
        for r0 in range(0, tile, chunk):
            pos0 = t * tile if ramp else None
            pooled, act = _mix_chunk(ep_ref, ec_ref, r0, chunk, dw_ref, dwb_ref, lng_ref, lnb_ref, pos0, sh_ref)
            mixed_ref[r0:r0 + chunk, 0:POOL_WIDTH] = pooled
            mixed_ref[r0:r0 + chunk, POOL_WIDTH:] = act
        ptail = ep_ref[tile:tile + POOL_HALO, :]
        ctail = ec_ref[tile:tile + CONV_HALO, :]
        ep_ref[0:POOL_HALO, :] = ptail
        ec_ref[0:CONV_HALO, :] = ctail
        pst_ref[0] = ptail
        cst_ref[0] = ctail
    else:
        up_ref[...] = up
        v_ref[...] = v

        def seq_body(s, carry):
            r = pl.multiple_of(s * 8, 8)
            sp_ref[0:POOL_HALO, :] = ppast_ref[s]
            sp_ref[POOL_HALO:, :] = ep_ref[pl.ds(POOL_HALO + r, 8), :]
            sc_ref[0:CONV_HALO, :] = cpast_ref[s]
            sc_ref[CONV_HALO:, :] = ec_ref[pl.ds(CONV_HALO + r, 8), :]
            pooled, act = _mix_chunk(sp_ref, sc_ref, 0, 8, dw_ref, dwb_ref, lng_ref, lnb_ref, None)
            mixed_ref[pl.ds(r, 8), 0:POOL_WIDTH] = pooled
            mixed_ref[pl.ds(r, 8), POOL_WIDTH:] = act
            return carry

        lax.fori_loop(0, tile // 8, seq_body, 0)

    pa = _dot(mixed_ref[:, 0:256].astype(_BF16), wpa_ref[...])
    pb = _dot(mixed_ref[:, 256:512].astype(_BF16), wpb_ref[...])
    pool_out = jnp.concatenate([pa, pb], axis=-1) * psc_ref[...]
    conv_out = _dot(mixed_ref[:, POOL_WIDTH:].astype(_BF16), pw_ref[...])
    h1 = (x + _dot(pool_out.astype(_BF16), wout_ref[0:POOL_WIDTH, :])
          + _dot(conv_out.astype(_BF16), wout_ref[POOL_WIDTH:, :]))

    if tail == "none":
        hout_ref[...] = h1
    elif skew:
        h1_ref[...] = h1
    elif tail == "ffn":
        ffn_tail(h1)
    else:
        hn2 = _rmsnorm(h1, g2_ref[...])
        hout_ref[...] = h1
        hn2_ref[...] = hn2
        logits = _dot(hn2.astype(_BF16), rw_ref[...])
        lane = lax.broadcasted_iota(jnp.int32, logits.shape, 1).astype(_F32)
        neg = jnp.float32(-jnp.inf)
        lg = jnp.where(lane < N_EXPERTS, logits, neg)
        m1 = jnp.max(lg, axis=-1, keepdims=True)
        i1 = jnp.min(jnp.where(lg == m1, lane, float(ROUTE_LANES)), axis=-1, keepdims=True)
        lg2 = jnp.where(lane == i1, neg, lg)
        m2 = jnp.max(lg2, axis=-1, keepdims=True)
        i2 = jnp.min(jnp.where(lg2 == m2, lane, float(ROUTE_LANES)), axis=-1, keepdims=True)
        e21 = jnp.exp(m2 - m1)
        gate1 = 1.0 / (1.0 + e21)
        gate2 = e21 * gate1
        route_ref[...] = jnp.where(lane == 0, i1, jnp.where(lane == 1, i2, jnp.where(
            lane == 2, gate1, jnp.where(lane == 3, gate2, 0.0))))


def _const_spec(shape):
    nd = len(shape)
    return pl.BlockSpec(shape, lambda *_: (0,) * nd, pipeline_mode=pl.Buffered(1))


def _front_call(h, ppast, cpast, lw, *, mode, ramp, tail, tile, n_seq, n_t, skew=False):
    rows = h.shape[0]
    weights = [lw["g1"], lw["w_in"], lw["wpa"], lw["wpb"], lw["pscale"], lw["dw"], lw["dwb"], lw["lng"],
               lw["lnb"], lw["pw"], lw["w_out"], lw["g2"]]
    if tail == "ffn":
        weights += [lw["wg"], lw["wu"], lw["wd"]]
    elif tail == "route":
        weights += [lw["rw"]]
    w_specs = [_const_spec(w.shape) for w in weights]

    if mode == "seq" and skew:
        assert tail == "ffn" and ppast.shape[0] == 1
        n_tiles = n_seq * n_t
        grid = (n_tiles + 1,)
        row_map = lambda s: (jnp.minimum(s, n_tiles - 1), 0)
        seq_map = lambda s: (jnp.minimum(s, n_tiles - 1) // n_t, 0, 0)
        past_specs = [pl.BlockSpec((1, POOL_HALO, POOL_WIDTH), lambda s: (0, 0, 0)),
                      pl.BlockSpec((1, CONV_HALO, CONV_WIDTH), lambda s: (0, 0, 0))]
        out_shape = [jax.ShapeDtypeStruct((rows, D_MODEL), _F32),
                     jax.ShapeDtypeStruct((n_seq, POOL_HALO, POOL_WIDTH), _F32),
                     jax.ShapeDtypeStruct((n_seq, CONV_HALO, CONV_WIDTH), _F32)]
        out_specs = [pl.BlockSpec((tile, D_MODEL), lambda s: (jnp.maximum(s - 1, 0), 0)),
                     pl.BlockSpec((1, POOL_HALO, POOL_WIDTH), seq_map),
                     pl.BlockSpec((1, CONV_HALO, CONV_WIDTH), seq_map)]
        semantics = ("arbitrary",)
    elif mode == "seq":
        grid = (n_seq, n_t)
        row_map = lambda b, t: (b * n_t + t, 0)
        shared = ppast.shape[0] == 1
        past_map = (lambda b, t: (0, 0, 0)) if shared else (lambda b, t: (b, 0, 0))
        past_specs = [pl.BlockSpec((1, POOL_HALO, POOL_WIDTH), past_map),
                      pl.BlockSpec((1, CONV_HALO, CONV_WIDTH), past_map)]
        out_shape = [jax.ShapeDtypeStruct((rows, D_MODEL), _F32),
                     jax.ShapeDtypeStruct((n_seq, POOL_HALO, POOL_WIDTH), _F32),
                     jax.ShapeDtypeStruct((n_seq, CONV_HALO, CONV_WIDTH), _F32)]
        out_specs = [pl.BlockSpec((tile, D_MODEL), row_map),
                     pl.BlockSpec((1, POOL_HALO, POOL_WIDTH), lambda b, t: (b, 0, 0)),
                     pl.BlockSpec((1, CONV_HALO, CONV_WIDTH), lambda b, t: (b, 0, 0))]
        semantics = ("arbitrary", "arbitrary")
    else:
        grid = (rows // tile,)
        row_map = lambda i: (i, 0)
        spt = tile // 8
        past_specs = [pl.BlockSpec((spt, POOL_HALO, POOL_WIDTH), lambda i: (i, 0, 0)),
                      pl.BlockSpec((spt, CONV_HALO, CONV_WIDTH), lambda i: (i, 0, 0))]
        out_shape = [jax.ShapeDtypeStruct((rows, D_MODEL), _F32),
                     jax.ShapeDtypeStruct((rows, POOL_WIDTH), _F32),
                     jax.ShapeDtypeStruct((rows, CONV_WIDTH), _F32)]
        out_specs = [pl.BlockSpec((tile, D_MODEL), row_map),
                     pl.BlockSpec((tile, POOL_WIDTH), row_map),
                     pl.BlockSpec((tile, CONV_WIDTH), row_map)]
        semantics = ("arbitrary",)
    if tail == "route":
        out_shape += [jax.ShapeDtypeStruct((rows, D_MODEL), _F32),
                      jax.ShapeDtypeStruct((rows, ROUTE_LANES), _F32)]
        out_specs += [pl.BlockSpec((tile, D_MODEL), row_map), pl.BlockSpec((tile, ROUTE_LANES), row_map)]

    scratch = [pltpu.VMEM((POOL_HALO + tile, POOL_WIDTH), _F32),
               pltpu.VMEM((CONV_HALO + tile, CONV_WIDTH), _F32),
               pltpu.VMEM((tile, POOL_WIDTH + CONV_WIDTH), _F32)]
    if mode == "dec":
        scratch += [pltpu.VMEM((POOL_HALO + 8, POOL_WIDTH), _F32), pltpu.VMEM((CONV_HALO + 8, CONV_WIDTH), _F32)]
    else:
        scratch += [pltpu.VMEM((7, tile + CONV_SHIFT_PAD, CONV_WIDTH), _F32)]
    if skew:
        scratch += [pltpu.VMEM((tile, D_MODEL), _F32)]

    body = functools.partial(_front_kernel, mode=mode, ramp=ramp, tail=tail, tile=tile, n_t=n_t, skew=skew)
    return pl.pallas_call(
        body,
        grid=grid,
        in_specs=[pl.BlockSpec((tile, D_MODEL), row_map)] + past_specs + w_specs,
        out_specs=out_specs,
        out_shape=out_shape,
        scratch_shapes=scratch,
        compiler_params=pltpu.CompilerParams(dimension_semantics=semantics, vmem_limit_bytes=VMEM_LIMIT),
        name=f"front_{mode}_{tail}" + ("_skew" if skew else ""),
    )(h, ppast, cpast, *weights)


def _segment_copies(c8_ref, lo_ref, segoff_ref, tidx, make_copy, act):
    for e in range(N_EXPERTS):
        k = tidx * N_EXPERTS + e
        lo_e = lo_ref[k]
        so_e = segoff_ref[k]

        def body(c, carry):
            act(make_copy(pl.multiple_of(lo_e + c * SEG_ALIGN, SEG_ALIGN),
                          pl.multiple_of(so_e + c * SEG_ALIGN, SEG_ALIGN)))
            return carry

        lax.fori_loop(0, c8_ref[k] // SEG_ALIGN, body, 0)


def _dispatch_kernel(segoff_ref, c8_ref, lo_ref, filloff_ref, filln_ref, hn_a_ref, route_a_ref, hn_b_ref,
                     route_b_ref, xs_hbm, lpos_ref, xl_ref, sem, zero_ref, fill_sem, *, tile, n_a):
    t = pl.program_id(0)
    n = pl.num_programs(0)
    slot = t % 2
    tt = t
    in_a = t < n_a
    hn = jnp.where(in_a, hn_a_ref[...], hn_b_ref[...])
    route = jnp.where(in_a, route_a_ref[...], route_b_ref[...])

    @pl.when(t == 0)
    def _():
        zero_ref[...] = jnp.zeros(zero_ref.shape, _F32)

        def fill(act):
            for r in range(N_EXPERTS + 1):
                off = filloff_ref[r]

                def body(c, carry):
                    dst = xs_hbm.at[pl.ds(pl.multiple_of(off + c * SEG_ALIGN, SEG_ALIGN), SEG_ALIGN)]
                    act(pltpu.make_async_copy(zero_ref, dst, fill_sem))
                    return carry

                lax.fori_loop(0, filln_ref[r], body, 0)

        fill(lambda c: c.start())
        fill(lambda c: c.wait())

    lane = lax.broadcasted_iota(jnp.int32, (tile, ROUTE_LANES), 1).astype(_F32)
    oh0 = (lane == route[:, 0:1]).astype(_F32)
    oh1 = (lane == route[:, 1:2]).astype(_F32)
    earlier = (lax.broadcasted_iota(jnp.int32, (tile, tile), 1)
               < lax.broadcasted_iota(jnp.int32, (tile, tile), 0)).astype(_BF16)
    rank = _dot(earlier, (oh0 + oh1).astype(_BF16))
    lane1 = lax.broadcasted_iota(jnp.int32, (1, ROUTE_LANES), 1)
    lo_vec = jnp.zeros((1, ROUTE_LANES), _F32)
    for e in range(N_EXPERTS):
        lo_vec = jnp.where(lane1 == e, lo_ref[tt * N_EXPERTS + e].astype(_F32), lo_vec)
    posv = rank + lo_vec
    p0 = jnp.sum(oh0 * posv, axis=-1, keepdims=True)
    p1 = jnp.sum(oh1 * posv, axis=-1, keepdims=True)
    slab = jnp.where(lane == 0, p0, jnp.where(lane == 1, p1, 0.0))
    lpos_ref[...] = slab
    slab_t = slab.T
    rowi = lax.broadcasted_iota(jnp.int32, (LOCAL_ROWS, tile), 0).astype(_F32)
    place = jnp.where(jnp.logical_or(rowi == slab_t[0:1, :], rowi == slab_t[1:2, :]), 1.0, 0.0).astype(_BF16)
    xl_ref[slot] = _dot(place, hn.astype(_BF16))

    def copies(tidx, s):
        def make_copy(lo, so):
            return pltpu.make_async_copy(xl_ref.at[s, pl.ds(lo, SEG_ALIGN)], xs_hbm.at[pl.ds(so, SEG_ALIGN)],
                                         sem.at[s])
        return functools.partial(_segment_copies, c8_ref, lo_ref, segoff_ref, tidx, make_copy)

    copies(tt, slot)(lambda c: c.start())

    @pl.when(t > 0)
    def _():
        copies(tt - 1, 1 - slot)(lambda c: c.wait())

    @pl.when(t == n - 1)
    def _():
        copies(tt, slot)(lambda c: c.wait())


def _dispatch_call(plan, hn_a, route_a, hn_b, route_b, *, tile):
    n_a = hn_a.shape[0] // tile
    n_b = hn_b.shape[0] // tile
    amap = lambda i, *_: (jnp.minimum(i, n_a - 1), 0)
    bmap = lambda i, *_: (jnp.maximum(i - n_a, 0), 0)
    tables = (plan["seg_off"], plan["c8"], plan["lo"], plan["fill_off"], plan["fill_n"])
    return pl.pallas_call(
        functools.partial(_dispatch_kernel, tile=tile, n_a=n_a),
        grid_spec=pltpu.PrefetchScalarGridSpec(
            num_scalar_prefetch=len(tables),
            grid=(n_a + n_b,),
            in_specs=[pl.BlockSpec((tile, D_MODEL), amap), pl.BlockSpec((tile, ROUTE_LANES), amap),
                      pl.BlockSpec((tile, D_MODEL), bmap), pl.BlockSpec((tile, ROUTE_LANES), bmap)],
            out_specs=[pl.BlockSpec(memory_space=pl.ANY), pl.BlockSpec((tile, ROUTE_LANES), lambda i, *_: (i, 0))],
            scratch_shapes=[pltpu.VMEM((2, LOCAL_ROWS, D_MODEL), _F32), pltpu.SemaphoreType.DMA((2,)),
                            pltpu.VMEM((SEG_ALIGN, D_MODEL), _F32), pltpu.SemaphoreType.DMA(())],
        ),
        out_shape=[jax.ShapeDtypeStruct((plan["n_tiles"] * MOE_ROW_TILE, D_MODEL), _F32),
                   jax.ShapeDtypeStruct(((n_a + n_b) * tile, ROUTE_LANES), _F32)],
        compiler_params=pltpu.CompilerParams(
            dimension_semantics=("arbitrary",), vmem_limit_bytes=VMEM_LIMIT, has_side_effects=True),
        name="moe_dispatch",
    )(*tables, hn_a, route_a, hn_b, route_b)


def _moe_kernel(te_ref, nv_ref, nu_ref, x_ref, wg_ref, wu_ref, wd_ref, o_ref, xb_ref, wgb_ref, wub_ref, wdb_ref):
    del te_ref, nu_ref
    i = pl.program_id(0)
    j = pl.program_id(1)
    nv = nv_ref[i]

    @pl.when(j == 0)
    def _():
        o_ref[...] = jnp.zeros(o_ref.shape, _F32)
        xb_ref[...] = x_ref[...].astype(_BF16)

    def expert_ffn(xs, wg, wu, wd):
        g = _dot(xs, wg)
        a = (g * _sigmoid(g) * _dot(xs, wu)).astype(_BF16)
        return _dot(a, wd)

    @pl.when(nv == MOE_ROW_TILE)
    def _():
        o_ref[...] += expert_ffn(xb_ref[...], wg_ref[0].astype(_BF16), wu_ref[0].astype(_BF16),
                                 wd_ref[0].astype(_BF16))

    @pl.when(jnp.logical_and(nv > 0, nv < MOE_ROW_TILE))
    def _():
        wgb_ref[...] = wg_ref[0].astype(_BF16)
        wub_ref[...] = wu_ref[0].astype(_BF16)
        wdb_ref[...] = wd_ref[0].astype(_BF16)
        for s in range(MOE_ROW_TILE // MOE_SUB):
            rows = slice(s * MOE_SUB, (s + 1) * MOE_SUB)

            @pl.when(nv > s * MOE_SUB)
            def _():
                o_ref[rows, :] += expert_ffn(xb_ref[rows, :], wgb_ref[...], wub_ref[...], wdb_ref[...])


def _moe_call(tile_expert, tile_valid, n_used, xs, wg, wu, wd):
    n_tiles = xs.shape[0] // MOE_ROW_TILE
    eff = wg.shape[2]
    n_j = eff // MOE_FF_TILE

    def row_map(i, j, te, nv, nu):
        return (jnp.maximum(jnp.minimum(i, nu[0] - 1), 0), 0)

    def col_of(i, j, nu):
        return jnp.where(i < nu[0], j, n_j - 1)

    return pl.pallas_call(
        _moe_kernel,
        grid_spec=pltpu.PrefetchScalarGridSpec(
            num_scalar_prefetch=3,
            grid=(n_tiles, n_j),
            in_specs=[
                pl.BlockSpec((MOE_ROW_TILE, D_MODEL), row_map),
                pl.BlockSpec((1, D_MODEL, MOE_FF_TILE), lambda i, j, te, nv, nu: (te[i], 0, col_of(i, j, nu))),
                pl.BlockSpec((1, D_MODEL, MOE_FF_TILE), lambda i, j, te, nv, nu: (te[i], 0, col_of(i, j, nu))),
                pl.BlockSpec((1, MOE_FF_TILE, D_MODEL), lambda i, j, te, nv, nu: (te[i], col_of(i, j, nu), 0)),
            ],
            out_specs=pl.BlockSpec((MOE_ROW_TILE, D_MODEL), lambda i, j, te, nv, nu: (i, 0)),
            scratch_shapes=[
                pltpu.VMEM((MOE_ROW_TILE, D_MODEL), _BF16),
                pltpu.VMEM((D_MODEL, MOE_FF_TILE), _BF16),
                pltpu.VMEM((D_MODEL, MOE_FF_TILE), _BF16),
                pltpu.VMEM((MOE_FF_TILE, D_MODEL), _BF16),
            ],
        ),
        out_shape=jax.ShapeDtypeStruct(xs.shape, _F32),
        compiler_params=pltpu.CompilerParams(
            dimension_semantics=("arbitrary", "arbitrary"), vmem_limit_bytes=VMEM_LIMIT),
        name="moe_grouped",
    )(tile_expert, tile_valid, n_used, xs, wg, wu, wd)


def _combine_kernel(segoff_ref, c8_ref, lo_ref, h_ref, route_ref, lpos_ref, fg_ref, ys_hbm, o_ref, yl_ref, sem,
                    *, tile, tile0):
    t = pl.program_id(0)
    n = pl.num_programs(0)
    slot = t % 2
    tt = tile0 + t

    def copies(tidx, s):
        def make_copy(lo, so):
            return pltpu.make_async_copy(ys_hbm.at[pl.ds(so, SEG_ALIGN)], yl_ref.at[s, pl.ds(lo, SEG_ALIGN)],
                                         sem.at[s])
        return functools.partial(_segment_copies, c8_ref, lo_ref, segoff_ref, tidx, make_copy)

    @pl.when(t == 0)
    def _():
        yl_ref[...] = jnp.zeros(yl_ref.shape, _F32)
        copies(tt, slot)(lambda c: c.start())

    @pl.when(t + 1 < n)
    def _():
        copies(tt + 1, 1 - slot)(lambda c: c.start())

    copies(tt, slot)(lambda c: c.wait())
    last = tt * N_EXPERTS + N_EXPERTS - 1
    used = lo_ref[last] + c8_ref[last]
    rowi = lax.broadcasted_iota(jnp.int32, (LOCAL_ROWS, 1), 0)
    ylb = jnp.where(rowi < used, yl_ref[slot], 0.0).astype(_BF16)
    lanel = lax.broadcasted_iota(jnp.int32, (tile, LOCAL_ROWS), 1).astype(_F32)
    take = (jnp.where(lanel == lpos_ref[:, 0:1], route_ref[:, 2:3], 0.0)
            + jnp.where(lanel == lpos_ref[:, 1:2], route_ref[:, 3:4], 0.0)).astype(_BF16)
    o_ref[...] = _rmsnorm(h_ref[...] + _dot(take, ylb), fg_ref[...])


def _combine_call(plan, h, route, lpos, fg, ys, *, tile, tile0):
    rows = h.shape[0]
    smap = lambda i, *_: (i, 0)
    return pl.pallas_call(
        functools.partial(_combine_kernel, tile=tile, tile0=tile0),
        grid_spec=pltpu.PrefetchScalarGridSpec(
            num_scalar_prefetch=3,
            grid=(rows // tile,),
            in_specs=[
                pl.BlockSpec((tile, D_MODEL), smap),
                pl.BlockSpec((tile, ROUTE_LANES), smap),
                pl.BlockSpec((tile, ROUTE_LANES), lambda i, *_: (i + tile0, 0)),
                pl.BlockSpec((1, D_MODEL), lambda i, *_: (0, 0)),
                pl.BlockSpec(memory_space=pl.ANY),
            ],
            out_specs=pl.BlockSpec((tile, D_MODEL), smap),
            scratch_shapes=[pltpu.VMEM((2, LOCAL_ROWS, D_MODEL), _F32), pltpu.SemaphoreType.DMA((2,))],
        ),
        out_shape=jax.ShapeDtypeStruct((rows, D_MODEL), _F32),
        compiler_params=pltpu.CompilerParams(
            dimension_semantics=("arbitrary",), vmem_limit_bytes=VMEM_LIMIT),
        name="moe_combine",
    )(plan["seg_off"], plan["c8"], plan["lo"], h, route, lpos, fg, ys)


def _layer_weights(l, norm1_g, w_in, pool_w, pool_scale, conv_dw_w, conv_dw_b, conv_ln_g, conv_ln_b,
                   conv_pw_w, w_out, norm2_g):
    zero = jnp.zeros((POOL_GROUP, POOL_GROUP), _F32)
    pwl = pool_w[l]
    wpa = jnp.block([[pwl[0], zero], [zero, pwl[1]]])
    wpb = jnp.block([[pwl[2], zero], [zero, pwl[3]]])
    return {
        "g1": norm1_g[l][None, :], "w_in": w_in[l].astype(_BF16),
        "wpa": wpa.astype(_BF16), "wpb": wpb.astype(_BF16), "pscale": pool_scale[l][None, :],
        "dw": conv_dw_w[l], "dwb": conv_dw_b[l][None, :], "lng": conv_ln_g[l][None, :],
        "lnb": conv_ln_b[l][None, :], "pw": conv_pw_w[l].astype(_BF16), "w_out": w_out[l].astype(_BF16),
        "g2": norm2_g[l][None, :],
    }


def _routing_plan(experts, tok_tile):
    n = experts.shape[0]
    n_tok_tiles = n // tok_tile
    ids = jnp.arange(N_EXPERTS, dtype=jnp.int32)
    chosen = jnp.sum((experts[:, :, None] == ids[None, None, :]).astype(jnp.int32), axis=1)
    cnt = jnp.sum(chosen.reshape(n_tok_tiles, tok_tile, N_EXPERTS), axis=1)
    c8 = ((cnt + SEG_ALIGN - 1) // SEG_ALIGN) * SEG_ALIGN
    lo = jnp.cumsum(c8, axis=1) - c8
    within = jnp.cumsum(c8, axis=0) - c8
    glen = jnp.sum(c8, axis=0)
    gpad = ((glen + MOE_ROW_TILE - 1) // MOE_ROW_TILE) * MOE_ROW_TILE
    ends = jnp.cumsum(gpad)
    goff = ends - gpad
    seg_off = goff[None, :] + within
    max_rows = 2 * n + (SEG_ALIGN - 1) * N_EXPERTS * n_tok_tiles
    n_tiles = -(-max_rows // MOE_ROW_TILE) + N_EXPERTS
    starts = jnp.arange(n_tiles, dtype=jnp.int32) * MOE_ROW_TILE
    te = jnp.sum((starts[:, None] >= ends[None, :]).astype(jnp.int32), axis=1)
    n_used = (ends[-1] // MOE_ROW_TILE).astype(jnp.int32)
    te_c = jnp.minimum(te, N_EXPERTS - 1)
    valid = jnp.clip(goff[te_c] + glen[te_c] - starts, 0, MOE_ROW_TILE)
    valid = jnp.where(te < N_EXPERTS, valid, 0).astype(jnp.int32)
    last_e = te_c[jnp.maximum(n_used - 1, 0)]
    te_c = jnp.where(te < N_EXPERTS, te_c, last_e).astype(jnp.int32)
    flat = lambda a: a.reshape(-1).astype(jnp.int32)
    total = jnp.full((1,), n_tiles * MOE_ROW_TILE, jnp.int32)
    fill_off = jnp.concatenate([goff + glen, ends[-1:]])
    fill_n = jnp.concatenate([gpad - glen, total - ends[-1:]]) // SEG_ALIGN
    return {"seg_off": flat(seg_off), "c8": flat(c8), "lo": flat(lo), "fill_off": flat(fill_off),
            "fill_n": flat(fill_n), "tile_expert": te_c, "tile_valid": valid, "n_used": n_used.reshape(1),
            "n_tiles": n_tiles}


def kernel(x_prompt, x_sample, state_pool, state_conv, meta_tokens, norm1_g, w_in, pool_w, pool_scale,
           conv_dw_w, conv_dw_b, conv_ln_g, conv_ln_b, conv_pw_w, w_out, norm2_g, ffn_w_gate, ffn_w_up,
           ffn_w_down, router_w, moe_w_gate, moe_w_up, moe_w_down, final_norm_g):
    batch, seq, _ = x_prompt.shape
    dec_batch, dec_seq, _ = x_sample.shape
    assert dec_seq == 8 and seq % ROW_TILE == 0 and (dec_batch * dec_seq) % ROW_TILE == 0
    assert (dec_batch * dec_seq) % DEC_ROW_TILE == 0
    lws = [_layer_weights(l, norm1_g, w_in, pool_w, pool_scale, conv_dw_w, conv_dw_b, conv_ln_g, conv_ln_b,
                          conv_pw_w, w_out, norm2_g) for l in range(2)]
    lws[0].update(wg=ffn_w_gate[0].astype(_BF16), wu=ffn_w_up[0].astype(_BF16), wd=ffn_w_down[0].astype(_BF16))
    rw = jnp.zeros((D_MODEL, ROUTE_LANES), _F32).at[:, :N_EXPERTS].set(router_w[0])
    lws[1].update(rw=rw.astype(_BF16))

    n_p = batch * seq
    n_s = dec_batch * dec_seq
    n_t = seq // ROW_TILE
    hp = x_prompt.reshape(n_p, D_MODEL)
    hs = x_sample.reshape(n_s, D_MODEL)
    hm = meta_tokens.astype(_F32)
    zero_pp = jnp.zeros((1, POOL_HALO, POOL_WIDTH), _F32)
    zero_cp = jnp.zeros((1, CONV_HALO, CONV_WIDTH), _F32)
    sp_pad = jnp.pad(state_pool, ((0, 0), (0, 0), (POOL_HALO - POOL_BUF, 0), (0, 0)))
    sc_pad = jnp.pad(state_conv, ((0, 0), (0, 0), (CONV_HALO - CONV_BUF, 0), (0, 0)))

    seq_kw = dict(mode="seq", tile=ROW_TILE, n_seq=batch, n_t=n_t)
    meta_kw = dict(mode="seq", ramp=True, tile=N_META, n_seq=1, n_t=1)
    dec_kw = dict(mode="dec", ramp=False, tile=DEC_ROW_TILE, n_seq=0, n_t=0)

    hm, pst_m0, cst_m0 = _front_call(hm, zero_pp, zero_cp, lws[0], tail="ffn", **meta_kw)
    hp, pst_p0, cst_p0 = _front_call(hp, pst_m0, cst_m0, lws[0], ramp=False, tail="ffn", skew=True, **seq_kw)
    hs, up_s0, v_s0 = _front_call(hs, sp_pad[0], sc_pad[0], lws[0], tail="ffn", **dec_kw)

    _, pst_m1, cst_m1 = _front_call(hm, zero_pp, zero_cp, lws[1], tail="none", **meta_kw)
    hp, pst_p1, cst_p1, hn_p, route_p = _front_call(hp, pst_m1, cst_m1, lws[1], ramp=False, tail="route", **seq_kw)
    hs, up_s1, v_s1, hn_s, route_s = _front_call(hs, sp_pad[1], sc_pad[1], lws[1], tail="route", **dec_kw)

    experts = jnp.concatenate([route_p[:, :2], route_s[:, :2]], axis=0).astype(jnp.int32)
    experts = jnp.clip(experts, 0, N_EXPERTS - 1)
    plan = _routing_plan(experts, ROW_TILE)
    s_tile0 = n_p // ROW_TILE

    xs, lpos = _dispatch_call(plan, hn_p, route_p, hn_s, route_s, tile=ROW_TILE)
    ys = _moe_call(plan["tile_expert"], plan["tile_valid"], plan["n_used"], xs,
                   moe_w_gate[0], moe_w_up[0], moe_w_down[0])

    fg = final_norm_g[None, :]
    y_p = _combine_call(plan, hp, route_p, lpos, fg, ys, tile=ROW_TILE, tile0=0)
    y_s = _combine_call(plan, hs, route_s, lpos, fg, ys, tile=ROW_TILE, tile0=s_tile0)

    new_pool_p = jnp.stack([pst_p0, pst_p1])[:, :, POOL_HALO - POOL_BUF:, :]
    new_conv_p = jnp.stack([cst_p0, cst_p1])[:, :, CONV_HALO - CONV_BUF:, :]
    up_s = jnp.stack([up_s0, up_s1]).reshape(2, dec_batch, dec_seq, POOL_WIDTH)
    v_s = jnp.stack([v_s0, v_s1]).reshape(2, dec_batch, dec_seq, CONV_WIDTH)
    new_pool_s = jnp.concatenate([state_pool, up_s], axis=2)[:, :, -POOL_BUF:, :]
    new_conv_s = jnp.concatenate([state_conv, v_s], axis=2)[:, :, -CONV_BUF:, :]
    return (y_p.reshape(batch, seq, D_MODEL), y_s.reshape(dec_batch, dec_seq, D_MODEL),
            new_pool_p, new_conv_p, new_pool_s, new_conv_s)
```

```python
import functools

import jax
import jax.numpy as jnp
from jax import lax
from jax.experimental import pallas as pl
from jax.experimental.pallas import tpu as pltpu

D_MODEL = 1024
N_META = 16
POOL_WIDTH = 512
CONV_WIDTH = 512
POOL_WINDOWS = (2, 4, 8, 16)
POOL_GROUP = 128
POOL_BUF = 15
CONV_KERNEL = 31
CONV_BUF = 30
IN_COLS = POOL_WIDTH + 2 * CONV_WIDTH
N_EXPERTS = 8
RMS_EPS = 1e-6
LN_EPS = 1e-5

POOL_HALO = 16
CONV_HALO = 32
CONV_SHIFT_PAD = CONV_HALO - 8
ROUTE_LANES = 128

ROW_TILE = 512
DEC_ROW_TILE = 256
MIX_CHUNK = 64
FFN_CHUNK = 512

MOE_ROW_TILE = 1024
MOE_SUB = 256
MOE_FF_TILE = 512
SEG_ALIGN = 8
LOCAL_ROWS = -(-(2 * ROW_TILE + (SEG_ALIGN - 1) * N_EXPERTS) // 128) * 128

VMEM_LIMIT = 56 * 1024 * 1024

_BF16 = jnp.bfloat16
_F32 = jnp.float32


def _dot(a, b):
    return jnp.dot(a, b, preferred_element_type=_F32)


def _sigmoid(x):
    return 1.0 / (1.0 + jnp.exp(-x))


def _rmsnorm(x, g):
    return x * lax.rsqrt(jnp.mean(x * x, axis=-1, keepdims=True) + RMS_EPS) * g


def _mix_chunk(ep_ref, ec_ref, r0, rows, dw_ref, dwb_ref, lng_ref, lnb_ref, pos0, sh_ref=None):
    def conv_rows(start, c0):
        if sh_ref is None or start % 8 == 0:
            return ec_ref[start:start + rows, c0:c0 + 128]
        q, r = divmod(start, 8)
        return sh_ref[r - 1, 8 * q:8 * q + rows, c0:c0 + 128]

    pooled_cols = []
    for gi, win in enumerate(POOL_WINDOWS):
        c0 = gi * POOL_GROUP
        base = POOL_HALO + r0
        x0 = ep_ref[base:base + rows, c0:c0 + POOL_GROUP]
        s = x0
        for k in range(1, win):
            s = s + ep_ref[base - k:base - k + rows, c0:c0 + POOL_GROUP]
        if pos0 is None:
            pooled = s * (1.0 / win) - x0
        else:
            pos = pos0 + r0 + lax.broadcasted_iota(jnp.int32, (rows, POOL_GROUP), 0)
            cnt = jnp.minimum(pos + 1, win).astype(_F32)
            pooled = s / cnt - x0
        pooled_cols.append(pooled)
    pooled = jnp.concatenate(pooled_cols, axis=-1)

    conv_cols = []
    for cg in range(CONV_WIDTH // 128):
        c0 = cg * 128
        base = CONV_HALO - CONV_BUF + r0
        acc = conv_rows(base, c0) * dw_ref[0:1, c0:c0 + 128]
        for k in range(1, CONV_KERNEL):
            acc = acc + conv_rows(base + k, c0) * dw_ref[k:k + 1, c0:c0 + 128]
        conv_cols.append(acc + dwb_ref[:, c0:c0 + 128])
    y = jnp.concatenate(conv_cols, axis=-1)
    mu = jnp.mean(y, axis=-1, keepdims=True)
    d = y - mu
    var = jnp.mean(d * d, axis=-1, keepdims=True)
    yn = d * lax.rsqrt(var + LN_EPS) * lng_ref[...] + lnb_ref[...]
    return pooled, yn * _sigmoid(yn)


def _front_kernel(*refs, mode, ramp, tail, tile, n_t, skew):
    it = iter(refs)
    h_ref, ppast_ref, cpast_ref = next(it), next(it), next(it)
    g1_ref, win_ref, wpa_ref, wpb_ref, psc_ref = next(it), next(it), next(it), next(it), next(it)
    dw_ref, dwb_ref, lng_ref, lnb_ref, pw_ref, wout_ref, g2_ref = (next(it) for _ in range(7))
    if tail == "ffn":
        wg_ref, wu_ref, wd_ref = next(it), next(it), next(it)
    elif tail == "route":
        rw_ref = next(it)
    hout_ref = next(it)
    if mode == "seq":
        pst_ref, cst_ref = next(it), next(it)
    else:
        up_ref, v_ref = next(it), next(it)
    if tail == "route":
        hn2_ref, route_ref, cnt_ref = next(it), next(it), next(it)
    ep_ref, ec_ref, mixed_ref = next(it), next(it), next(it)
    if mode == "dec":
        sp_ref, sc_ref = next(it), next(it)
    else:
        sh_ref = next(it)
    if skew:
        h1_ref = next(it)

    def ffn_pieces(h_in):
        hb = _rmsnorm(h_in, g2_ref[...]).astype(_BF16)
        d_ff = wg_ref.shape[1]
        hout_ref[...] = h_in

        def piece(c0):
            cs = min(FFN_CHUNK, d_ff - c0)
            g = _dot(hb, wg_ref[:, c0:c0 + cs])
            a = (g * _sigmoid(g) * _dot(hb, wu_ref[:, c0:c0 + cs])).astype(_BF16)
            hout_ref[...] += _dot(a, wd_ref[c0:c0 + cs, :])

        return [functools.partial(piece, c0) for c0 in range(0, d_ff, FFN_CHUNK)]

    def ffn_tail(h_in):
        for piece in ffn_pieces(h_in):
            piece()

    if mode == "seq":
        if skew:
            step = pl.program_id(0)
            t = lax.rem(step, n_t)

            @pl.when(step == 0)
            def _():
                h1_ref[...] = jnp.zeros(h1_ref.shape, _F32)
        else:
            t = pl.program_id(1)

        @pl.when(t == 0)
        def _():
            ep_ref[0:POOL_HALO, :] = ppast_ref[0]
            ec_ref[0:CONV_HALO, :] = cpast_ref[0]

    pending = ffn_pieces(h1_ref[...]) if skew else []

    x = h_ref[...]
    hn = _rmsnorm(x, g1_ref[...]).astype(_BF16)
    u = _dot(hn, win_ref[...])
    up = u[:, :POOL_WIDTH]
    v = u[:, POOL_WIDTH:POOL_WIDTH + CONV_WIDTH] * _sigmoid(u[:, POOL_WIDTH + CONV_WIDTH:])
    ep_ref[POOL_HALO:, :] = up
    ec_ref[CONV_HALO:, :] = v

    if mode == "seq":
        for r in range(1, 8):
            sh_ref[r - 1] = ec_ref[r:r + tile + CONV_SHIFT_PAD, :]
        chunk = min(MIX_CHUNK, tile)
        for r0 in range(0, tile, chunk):
            pos0 = t * tile if ramp else None
            pooled, act = _mix_chunk(ep_ref, ec_ref, r0, chunk, dw_ref, dwb_ref, lng_ref, lnb_ref, pos0, sh_ref)
            mixed_ref[r0:r0 + chunk, 0:POOL_WIDTH] = pooled
            mixed_ref[r0:r0 + chunk, POOL_WIDTH:] = act
            if pending:
                pending.pop(0)()
        while pending:
            pending.pop(0)()
        ptail = ep_ref[tile:tile + POOL_HALO, :]
        ctail = ec_ref[tile:tile + CONV_HALO, :]
        ep_ref[0:POOL_HALO, :] = ptail
        ec_ref[0:CONV_HALO, :] = ctail
        pst_ref[0] = ptail
        cst_ref[0] = ctail
    else:
        up_ref[...] = up
        v_ref[...] = v

        def seq_body(s, carry):
            r = pl.multiple_of(s * 8, 8)
            sp_ref[POOL_HALO - POOL_BUF:POOL_HALO, :] = ppast_ref[s]
            sp_ref[POOL_HALO:, :] = ep_ref[pl.ds(POOL_HALO + r, 8), :]
            sc_ref[CONV_HALO - CONV_BUF:CONV_HALO, :] = cpast_ref[s]
            sc_ref[CONV_HALO:, :] = ec_ref[pl.ds(CONV_HALO + r, 8), :]
            pooled, act = _mix_chunk(sp_ref, sc_ref, 0, 8, dw_ref, dwb_ref, lng_ref, lnb_ref, None)
            mixed_ref[pl.ds(r, 8), 0:POOL_WIDTH] = pooled
            mixed_ref[pl.ds(r, 8), POOL_WIDTH:] = act
            return carry

        lax.fori_loop(0, tile // 8, seq_body, 0)

    pa = _dot(mixed_ref[:, 0:256].astype(_BF16), wpa_ref[...])
    pb = _dot(mixed_ref[:, 256:512].astype(_BF16), wpb_ref[...])
    pool_out = jnp.concatenate([pa, pb], axis=-1) * psc_ref[...]
    conv_out = _dot(mixed_ref[:, POOL_WIDTH:].astype(_BF16), pw_ref[...])
    h1 = (x + _dot(pool_out.astype(_BF16), wout_ref[0:POOL_WIDTH, :])
          + _dot(conv_out.astype(_BF16), wout_ref[POOL_WIDTH:, :]))

    if tail == "none":
        hout_ref[...] = h1
    elif skew:
        h1_ref[...] = h1
    elif tail == "ffn":
        ffn_tail(h1)
    else:
        hn2 = _rmsnorm(h1, g2_ref[...])
        hout_ref[...] = h1
        hn2_ref[...] = hn2.astype(_BF16)
        logits = _dot(hn2.astype(_BF16), rw_ref[...])
        lane = lax.broadcasted_iota(jnp.int32, logits.shape, 1).astype(_F32)
        neg = jnp.float32(-jnp.inf)
        lg = jnp.where(lane < N_EXPERTS, logits, neg)
        m1 = jnp.max(lg, axis=-1, keepdims=True)
        i1 = jnp.min(jnp.where(lg == m1, lane, float(ROUTE_LANES)), axis=-1, keepdims=True)
        lg2 = jnp.where(lane == i1, neg, lg)
        m2 = jnp.max(lg2, axis=-1, keepdims=True)
        i2 = jnp.min(jnp.where(lg2 == m2, lane, float(ROUTE_LANES)), axis=-1, keepdims=True)
        e21 = jnp.exp(m2 - m1)
        gate1 = 1.0 / (1.0 + e21)
        gate2 = e21 * gate1
        route_ref[...] = jnp.where(lane == 0, i1, jnp.where(lane == 1, i2, jnp.where(
            lane == 2, gate1, jnp.where(lane == 3, gate2, 0.0))))
        chosen = (lane == i1).astype(_F32) + (lane == i2).astype(_F32)
        cnt_ref[...] = jnp.broadcast_to(jnp.sum(chosen, axis=0, keepdims=True), cnt_ref.shape)


def _const_spec(shape):
    nd = len(shape)
    return pl.BlockSpec(shape, lambda *_: (0,) * nd, pipeline_mode=pl.Buffered(1))


def _front_call(h, ppast, cpast, lw, *, mode, ramp, tail, tile, n_seq, n_t, skew=False, layer=None):
    rows = h.shape[0]
    weights = [lw["g1"], lw["w_in"], lw["wpa"], lw["wpb"], lw["pscale"], lw["dw"], lw["dwb"], lw["lng"],
               lw["lnb"], lw["pw"], lw["w_out"], lw["g2"]]
    if tail == "ffn":
        weights += [lw["wg"], lw["wu"], lw["wd"]]
    elif tail == "route":
        weights += [lw["rw"]]
    w_specs = [_const_spec(w.shape) for w in weights]

    if mode == "seq" and skew:
        assert tail == "ffn" and ppast.shape[0] == 1
        n_tiles = n_seq * n_t
        grid = (n_tiles + 1,)
        row_map = lambda s: (jnp.minimum(s, n_tiles - 1), 0)
        seq_map = lambda s: (jnp.minimum(s, n_tiles - 1) // n_t, 0, 0)
        past_specs = [pl.BlockSpec((1, POOL_HALO, POOL_WIDTH), lambda s: (0, 0, 0)),
                      pl.BlockSpec((1, CONV_HALO, CONV_WIDTH), lambda s: (0, 0, 0))]
        out_shape = [jax.ShapeDtypeStruct((rows, D_MODEL), _F32),
                     jax.ShapeDtypeStruct((n_seq, POOL_HALO, POOL_WIDTH), _F32),
                     jax.ShapeDtypeStruct((n_seq, CONV_HALO, CONV_WIDTH), _F32)]
        out_specs = [pl.BlockSpec((tile, D_MODEL), lambda s: (jnp.maximum(s - 1, 0), 0)),
                     pl.BlockSpec((1, POOL_HALO, POOL_WIDTH), seq_map),
                     pl.BlockSpec((1, CONV_HALO, CONV_WIDTH), seq_map)]
        semantics = ("arbitrary",)
    elif mode == "seq":
        grid = (n_seq, n_t)
        row_map = lambda b, t: (b * n_t + t, 0)
        shared = ppast.shape[0] == 1
        past_map = (lambda b, t: (0, 0, 0)) if shared else (lambda b, t: (b, 0, 0))
        past_specs = [pl.BlockSpec((1, POOL_HALO, POOL_WIDTH), past_map),
                      pl.BlockSpec((1, CONV_HALO, CONV_WIDTH), past_map)]
        out_shape = [jax.ShapeDtypeStruct((rows, D_MODEL), _F32),
                     jax.ShapeDtypeStruct((n_seq, POOL_HALO, POOL_WIDTH), _F32),
                     jax.ShapeDtypeStruct((n_seq, CONV_HALO, CONV_WIDTH), _F32)]
        out_specs = [pl.BlockSpec((tile, D_MODEL), row_map),
                     pl.BlockSpec((1, POOL_HALO, POOL_WIDTH), lambda b, t: (b, 0, 0)),
                     pl.BlockSpec((1, CONV_HALO, CONV_WIDTH), lambda b, t: (b, 0, 0))]
        semantics = ("arbitrary", "arbitrary")
    else:
        grid = (rows // tile,)
        row_map = lambda i: (i, 0)
        spt = tile // 8
        past_specs = [pl.BlockSpec((None, spt, POOL_BUF, POOL_WIDTH), lambda i: (layer, i, 0, 0)),
                      pl.BlockSpec((None, spt, CONV_BUF, CONV_WIDTH), lambda i: (layer, i, 0, 0))]
        out_shape = [jax.ShapeDtypeStruct((rows, D_MODEL), _F32),
                     jax.ShapeDtypeStruct((rows, POOL_WIDTH), _F32),
                     jax.ShapeDtypeStruct((rows, CONV_WIDTH), _F32)]
        out_specs = [pl.BlockSpec((tile, D_MODEL), row_map),
                     pl.BlockSpec((tile, POOL_WIDTH), row_map),
                     pl.BlockSpec((tile, CONV_WIDTH), row_map)]
        semantics = ("arbitrary",)
    if tail == "route":
        out_shape += [jax.ShapeDtypeStruct((rows, D_MODEL), _BF16),
                      jax.ShapeDtypeStruct((rows, ROUTE_LANES), _F32),
                      jax.ShapeDtypeStruct((rows // tile * 8, ROUTE_LANES), _F32)]
        out_specs += [pl.BlockSpec((tile, D_MODEL), row_map), pl.BlockSpec((tile, ROUTE_LANES), row_map),
                      pl.BlockSpec((8, ROUTE_LANES), row_map)]

    scratch = [pltpu.VMEM((POOL_HALO + tile, POOL_WIDTH), _F32),
               pltpu.VMEM((CONV_HALO + tile, CONV_WIDTH), _F32),
               pltpu.VMEM((tile, POOL_WIDTH + CONV_WIDTH), _F32)]
    if mode == "dec":
        scratch += [pltpu.VMEM((POOL_HALO + 8, POOL_WIDTH), _F32), pltpu.VMEM((CONV_HALO + 8, CONV_WIDTH), _F32)]
    else:
        scratch += [pltpu.VMEM((7, tile + CONV_SHIFT_PAD, CONV_WIDTH), _F32)]
    if skew:
        scratch += [pltpu.VMEM((tile, D_MODEL), _F32)]

    body = functools.partial(_front_kernel, mode=mode, ramp=ramp, tail=tail, tile=tile, n_t=n_t, skew=skew)
    return pl.pallas_call(
        body,
        grid=grid,
        in_specs=[pl.BlockSpec((tile, D_MODEL), row_map)] + past_specs + w_specs,
        out_specs=out_specs,
        out_shape=out_shape,
        scratch_shapes=scratch,
        compiler_params=pltpu.CompilerParams(dimension_semantics=semantics, vmem_limit_bytes=VMEM_LIMIT),
        name=f"front_{mode}_{tail}" + ("_skew" if skew else ""),
    )(h, ppast, cpast, *weights)


def _segment_copies(c8_ref, lo_ref, segoff_ref, tidx, make_copy, act):
    for e in range(N_EXPERTS):
        k = tidx * N_EXPERTS + e
        lo_e = lo_ref[k]
        so_e = segoff_ref[k]

        def body(c, carry):
            act(make_copy(pl.multiple_of(lo_e + c * SEG_ALIGN, SEG_ALIGN),
                          pl.multiple_of(so_e + c * SEG_ALIGN, SEG_ALIGN)))
            return carry

        lax.fori_loop(0, c8_ref[k] // SEG_ALIGN, body, 0)


def _dispatch_kernel(segoff_ref, c8_ref, lo_ref, filloff_ref, filln_ref, hn_a_ref, route_a_ref, hn_b_ref,
                     route_b_ref, xs_hbm, lpos_ref, xl_ref, sem, zero_ref, fill_sem, *, tile, n_a):
    t = pl.program_id(0)
    n = pl.num_programs(0)
    slot = t % 2
    tt = t
    in_a = t < n_a
    hn = jnp.where(in_a, hn_a_ref[...], hn_b_ref[...])
    route = jnp.where(in_a, route_a_ref[...], route_b_ref[...])

    @pl.when(t == 0)
    def _():
        zero_ref[...] = jnp.zeros(zero_ref.shape, _F32)

        def fill(act):
            for r in range(N_EXPERTS + 1):
                off = filloff_ref[r]

                def body(c, carry):
                    dst = xs_hbm.at[pl.ds(pl.multiple_of(off + c * SEG_ALIGN, SEG_ALIGN), SEG_ALIGN)]
                    act(pltpu.make_async_copy(zero_ref, dst, fill_sem))
                    return carry

                lax.fori_loop(0, filln_ref[r], body, 0)

        fill(lambda c: c.start())
        fill(lambda c: c.wait())

    lane = lax.broadcasted_iota(jnp.int32, (tile, ROUTE_LANES), 1).astype(_F32)
    oh0 = (lane == route[:, 0:1]).astype(_F32)
    oh1 = (lane == route[:, 1:2]).astype(_F32)
    earlier = (lax.broadcasted_iota(jnp.int32, (tile, tile), 1)
               < lax.broadcasted_iota(jnp.int32, (tile, tile), 0)).astype(_BF16)
    rank = _dot(earlier, (oh0 + oh1).astype(_BF16))
    lane1 = lax.broadcasted_iota(jnp.int32, (1, ROUTE_LANES), 1)
    lo_vec = jnp.zeros((1, ROUTE_LANES), _F32)
    for e in range(N_EXPERTS):
        lo_vec = jnp.where(lane1 == e, lo_ref[tt * N_EXPERTS + e].astype(_F32), lo_vec)
    posv = rank + lo_vec
    p0 = jnp.sum(oh0 * posv, axis=-1, keepdims=True)
    p1 = jnp.sum(oh1 * posv, axis=-1, keepdims=True)
    slab = jnp.where(lane == 0, p0, jnp.where(lane == 1, p1, 0.0))
    lpos_ref[...] = slab
    slab_t = slab.T
    rowi = lax.broadcasted_iota(jnp.int32, (LOCAL_ROWS, tile), 0).astype(_F32)
    place = jnp.where(jnp.logical_or(rowi == slab_t[0:1, :], rowi == slab_t[1:2, :]), 1.0, 0.0).astype(_BF16)
    xl_ref[slot] = _dot(place, hn.astype(_BF16))

    def copies(tidx, s):
        def make_copy(lo, so):
            return pltpu.make_async_copy(xl_ref.at[s, pl.ds(lo, SEG_ALIGN)], xs_hbm.at[pl.ds(so, SEG_ALIGN)],
                                         sem.at[s])
        return functools.partial(_segment_copies, c8_ref, lo_ref, segoff_ref, tidx, make_copy)

    copies(tt, slot)(lambda c: c.start())

    @pl.when(t > 0)
    def _():
        copies(tt - 1, 1 - slot)(lambda c: c.wait())

    @pl.when(t == n - 1)
    def _():
        copies(tt, slot)(lambda c: c.wait())


def _dispatch_call(plan, hn_a, route_a, hn_b, route_b, *, tile):
    n_a = hn_a.shape[0] // tile
    n_b = hn_b.shape[0] // tile
    amap = lambda i, *_: (jnp.minimum(i, n_a - 1), 0)
    bmap = lambda i, *_: (jnp.maximum(i - n_a, 0), 0)
    tables = (plan["seg_off"], plan["c8"], plan["lo"], plan["fill_off"], plan["fill_n"])
    return pl.pallas_call(
        functools.partial(_dispatch_kernel, tile=tile, n_a=n_a),
        grid_spec=pltpu.PrefetchScalarGridSpec(
            num_scalar_prefetch=len(tables),
            grid=(n_a + n_b,),
            in_specs=[pl.BlockSpec((tile, D_MODEL), amap), pl.BlockSpec((tile, ROUTE_LANES), amap),
                      pl.BlockSpec((tile, D_MODEL), bmap), pl.BlockSpec((tile, ROUTE_LANES), bmap)],
            out_specs=[pl.BlockSpec(memory_space=pl.ANY), pl.BlockSpec((tile, ROUTE_LANES), lambda i, *_: (i, 0))],
            scratch_shapes=[pltpu.VMEM((2, LOCAL_ROWS, D_MODEL), _F32), pltpu.SemaphoreType.DMA((2,)),
                            pltpu.VMEM((SEG_ALIGN, D_MODEL), _F32), pltpu.SemaphoreType.DMA(())],
        ),
        out_shape=[jax.ShapeDtypeStruct((plan["n_tiles"] * MOE_ROW_TILE, D_MODEL), _F32),
                   jax.ShapeDtypeStruct(((n_a + n_b) * tile, ROUTE_LANES), _F32)],
        compiler_params=pltpu.CompilerParams(
            dimension_semantics=("arbitrary",), vmem_limit_bytes=VMEM_LIMIT, has_side_effects=True),
        name="moe_dispatch",
    )(*tables, hn_a, route_a, hn_b, route_b)


def _moe_kernel(te_ref, nv_ref, nu_ref, x_ref, wg_ref, wu_ref, wd_ref, o_ref, xb_ref, wgb_ref, wub_ref, wdb_ref):
    del te_ref, nu_ref
    i = pl.program_id(0)
    j = pl.program_id(1)
    nv = nv_ref[i]

    @pl.when(j == 0)
    def _():
        o_ref[...] = jnp.zeros(o_ref.shape, _F32)
        xb_ref[...] = x_ref[...].astype(_BF16)

    def expert_ffn(xs, wg, wu, wd):
        g = _dot(xs, wg)
        a = (g * _sigmoid(g) * _dot(xs, wu)).astype(_BF16)
        return _dot(a, wd)

    @pl.when(nv == MOE_ROW_TILE)
    def _():
        o_ref[...] += expert_ffn(xb_ref[...], wg_ref[0].astype(_BF16), wu_ref[0].astype(_BF16),
                                 wd_ref[0].astype(_BF16))

    @pl.when(jnp.logical_and(nv > 0, nv < MOE_ROW_TILE))
    def _():
        wgb_ref[...] = wg_ref[0].astype(_BF16)
        wub_ref[...] = wu_ref[0].astype(_BF16)
        wdb_ref[...] = wd_ref[0].astype(_BF16)
        for s in range(MOE_ROW_TILE // MOE_SUB):
            rows = slice(s * MOE_SUB, (s + 1) * MOE_SUB)

            @pl.when(nv > s * MOE_SUB)
            def _():
                o_ref[rows, :] += expert_ffn(xb_ref[rows, :], wgb_ref[...], wub_ref[...], wdb_ref[...])


def _moe_call(tile_expert, tile_valid, n_used, xs, wg, wu, wd):
    n_tiles = xs.shape[0] // MOE_ROW_TILE
    eff = wg.shape[2]
    n_j = eff // MOE_FF_TILE

    def row_map(i, j, te, nv, nu):
        return (jnp.maximum(jnp.minimum(i, nu[0] - 1), 0), 0)

    def col_of(i, j, nu):
        return jnp.where(i < nu[0], j, n_j - 1)

    return pl.pallas_call(
        _moe_kernel,
        grid_spec=pltpu.PrefetchScalarGridSpec(
            num_scalar_prefetch=3,
            grid=(n_tiles, n_j),
            in_specs=[
                pl.BlockSpec((MOE_ROW_TILE, D_MODEL), row_map),
                pl.BlockSpec((1, D_MODEL, MOE_FF_TILE), lambda i, j, te, nv, nu: (te[i], 0, col_of(i, j, nu))),
                pl.BlockSpec((1, D_MODEL, MOE_FF_TILE), lambda i, j, te, nv, nu: (te[i], 0, col_of(i, j, nu))),
                pl.BlockSpec((1, MOE_FF_TILE, D_MODEL), lambda i, j, te, nv, nu: (te[i], col_of(i, j, nu), 0)),
            ],
            out_specs=pl.BlockSpec((MOE_ROW_TILE, D_MODEL), lambda i, j, te, nv, nu: (i, 0)),
            scratch_shapes=[
                pltpu.VMEM((MOE_ROW_TILE, D_MODEL), _BF16),
                pltpu.VMEM((D_MODEL, MOE_FF_TILE), _BF16),
                pltpu.VMEM((D_MODEL, MOE_FF_TILE), _BF16),
                pltpu.VMEM((MOE_FF_TILE, D_MODEL), _BF16),
            ],
        ),
        out_shape=jax.ShapeDtypeStruct(xs.shape, _F32),
        compiler_params=pltpu.CompilerParams(
            dimension_semantics=("arbitrary", "arbitrary"), vmem_limit_bytes=VMEM_LIMIT),
        name="moe_grouped",
    )(tile_expert, tile_valid, n_used, xs, wg, wu, wd)


def _combine_kernel(segoff_ref, c8_ref, lo_ref, h_ref, route_ref, lpos_ref, fg_ref, ys_hbm, o_ref, yl_ref, sem,
                    *, tile, tile0):
    t = pl.program_id(0)
    n = pl.num_programs(0)
    slot = t % 2
    tt = tile0 + t

    def copies(tidx, s):
        def make_copy(lo, so):
            return pltpu.make_async_copy(ys_hbm.at[pl.ds(so, SEG_ALIGN)], yl_ref.at[s, pl.ds(lo, SEG_ALIGN)],
                                         sem.at[s])
        return functools.partial(_segment_copies, c8_ref, lo_ref, segoff_ref, tidx, make_copy)

    @pl.when(t == 0)
    def _():
        yl_ref[...] = jnp.zeros(yl_ref.shape, _F32)
        copies(tt, slot)(lambda c: c.start())

    @pl.when(t + 1 < n)
    def _():
        copies(tt + 1, 1 - slot)(lambda c: c.start())

    copies(tt, slot)(lambda c: c.wait())
    last = tt * N_EXPERTS + N_EXPERTS - 1
    used = lo_ref[last] + c8_ref[last]
    rowi = lax.broadcasted_iota(jnp.int32, (LOCAL_ROWS, 1), 0)
    ylb = jnp.where(rowi < used, yl_ref[slot], 0.0).astype(_BF16)
    lanel = lax.broadcasted_iota(jnp.int32, (tile, LOCAL_ROWS), 1).astype(_F32)
    take = (jnp.where(lanel == lpos_ref[:, 0:1], route_ref[:, 2:3], 0.0)
            + jnp.where(lanel == lpos_ref[:, 1:2], route_ref[:, 3:4], 0.0)).astype(_BF16)
    o_ref[...] = _rmsnorm(h_ref[...] + _dot(take, ylb), fg_ref[...])


def _combine_call(plan, h, route, lpos, fg, ys, *, tile, tile0):
    rows = h.shape[0]
    smap = lambda i, *_: (i, 0)
    return pl.pallas_call(
        functools.partial(_combine_kernel, tile=tile, tile0=tile0),
        grid_spec=pltpu.PrefetchScalarGridSpec(
            num_scalar_prefetch=3,
            grid=(rows // tile,),
            in_specs=[
                pl.BlockSpec((tile, D_MODEL), smap),
                pl.BlockSpec((tile, ROUTE_LANES), smap),
                pl.BlockSpec((tile, ROUTE_LANES), lambda i, *_: (i + tile0, 0)),
                pl.BlockSpec((1, D_MODEL), lambda i, *_: (0, 0)),
                pl.BlockSpec(memory_space=pl.ANY),
            ],
            out_specs=pl.BlockSpec((tile, D_MODEL), smap),
            scratch_shapes=[pltpu.VMEM((2, LOCAL_ROWS, D_MODEL), _F32), pltpu.SemaphoreType.DMA((2,))],
        ),
        out_shape=jax.ShapeDtypeStruct((rows, D_MODEL), _F32),
        compiler_params=pltpu.CompilerParams(
            dimension_semantics=("arbitrary",), vmem_limit_bytes=VMEM_LIMIT),
        name="moe_combine",
    )(plan["seg_off"], plan["c8"], plan["lo"], h, route, lpos, fg, ys)


def _layer_weights(l, norm1_g, w_in, pool_w, pool_scale, conv_dw_w, conv_dw_b, conv_ln_g, conv_ln_b,
                   conv_pw_w, w_out, norm2_g):
    zero = jnp.zeros((POOL_GROUP, POOL_GROUP), _F32)
    pwl = pool_w[l]
    wpa = jnp.block([[pwl[0], zero], [zero, pwl[1]]])
    wpb = jnp.block([[pwl[2], zero], [zero, pwl[3]]])
    return {
        "g1": norm1_g[l][None, :], "w_in": w_in[l].astype(_BF16),
        "wpa": wpa.astype(_BF16), "wpb": wpb.astype(_BF16), "pscale": pool_scale[l][None, :],
        "dw": conv_dw_w[l], "dwb": conv_dw_b[l][None, :], "lng": conv_ln_g[l][None, :],
        "lnb": conv_ln_b[l][None, :], "pw": conv_pw_w[l].astype(_BF16), "w_out": w_out[l].astype(_BF16),
        "g2": norm2_g[l][None, :],
    }


def _routing_plan(cnt):
    n_tok_tiles = cnt.shape[0]
    n = n_tok_tiles * ROW_TILE
    c8 = ((cnt + SEG_ALIGN - 1) // SEG_ALIGN) * SEG_ALIGN
    lo = jnp.cumsum(c8, axis=1) - c8
    within = jnp.cumsum(c8, axis=0) - c8
    glen = jnp.sum(c8, axis=0)
    gpad = ((glen + MOE_ROW_TILE - 1) // MOE_ROW_TILE) * MOE_ROW_TILE
    ends = jnp.cumsum(gpad)
    goff = ends - gpad
    seg_off = goff[None, :] + within
    max_rows = 2 * n + (SEG_ALIGN - 1) * N_EXPERTS * n_tok_tiles
    n_tiles = -(-max_rows // MOE_ROW_TILE) + N_EXPERTS
    starts = jnp.arange(n_tiles, dtype=jnp.int32) * MOE_ROW_TILE
    te = jnp.sum((starts[:, None] >= ends[None, :]).astype(jnp.int32), axis=1)
    n_used = (ends[-1] // MOE_ROW_TILE).astype(jnp.int32)
    te_c = jnp.minimum(te, N_EXPERTS - 1)
    valid = jnp.clip(goff[te_c] + glen[te_c] - starts, 0, MOE_ROW_TILE)
    valid = jnp.where(te < N_EXPERTS, valid, 0).astype(jnp.int32)
    last_e = te_c[jnp.maximum(n_used - 1, 0)]
    te_c = jnp.where(te < N_EXPERTS, te_c, last_e).astype(jnp.int32)
    flat = lambda a: a.reshape(-1).astype(jnp.int32)
    total = jnp.full((1,), n_tiles * MOE_ROW_TILE, jnp.int32)
    fill_off = jnp.concatenate([goff + glen, ends[-1:]])
    fill_n = jnp.concatenate([gpad - glen, total - ends[-1:]]) // SEG_ALIGN
    return {"seg_off": flat(seg_off), "c8": flat(c8), "lo": flat(lo), "fill_off": flat(fill_off),
            "fill_n": flat(fill_n), "tile_expert": te_c, "tile_valid": valid, "n_used": n_used.reshape(1),
            "n_tiles": n_tiles}


def kernel(x_prompt, x_sample, state_pool, state_conv, meta_tokens, norm1_g, w_in, pool_w, pool_scale,
           conv_dw_w, conv_dw_b, conv_ln_g, conv_ln_b, conv_pw_w, w_out, norm2_g, ffn_w_gate, ffn_w_up,
           ffn_w_down, router_w, moe_w_gate, moe_w_up, moe_w_down, final_norm_g):
    batch, seq, _ = x_prompt.shape
    dec_batch, dec_seq, _ = x_sample.shape
    assert dec_seq == 8 and seq % ROW_TILE == 0 and (dec_batch * dec_seq) % ROW_TILE == 0
    assert (dec_batch * dec_seq) % DEC_ROW_TILE == 0
    lws = [_layer_weights(l, norm1_g, w_in, pool_w, pool_scale, conv_dw_w, conv_dw_b, conv_ln_g, conv_ln_b,
                          conv_pw_w, w_out, norm2_g) for l in range(2)]
    lws[0].update(wg=ffn_w_gate[0].astype(_BF16), wu=ffn_w_up[0].astype(_BF16), wd=ffn_w_down[0].astype(_BF16))
    rw = jnp.zeros((D_MODEL, ROUTE_LANES), _F32).at[:, :N_EXPERTS].set(router_w[0])
    lws[1].update(rw=rw.astype(_BF16))

    n_p = batch * seq
    n_s = dec_batch * dec_seq
    n_t = seq // ROW_TILE
    hp = x_prompt.reshape(n_p, D_MODEL)
    hs = x_sample.reshape(n_s, D_MODEL)
    hm = meta_tokens.astype(_F32)
    zero_pp = jnp.zeros((1, POOL_HALO, POOL_WIDTH), _F32)
    zero_cp = jnp.zeros((1, CONV_HALO, CONV_WIDTH), _F32)

    seq_kw = dict(mode="seq", tile=ROW_TILE, n_seq=batch, n_t=n_t)
    meta_kw = dict(mode="seq", ramp=True, tile=N_META, n_seq=1, n_t=1)
    dec_kw = dict(mode="dec", ramp=False, tile=DEC_ROW_TILE, n_seq=0, n_t=0)

    hm, pst_m0, cst_m0 = _front_call(hm, zero_pp, zero_cp, lws[0], tail="ffn", **meta_kw)
    hp, pst_p0, cst_p0 = _front_call(hp, pst_m0, cst_m0, lws[0], ramp=False, tail="ffn", skew=True, **seq_kw)
    hs, up_s0, v_s0 = _front_call(hs, state_pool, state_conv, lws[0], tail="ffn", layer=0, **dec_kw)

    _, pst_m1, cst_m1 = _front_call(hm, zero_pp, zero_cp, lws[1], tail="none", **meta_kw)
    hp, pst_p1, cst_p1, hn_p, route_p, cnt_p = _front_call(hp, pst_m1, cst_m1, lws[1], ramp=False, tail="route", **seq_kw)
    hs, up_s1, v_s1, hn_s, route_s, cnt_s = _front_call(hs, state_pool, state_conv, lws[1], tail="route", layer=1, **dec_kw)

    per_tile = lambda c, tile: c[::8, :N_EXPERTS].reshape(-1, ROW_TILE // tile, N_EXPERTS).sum(axis=1)
    cnt = jnp.concatenate([per_tile(cnt_p, ROW_TILE), per_tile(cnt_s, DEC_ROW_TILE)], axis=0)
    plan = _routing_plan(cnt.astype(jnp.int32))
    s_tile0 = n_p // ROW_TILE

    xs, lpos = _dispatch_call(plan, hn_p, route_p, hn_s, route_s, tile=ROW_TILE)
    ys = _moe_call(plan["tile_expert"], plan["tile_valid"], plan["n_used"], xs,
                   moe_w_gate[0], moe_w_up[0], moe_w_down[0])

    fg = final_norm_g[None, :]
    y_p = _combine_call(plan, hp, route_p, lpos, fg, ys, tile=ROW_TILE, tile0=0)
    y_s = _combine_call(plan, hs, route_s, lpos, fg, ys, tile=ROW_TILE, tile0=s_tile0)

    new_pool_p = jnp.stack([pst_p0, pst_p1])[:, :, POOL_HALO - POOL_BUF:, :]
    new_conv_p = jnp.stack([cst_p0, cst_p1])[:, :, CONV_HALO - CONV_BUF:, :]
    up_s = jnp.stack([up_s0, up_s1]).reshape(2, dec_batch, dec_seq, POOL_WIDTH)
    v_s = jnp.stack([v_s0, v_s1]).reshape(2, dec_batch, dec_seq, CONV_WIDTH)
    new_pool_s = jnp.concatenate([state_pool, up_s], axis=2)[:, :, -POOL_BUF:, :]
    new_conv_s = jnp.concatenate([state_conv, v_s], axis=2)[:, :, -CONV_BUF:, :]
    return (y_p.reshape(batch, seq, D_MODEL), y_s.reshape(dec_batch, dec_seq, D_MODEL),
            new_pool_p, new_conv_p, new_pool_s, new_conv_s)
```

```python
import functools

import jax
import jax.numpy as jnp
from jax import lax
from jax.experimental import pallas as pl
from jax.experimental.pallas import tpu as pltpu

D_MODEL = 1024
N_META = 16
POOL_WIDTH = 512
CONV_WIDTH = 512
POOL_WINDOWS = (2, 4, 8, 16)
POOL_GROUP = 128
POOL_BUF = 15
CONV_KERNEL = 31
CONV_BUF = 30
IN_COLS = POOL_WIDTH + 2 * CONV_WIDTH
N_EXPERTS = 8
DEC_SEQ = 8
RMS_EPS = 1e-6
LN_EPS = 1e-5

POOL_HALO = 16
CONV_HALO = 32
CONV_SHIFT_PAD = CONV_HALO - 8
ROUTE_LANES = 128

ROW_TILE = 512
DEC_ROW_TILE = 256
MIX_CHUNK = 64
FFN_CHUNK = 512

MOE_ROW_TILE = 1024
MOE_SUB = 256
MOE_FF_TILE = 512
SEG_ALIGN = 8
LOCAL_ROWS = -(-(2 * ROW_TILE + (SEG_ALIGN - 1) * N_EXPERTS) // 128) * 128

VMEM_LIMIT = 56 * 1024 * 1024

_BF16 = jnp.bfloat16
_F32 = jnp.float32


def _dot(a, b):
    return jnp.dot(a, b, preferred_element_type=_F32)


def _sigmoid(x):
    return 1.0 / (1.0 + jnp.exp(-x))


def _rmsnorm(x, g):
    return x * lax.rsqrt(jnp.mean(x * x, axis=-1, keepdims=True) + RMS_EPS) * g


def _mix_chunk(ep_ref, ec_ref, r0, rows, dw_ref, dwb_ref, lng_ref, lnb_ref, pos0, sh_ref=None):
    def conv_rows(start, c0):
        if sh_ref is None or start % 8 == 0:
            return ec_ref[start:start + rows, c0:c0 + 128]
        q, r = divmod(start, 8)
        return sh_ref[r - 1, 8 * q:8 * q + rows, c0:c0 + 128]

    pooled_cols = []
    for gi, win in enumerate(POOL_WINDOWS):
        c0 = gi * POOL_GROUP
        base = POOL_HALO + r0
        x0 = ep_ref[base:base + rows, c0:c0 + POOL_GROUP]
        s = x0
        for k in range(1, win):
            s = s + ep_ref[base - k:base - k + rows, c0:c0 + POOL_GROUP]
        if pos0 is None:
            pooled = s * (1.0 / win) - x0
        else:
            pos = pos0 + r0 + lax.broadcasted_iota(jnp.int32, (rows, POOL_GROUP), 0)
            cnt = jnp.minimum(pos + 1, win).astype(_F32)
            pooled = s / cnt - x0
        pooled_cols.append(pooled)
    pooled = jnp.concatenate(pooled_cols, axis=-1)

    conv_cols = []
    for cg in range(CONV_WIDTH // 128):
        c0 = cg * 128
        base = CONV_HALO - CONV_BUF + r0
        acc = conv_rows(base, c0) * dw_ref[0:1, c0:c0 + 128]
        for k in range(1, CONV_KERNEL):
            acc = acc + conv_rows(base + k, c0) * dw_ref[k:k + 1, c0:c0 + 128]
        conv_cols.append(acc + dwb_ref[:, c0:c0 + 128])
    y = jnp.concatenate(conv_cols, axis=-1)
    mu = jnp.mean(y, axis=-1, keepdims=True)
    d = y - mu
    var = jnp.mean(d * d, axis=-1, keepdims=True)
    yn = d * lax.rsqrt(var + LN_EPS) * lng_ref[...] + lnb_ref[...]
    return pooled, yn * _sigmoid(yn)


def _front_kernel(*refs, mode, ramp, tail, tile, n_t, skew):
    it = iter(refs)
    h_ref, ppast_ref, cpast_ref = next(it), next(it), next(it)
    g1_ref, win_ref, wpa_ref, wpb_ref, psc_ref = next(it), next(it), next(it), next(it), next(it)
    dw_ref, dwb_ref, lng_ref, lnb_ref, pw_ref, wout_ref, g2_ref = (next(it) for _ in range(7))
    if tail == "ffn":
        wg_ref, wu_ref, wd_ref = next(it), next(it), next(it)
    elif tail == "route":
        rw_ref = next(it)
    hout_ref = next(it)
    if mode == "seq":
        pst_ref, cst_ref = next(it), next(it)
    else:
        up_ref, v_ref = next(it), next(it)
    if tail == "route":
        hn2_ref, route_ref, cnt_ref = next(it), next(it), next(it)
    ep_ref, ec_ref, mixed_ref = next(it), next(it), next(it)
    if mode == "seq":
        sh_ref = next(it)
    if skew:
        h1_ref = next(it)

    def ffn_pieces(h_in):
        hb = _rmsnorm(h_in, g2_ref[...]).astype(_BF16)
        d_ff = wg_ref.shape[1]
        hout_ref[...] = h_in

        def piece(c0):
            cs = min(FFN_CHUNK, d_ff - c0)
            g = _dot(hb, wg_ref[:, c0:c0 + cs])
            a = (g * _sigmoid(g) * _dot(hb, wu_ref[:, c0:c0 + cs])).astype(_BF16)
            hout_ref[...] += _dot(a, wd_ref[c0:c0 + cs, :])

        return [functools.partial(piece, c0) for c0 in range(0, d_ff, FFN_CHUNK)]

    def ffn_tail(h_in):
        for piece in ffn_pieces(h_in):
            piece()

    if mode == "seq":
        if skew:
            step = pl.program_id(0)
            t = lax.rem(step, n_t)

            @pl.when(step == 0)
            def _():
                h1_ref[...] = jnp.zeros(h1_ref.shape, _F32)
        else:
            t = pl.program_id(1)

        @pl.when(t == 0)
        def _():
            ep_ref[0:POOL_HALO, :] = ppast_ref[0]
            ec_ref[0:CONV_HALO, :] = cpast_ref[0]

    pending = ffn_pieces(h1_ref[...]) if skew else []

    x = h_ref[...]
    hn = _rmsnorm(x, g1_ref[...]).astype(_BF16)
    u = _dot(hn, win_ref[...])
    up = u[:, :POOL_WIDTH]
    v = u[:, POOL_WIDTH:POOL_WIDTH + CONV_WIDTH] * _sigmoid(u[:, POOL_WIDTH + CONV_WIDTH:])
    if mode == "seq":
        ep_ref[POOL_HALO:, :] = up
        ec_ref[CONV_HALO:, :] = v
    else:
        for g in range(POOL_WIDTH // 128):
            ep_ref[g] = up[:, g * 128:(g + 1) * 128]
            ec_ref[g] = v[:, g * 128:(g + 1) * 128]

    if mode == "seq":
        for r in range(1, 8):
            sh_ref[r - 1] = ec_ref[r:r + tile + CONV_SHIFT_PAD, :]
        chunk = min(MIX_CHUNK, tile)
        for r0 in range(0, tile, chunk):
            pos0 = t * tile if ramp else None
            pooled, act = _mix_chunk(ep_ref, ec_ref, r0, chunk, dw_ref, dwb_ref, lng_ref, lnb_ref, pos0, sh_ref)
            mixed_ref[r0:r0 + chunk, 0:POOL_WIDTH] = pooled
            mixed_ref[r0:r0 + chunk, POOL_WIDTH:] = act
            if pending:
                pending.pop(0)()
        while pending:
            pending.pop(0)()
        ptail = ep_ref[tile:tile + POOL_HALO, :]
        ctail = ec_ref[tile:tile + CONV_HALO, :]
        ep_ref[0:POOL_HALO, :] = ptail
        ec_ref[0:CONV_HALO, :] = ctail
        pst_ref[0] = ptail
        cst_ref[0] = ctail
    else:
        spt = tile // DEC_SEQ
        for t in range(DEC_SEQ):
            for g in range(POOL_WIDTH // 128):
                up_ref[t, :, g * 128:(g + 1) * 128] = ep_ref[g, pl.ds(t, spt, stride=DEC_SEQ), :]
                v_ref[t, :, g * 128:(g + 1) * 128] = ec_ref[g, pl.ds(t, spt, stride=DEC_SEQ), :]

        def pool_row(i, cols):
            return ppast_ref[i, :, cols] if i < POOL_BUF else up_ref[i - POOL_BUF, :, cols]

        def conv_row(i):
            return cpast_ref[i] if i < CONV_BUF else v_ref[i - CONV_BUF]

        for t in range(DEC_SEQ):
            pooled_cols = []
            for gi, win in enumerate(POOL_WINDOWS):
                cols = slice(gi * POOL_GROUP, (gi + 1) * POOL_GROUP)
                x0 = pool_row(POOL_BUF + t, cols)
                s = x0
                for k in range(1, win):
                    s = s + pool_row(POOL_BUF + t - k, cols)
                pooled_cols.append(s * (1.0 / win) - x0)
            y = conv_row(t) * dw_ref[0:1, :]
            for k in range(1, CONV_KERNEL):
                y = y + conv_row(t + k) * dw_ref[k:k + 1, :]
            y = y + dwb_ref[...]
            mu = jnp.mean(y, axis=-1, keepdims=True)
            d = y - mu
            var = jnp.mean(d * d, axis=-1, keepdims=True)
            yn = d * lax.rsqrt(var + LN_EPS) * lng_ref[...] + lnb_ref[...]
            act = yn * _sigmoid(yn)
            for g in range(POOL_WIDTH // 128):
                mixed_ref[g, pl.ds(t, spt, stride=DEC_SEQ), :] = pooled_cols[g]
                mixed_ref[POOL_WIDTH // 128 + g, pl.ds(t, spt, stride=DEC_SEQ), :] = act[:, g * 128:(g + 1) * 128]

    def mixed_cols(c0, c1):
        if mode == "seq":
            return mixed_ref[:, c0:c1].astype(_BF16)
        return jnp.concatenate([mixed_ref[g] for g in range(c0 // 128, c1 // 128)], axis=-1).astype(_BF16)

    pa = _dot(mixed_cols(0, 256), wpa_ref[...])
    pb = _dot(mixed_cols(256, 512), wpb_ref[...])
    pool_out = jnp.concatenate([pa, pb], axis=-1) * psc_ref[...]
    conv_out = _dot(mixed_cols(POOL_WIDTH, POOL_WIDTH + CONV_WIDTH), pw_ref[...])
    h1 = (x + _dot(pool_out.astype(_BF16), wout_ref[0:POOL_WIDTH, :])
          + _dot(conv_out.astype(_BF16), wout_ref[POOL_WIDTH:, :]))

    if tail == "none":
        hout_ref[...] = h1
    elif skew:
        h1_ref[...] = h1
    elif tail == "ffn":
        ffn_tail(h1)
    else:
        hn2 = _rmsnorm(h1, g2_ref[...])
        hout_ref[...] = h1
        hn2_ref[...] = hn2.astype(_BF16)
        logits = _dot(hn2.astype(_BF16), rw_ref[...])
        lane = lax.broadcasted_iota(jnp.int32, logits.shape, 1).astype(_F32)
        neg = jnp.float32(-jnp.inf)
        lg = jnp.where(lane < N_EXPERTS, logits, neg)
        m1 = jnp.max(lg, axis=-1, keepdims=True)
        i1 = jnp.min(jnp.where(lg == m1, lane, float(ROUTE_LANES)), axis=-1, keepdims=True)
        lg2 = jnp.where(lane == i1, neg, lg)
        m2 = jnp.max(lg2, axis=-1, keepdims=True)
        i2 = jnp.min(jnp.where(lg2 == m2, lane, float(ROUTE_LANES)), axis=-1, keepdims=True)
        e21 = jnp.exp(m2 - m1)
        gate1 = 1.0 / (1.0 + e21)
        gate2 = e21 * gate1
        route_ref[...] = jnp.where(lane == 0, i1, jnp.where(lane == 1, i2, jnp.where(
            lane == 2, gate1, jnp.where(lane == 3, gate2, 0.0))))
        chosen = (lane == i1).astype(_F32) + (lane == i2).astype(_F32)
        cnt_ref[...] = jnp.broadcast_to(jnp.sum(chosen, axis=0, keepdims=True), cnt_ref.shape)


def _const_spec(shape):
    nd = len(shape)
    return pl.BlockSpec(shape, lambda *_: (0,) * nd, pipeline_mode=pl.Buffered(1))


def _front_call(h, ppast, cpast, lw, *, mode, ramp, tail, tile, n_seq, n_t, skew=False, layer=None):
    rows = h.shape[0]
    weights = [lw["g1"], lw["w_in"], lw["wpa"], lw["wpb"], lw["pscale"], lw["dw"], lw["dwb"], lw["lng"],
               lw["lnb"], lw["pw"], lw["w_out"], lw["g2"]]
    if tail == "ffn":
        weights += [lw["wg"], lw["wu"], lw["wd"]]
    elif tail == "route":
        weights += [lw["rw"]]
    w_specs = [_const_spec(w.shape) for w in weights]

    if mode == "seq" and skew:
        assert tail == "ffn" and ppast.shape[0] == 1
        n_tiles = n_seq * n_t
        grid = (n_tiles + 1,)
        row_map = lambda s: (jnp.minimum(s, n_tiles - 1), 0)
        seq_map = lambda s: (jnp.minimum(s, n_tiles - 1) // n_t, 0, 0)
        past_specs = [pl.BlockSpec((1, POOL_HALO, POOL_WIDTH), lambda s: (0, 0, 0)),
                      pl.BlockSpec((1, CONV_HALO, CONV_WIDTH), lambda s: (0, 0, 0))]
        out_shape = [jax.ShapeDtypeStruct((rows, D_MODEL), _F32),
                     jax.ShapeDtypeStruct((n_seq, POOL_HALO, POOL_WIDTH), _F32),
                     jax.ShapeDtypeStruct((n_seq, CONV_HALO, CONV_WIDTH), _F32)]
        out_specs = [pl.BlockSpec((tile, D_MODEL), lambda s: (jnp.maximum(s - 1, 0), 0)),
                     pl.BlockSpec((1, POOL_HALO, POOL_WIDTH), seq_map),
                     pl.BlockSpec((1, CONV_HALO, CONV_WIDTH), seq_map)]
        semantics = ("arbitrary",)
    elif mode == "seq":
        grid = (n_seq, n_t)
        row_map = lambda b, t: (b * n_t + t, 0)
        shared = ppast.shape[0] == 1
        past_map = (lambda b, t: (0, 0, 0)) if shared else (lambda b, t: (b, 0, 0))
        past_specs = [pl.BlockSpec((1, POOL_HALO, POOL_WIDTH), past_map),
                      pl.BlockSpec((1, CONV_HALO, CONV_WIDTH), past_map)]
        out_shape = [jax.ShapeDtypeStruct((rows, D_MODEL), _F32),
                     jax.ShapeDtypeStruct((n_seq, POOL_HALO, POOL_WIDTH), _F32),
                     jax.ShapeDtypeStruct((n_seq, CONV_HALO, CONV_WIDTH), _F32)]
        out_specs = [pl.BlockSpec((tile, D_MODEL), row_map),
                     pl.BlockSpec((1, POOL_HALO, POOL_WIDTH), lambda b, t: (b, 0, 0)),
                     pl.BlockSpec((1, CONV_HALO, CONV_WIDTH), lambda b, t: (b, 0, 0))]
        semantics = ("arbitrary", "arbitrary")
    else:
        grid = (rows // tile,)
        row_map = lambda i: (i, 0)
        spt = tile // DEC_SEQ
        past_specs = [pl.BlockSpec((None, POOL_BUF, spt, POOL_WIDTH), lambda i: (layer, 0, i, 0)),
                      pl.BlockSpec((None, CONV_BUF, spt, CONV_WIDTH), lambda i: (layer, 0, i, 0))]
        out_shape = [jax.ShapeDtypeStruct((rows, D_MODEL), _F32),
                     jax.ShapeDtypeStruct((DEC_SEQ, rows // DEC_SEQ, POOL_WIDTH), _F32),
                     jax.ShapeDtypeStruct((DEC_SEQ, rows // DEC_SEQ, CONV_WIDTH), _F32)]
        out_specs = [pl.BlockSpec((tile, D_MODEL), row_map),
                     pl.BlockSpec((DEC_SEQ, spt, POOL_WIDTH), lambda i: (0, i, 0)),
                     pl.BlockSpec((DEC_SEQ, spt, CONV_WIDTH), lambda i: (0, i, 0))]
        semantics = ("arbitrary",)
    if tail == "route":
        out_shape += [jax.ShapeDtypeStruct((rows, D_MODEL), _BF16),
                      jax.ShapeDtypeStruct((rows, ROUTE_LANES), _F32),
                      jax.ShapeDtypeStruct((rows // tile * 8, ROUTE_LANES), _F32)]
        out_specs += [pl.BlockSpec((tile, D_MODEL), row_map), pl.BlockSpec((tile, ROUTE_LANES), row_map),
                      pl.BlockSpec((8, ROUTE_LANES), row_map)]

    if mode == "seq":
        scratch = [pltpu.VMEM((POOL_HALO + tile, POOL_WIDTH), _F32),
                   pltpu.VMEM((CONV_HALO + tile, CONV_WIDTH), _F32),
                   pltpu.VMEM((tile, POOL_WIDTH + CONV_WIDTH), _F32),
                   pltpu.VMEM((7, tile + CONV_SHIFT_PAD, CONV_WIDTH), _F32)]
    else:
        scratch = [pltpu.VMEM((POOL_WIDTH // 128, tile, 128), _F32),
                   pltpu.VMEM((CONV_WIDTH // 128, tile, 128), _F32),
                   pltpu.VMEM(((POOL_WIDTH + CONV_WIDTH) // 128, tile, 128), _F32)]
    if skew:
        scratch += [pltpu.VMEM((tile, D_MODEL), _F32)]

    body = functools.partial(_front_kernel, mode=mode, ramp=ramp, tail=tail, tile=tile, n_t=n_t, skew=skew)
    return pl.pallas_call(
        body,
        grid=grid,
        in_specs=[pl.BlockSpec((tile, D_MODEL), row_map)] + past_specs + w_specs,
        out_specs=out_specs,
        out_shape=out_shape,
        scratch_shapes=scratch,
        compiler_params=pltpu.CompilerParams(dimension_semantics=semantics, vmem_limit_bytes=VMEM_LIMIT),
        name=f"front_{mode}_{tail}" + ("_skew" if skew else ""),
    )(h, ppast, cpast, *weights)


def _segment_copies(c8_ref, lo_ref, segoff_ref, tidx, make_copy, act):
    for e in range(N_EXPERTS):
        k = tidx * N_EXPERTS + e
        lo_e = lo_ref[k]
        so_e = segoff_ref[k]

        def body(c, carry):
            act(make_copy(pl.multiple_of(lo_e + c * SEG_ALIGN, SEG_ALIGN),
                          pl.multiple_of(so_e + c * SEG_ALIGN, SEG_ALIGN)))
            return carry

        lax.fori_loop(0, c8_ref[k] // SEG_ALIGN, body, 0)


def _dispatch_kernel(segoff_ref, c8_ref, lo_ref, filloff_ref, filln_ref, hn_a_ref, route_a_ref, hn_b_ref,
                     route_b_ref, xs_hbm, lpos_ref, xl_ref, sem, zero_ref, fill_sem, *, tile, n_a):
    t = pl.program_id(0)
    n = pl.num_programs(0)
    slot = t % 2
    tt = t
    in_a = t < n_a
    hn = jnp.where(in_a, hn_a_ref[...], hn_b_ref[...])
    route = jnp.where(in_a, route_a_ref[...], route_b_ref[...])

    @pl.when(t == 0)
    def _():
        zero_ref[...] = jnp.zeros(zero_ref.shape, _F32)

        def fill(act):
            for r in range(N_EXPERTS + 1):
                off = filloff_ref[r]

                def body(c, carry):
                    dst = xs_hbm.at[pl.ds(pl.multiple_of(off + c * SEG_ALIGN, SEG_ALIGN), SEG_ALIGN)]
                    act(pltpu.make_async_copy(zero_ref, dst, fill_sem))
                    return carry

                lax.fori_loop(0, filln_ref[r], body, 0)

        fill(lambda c: c.start())
        fill(lambda c: c.wait())

    lane = lax.broadcasted_iota(jnp.int32, (tile, ROUTE_LANES), 1).astype(_F32)
    oh0 = (lane == route[:, 0:1]).astype(_F32)
    oh1 = (lane == route[:, 1:2]).astype(_F32)
    earlier = (lax.broadcasted_iota(jnp.int32, (tile, tile), 1)
               < lax.broadcasted_iota(jnp.int32, (tile, tile), 0)).astype(_BF16)
    rank = _dot(earlier, (oh0 + oh1).astype(_BF16))
    lane1 = lax.broadcasted_iota(jnp.int32, (1, ROUTE_LANES), 1)
    lo_vec = jnp.zeros((1, ROUTE_LANES), _F32)
    for e in range(N_EXPERTS):
        lo_vec = jnp.where(lane1 == e, lo_ref[tt * N_EXPERTS + e].astype(_F32), lo_vec)
    posv = rank + lo_vec
    p0 = jnp.sum(oh0 * posv, axis=-1, keepdims=True)
    p1 = jnp.sum(oh1 * posv, axis=-1, keepdims=True)
    slab = jnp.where(lane == 0, p0, jnp.where(lane == 1, p1, 0.0))
    lpos_ref[...] = slab
    slab_t = slab.T
    rowi = lax.broadcasted_iota(jnp.int32, (LOCAL_ROWS, tile), 0).astype(_F32)
    place = jnp.where(jnp.logical_or(rowi == slab_t[0:1, :], rowi == slab_t[1:2, :]), 1.0, 0.0).astype(_BF16)
    xl_ref[slot] = _dot(place, hn.astype(_BF16))

    def copies(tidx, s):
        def make_copy(lo, so):
            return pltpu.make_async_copy(xl_ref.at[s, pl.ds(lo, SEG_ALIGN)], xs_hbm.at[pl.ds(so, SEG_ALIGN)],
                                         sem.at[s])
        return functools.partial(_segment_copies, c8_ref, lo_ref, segoff_ref, tidx, make_copy)

    copies(tt, slot)(lambda c: c.start())

    @pl.when(t > 0)
    def _():
        copies(tt - 1, 1 - slot)(lambda c: c.wait())

    @pl.when(t == n - 1)
    def _():
        copies(tt, slot)(lambda c: c.wait())


def _dispatch_call(plan, hn_a, route_a, hn_b, route_b, *, tile):
    n_a = hn_a.shape[0] // tile
    n_b = hn_b.shape[0] // tile
    amap = lambda i, *_: (jnp.minimum(i, n_a - 1), 0)
    bmap = lambda i, *_: (jnp.maximum(i - n_a, 0), 0)
    tables = (plan["seg_off"], plan["c8"], plan["lo"], plan["fill_off"], plan["fill_n"])
    return pl.pallas_call(
        functools.partial(_dispatch_kernel, tile=tile, n_a=n_a),
        grid_spec=pltpu.PrefetchScalarGridSpec(
            num_scalar_prefetch=len(tables),
            grid=(n_a + n_b,),
            in_specs=[pl.BlockSpec((tile, D_MODEL), amap), pl.BlockSpec((tile, ROUTE_LANES), amap),
                      pl.BlockSpec((tile, D_MODEL), bmap), pl.BlockSpec((tile, ROUTE_LANES), bmap)],
            out_specs=[pl.BlockSpec(memory_space=pl.ANY), pl.BlockSpec((tile, ROUTE_LANES), lambda i, *_: (i, 0))],
            scratch_shapes=[pltpu.VMEM((2, LOCAL_ROWS, D_MODEL), _F32), pltpu.SemaphoreType.DMA((2,)),
                            pltpu.VMEM((SEG_ALIGN, D_MODEL), _F32), pltpu.SemaphoreType.DMA(())],
        ),
        out_shape=[jax.ShapeDtypeStruct((plan["n_tiles"] * MOE_ROW_TILE, D_MODEL), _F32),
                   jax.ShapeDtypeStruct(((n_a + n_b) * tile, ROUTE_LANES), _F32)],
        compiler_params=pltpu.CompilerParams(
            dimension_semantics=("arbitrary",), vmem_limit_bytes=VMEM_LIMIT, has_side_effects=True),
        name="moe_dispatch",
    )(*tables, hn_a, route_a, hn_b, route_b)


def _moe_kernel(te_ref, nv_ref, nu_ref, x_ref, wg_ref, wu_ref, wd_ref, o_ref, xb_ref, wgb_ref, wub_ref, wdb_ref):
    del te_ref, nu_ref
    i = pl.program_id(0)
    j = pl.program_id(1)
    nv = nv_ref[i]

    @pl.when(j == 0)
    def _():
        o_ref[...] = jnp.zeros(o_ref.shape, _F32)
        xb_ref[...] = x_ref[...].astype(_BF16)

    def expert_ffn(xs, wg, wu, wd):
        g = _dot(xs, wg)
        a = (g * _sigmoid(g) * _dot(xs, wu)).astype(_BF16)
        return _dot(a, wd)

    @pl.when(nv == MOE_ROW_TILE)
    def _():
        o_ref[...] += expert_ffn(xb_ref[...], wg_ref[0].astype(_BF16), wu_ref[0].astype(_BF16),
                                 wd_ref[0].astype(_BF16))

    @pl.when(jnp.logical_and(nv > 0, nv < MOE_ROW_TILE))
    def _():
        wgb_ref[...] = wg_ref[0].astype(_BF16)
        wub_ref[...] = wu_ref[0].astype(_BF16)
        wdb_ref[...] = wd_ref[0].astype(_BF16)
        for s in range(MOE_ROW_TILE // MOE_SUB):
            rows = slice(s * MOE_SUB, (s + 1) * MOE_SUB)

            @pl.when(nv > s * MOE_SUB)
            def _():
                o_ref[rows, :] += expert_ffn(xb_ref[rows, :], wgb_ref[...], wub_ref[...], wdb_ref[...])


def _moe_call(tile_expert, tile_valid, n_used, xs, wg, wu, wd):
    n_tiles = xs.shape[0] // MOE_ROW_TILE
    eff = wg.shape[2]
    n_j = eff // MOE_FF_TILE

    def row_map(i, j, te, nv, nu):
        return (jnp.maximum(jnp.minimum(i, nu[0] - 1), 0), 0)

    def col_of(i, j, nu):
        return jnp.where(i < nu[0], j, n_j - 1)

    return pl.pallas_call(
        _moe_kernel,
        grid_spec=pltpu.PrefetchScalarGridSpec(
            num_scalar_prefetch=3,
            grid=(n_tiles, n_j),
            in_specs=[
                pl.BlockSpec((MOE_ROW_TILE, D_MODEL), row_map),
                pl.BlockSpec((1, D_MODEL, MOE_FF_TILE), lambda i, j, te, nv, nu: (te[i], 0, col_of(i, j, nu))),
                pl.BlockSpec((1, D_MODEL, MOE_FF_TILE), lambda i, j, te, nv, nu: (te[i], 0, col_of(i, j, nu))),
                pl.BlockSpec((1, MOE_FF_TILE, D_MODEL), lambda i, j, te, nv, nu: (te[i], col_of(i, j, nu), 0)),
            ],
            out_specs=pl.BlockSpec((MOE_ROW_TILE, D_MODEL), lambda i, j, te, nv, nu: (i, 0)),
            scratch_shapes=[
                pltpu.VMEM((MOE_ROW_TILE, D_MODEL), _BF16),
                pltpu.VMEM((D_MODEL, MOE_FF_TILE), _BF16),
                pltpu.VMEM((D_MODEL, MOE_FF_TILE), _BF16),
                pltpu.VMEM((MOE_FF_TILE, D_MODEL), _BF16),
            ],
        ),
        out_shape=jax.ShapeDtypeStruct(xs.shape, _F32),
        compiler_params=pltpu.CompilerParams(
            dimension_semantics=("arbitrary", "arbitrary"), vmem_limit_bytes=VMEM_LIMIT),
        name="moe_grouped",
    )(tile_expert, tile_valid, n_used, xs, wg, wu, wd)


def _combine_kernel(segoff_ref, c8_ref, lo_ref, h_ref, route_ref, lpos_ref, fg_ref, ys_hbm, o_ref, yl_ref, sem,
                    *, tile, tile0):
    t = pl.program_id(0)
    n = pl.num_programs(0)
    slot = t % 2
    tt = tile0 + t

    def copies(tidx, s):
        def make_copy(lo, so):
            return pltpu.make_async_copy(ys_hbm.at[pl.ds(so, SEG_ALIGN)], yl_ref.at[s, pl.ds(lo, SEG_ALIGN)],
                                         sem.at[s])
        return functools.partial(_segment_copies, c8_ref, lo_ref, segoff_ref, tidx, make_copy)

    @pl.when(t == 0)
    def _():
        yl_ref[...] = jnp.zeros(yl_ref.shape, _F32)
        copies(tt, slot)(lambda c: c.start())

    @pl.when(t + 1 < n)
    def _():
        copies(tt + 1, 1 - slot)(lambda c: c.start())

    copies(tt, slot)(lambda c: c.wait())
    last = tt * N_EXPERTS + N_EXPERTS - 1
    used = lo_ref[last] + c8_ref[last]
    rowi = lax.broadcasted_iota(jnp.int32, (LOCAL_ROWS, 1), 0)
    ylb = jnp.where(rowi < used, yl_ref[slot], 0.0).astype(_BF16)
    lanel = lax.broadcasted_iota(jnp.int32, (tile, LOCAL_ROWS), 1).astype(_F32)
    take = (jnp.where(lanel == lpos_ref[:, 0:1], route_ref[:, 2:3], 0.0)
            + jnp.where(lanel == lpos_ref[:, 1:2], route_ref[:, 3:4], 0.0)).astype(_BF16)
    o_ref[...] = _rmsnorm(h_ref[...] + _dot(take, ylb), fg_ref[...])


def _combine_call(plan, h, route, lpos, fg, ys, *, tile, tile0):
    rows = h.shape[0]
    smap = lambda i, *_: (i, 0)
    return pl.pallas_call(
        functools.partial(_combine_kernel, tile=tile, tile0=tile0),
        grid_spec=pltpu.PrefetchScalarGridSpec(
            num_scalar_prefetch=3,
            grid=(rows // tile,),
            in_specs=[
                pl.BlockSpec((tile, D_MODEL), smap),
                pl.BlockSpec((tile, ROUTE_LANES), smap),
                pl.BlockSpec((tile, ROUTE_LANES), lambda i, *_: (i + tile0, 0)),
                pl.BlockSpec((1, D_MODEL), lambda i, *_: (0, 0)),
                pl.BlockSpec(memory_space=pl.ANY),
            ],
            out_specs=pl.BlockSpec((tile, D_MODEL), smap),
            scratch_shapes=[pltpu.VMEM((2, LOCAL_ROWS, D_MODEL), _F32), pltpu.SemaphoreType.DMA((2,))],
        ),
        out_shape=jax.ShapeDtypeStruct((rows, D_MODEL), _F32),
        compiler_params=pltpu.CompilerParams(
            dimension_semantics=("arbitrary",), vmem_limit_bytes=VMEM_LIMIT),
        name="moe_combine",
    )(plan["seg_off"], plan["c8"], plan["lo"], h, route, lpos, fg, ys)


def _layer_weights(l, norm1_g, w_in, pool_w, pool_scale, conv_dw_w, conv_dw_b, conv_ln_g, conv_ln_b,
                   conv_pw_w, w_out, norm2_g):
    zero = jnp.zeros((POOL_GROUP, POOL_GROUP), _F32)
    pwl = pool_w[l]
    wpa = jnp.block([[pwl[0], zero], [zero, pwl[1]]])
    wpb = jnp.block([[pwl[2], zero], [zero, pwl[3]]])
    return {
        "g1": norm1_g[l][None, :], "w_in": w_in[l].astype(_BF16),
        "wpa": wpa.astype(_BF16), "wpb": wpb.astype(_BF16), "pscale": pool_scale[l][None, :],
        "dw": conv_dw_w[l], "dwb": conv_dw_b[l][None, :], "lng": conv_ln_g[l][None, :],
        "lnb": conv_ln_b[l][None, :], "pw": conv_pw_w[l].astype(_BF16), "w_out": w_out[l].astype(_BF16),
        "g2": norm2_g[l][None, :],
    }


def _routing_plan(cnt):
    n_tok_tiles = cnt.shape[0]
    n = n_tok_tiles * ROW_TILE
    c8 = ((cnt + SEG_ALIGN - 1) // SEG_ALIGN) * SEG_ALIGN
    lo = jnp.cumsum(c8, axis=1) - c8
    within = jnp.cumsum(c8, axis=0) - c8
    glen = jnp.sum(c8, axis=0)
    gpad = ((glen + MOE_ROW_TILE - 1) // MOE_ROW_TILE) * MOE_ROW_TILE
    ends = jnp.cumsum(gpad)
    goff = ends - gpad
    seg_off = goff[None, :] + within
    max_rows = 2 * n + (SEG_ALIGN - 1) * N_EXPERTS * n_tok_tiles
    n_tiles = -(-max_rows // MOE_ROW_TILE) + N_EXPERTS
    starts = jnp.arange(n_tiles, dtype=jnp.int32) * MOE_ROW_TILE
    te = jnp.sum((starts[:, None] >= ends[None, :]).astype(jnp.int32), axis=1)
    n_used = (ends[-1] // MOE_ROW_TILE).astype(jnp.int32)
    te_c = jnp.minimum(te, N_EXPERTS - 1)
    valid = jnp.clip(goff[te_c] + glen[te_c] - starts, 0, MOE_ROW_TILE)
    valid = jnp.where(te < N_EXPERTS, valid, 0).astype(jnp.int32)
    last_e = te_c[jnp.maximum(n_used - 1, 0)]
    te_c = jnp.where(te < N_EXPERTS, te_c, last_e).astype(jnp.int32)
    flat = lambda a: a.reshape(-1).astype(jnp.int32)
    total = jnp.full((1,), n_tiles * MOE_ROW_TILE, jnp.int32)
    fill_off = jnp.concatenate([goff + glen, ends[-1:]])
    fill_n = jnp.concatenate([gpad - glen, total - ends[-1:]]) // SEG_ALIGN
    return {"seg_off": flat(seg_off), "c8": flat(c8), "lo": flat(lo), "fill_off": flat(fill_off),
            "fill_n": flat(fill_n), "tile_expert": te_c, "tile_valid": valid, "n_used": n_used.reshape(1),
            "n_tiles": n_tiles}


def kernel(x_prompt, x_sample, state_pool, state_conv, meta_tokens, norm1_g, w_in, pool_w, pool_scale,
           conv_dw_w, conv_dw_b, conv_ln_g, conv_ln_b, conv_pw_w, w_out, norm2_g, ffn_w_gate, ffn_w_up,
           ffn_w_down, router_w, moe_w_gate, moe_w_up, moe_w_down, final_norm_g):
    batch, seq, _ = x_prompt.shape
    dec_batch, dec_seq, _ = x_sample.shape
    assert dec_seq == DEC_SEQ and seq % ROW_TILE == 0 and (dec_batch * dec_seq) % ROW_TILE == 0
    assert (dec_batch * dec_seq) % DEC_ROW_TILE == 0
    lws = [_layer_weights(l, norm1_g, w_in, pool_w, pool_scale, conv_dw_w, conv_dw_b, conv_ln_g, conv_ln_b,
                          conv_pw_w, w_out, norm2_g) for l in range(2)]
    lws[0].update(wg=ffn_w_gate[0].astype(_BF16), wu=ffn_w_up[0].astype(_BF16), wd=ffn_w_down[0].astype(_BF16))
    rw = jnp.zeros((D_MODEL, ROUTE_LANES), _F32).at[:, :N_EXPERTS].set(router_w[0])
    lws[1].update(rw=rw.astype(_BF16))

    n_p = batch * seq
    n_s = dec_batch * dec_seq
    n_t = seq // ROW_TILE
    hp = x_prompt.reshape(n_p, D_MODEL)
    hs = x_sample.reshape(n_s, D_MODEL)
    hm = meta_tokens.astype(_F32)
    zero_pp = jnp.zeros((1, POOL_HALO, POOL_WIDTH), _F32)
    zero_cp = jnp.zeros((1, CONV_HALO, CONV_WIDTH), _F32)
    pool_tm = jnp.transpose(state_pool, (0, 2, 1, 3))
    conv_tm = jnp.transpose(state_conv, (0, 2, 1, 3))

    seq_kw = dict(mode="seq", tile=ROW_TILE, n_seq=batch, n_t=n_t)
    meta_kw = dict(mode="seq", ramp=True, tile=N_META, n_seq=1, n_t=1)
    dec_kw = dict(mode="dec", ramp=False, tile=DEC_ROW_TILE, n_seq=0, n_t=0)

    hm, pst_m0, cst_m0 = _front_call(hm, zero_pp, zero_cp, lws[0], tail="ffn", **meta_kw)
    hp, pst_p0, cst_p0 = _front_call(hp, pst_m0, cst_m0, lws[0], ramp=False, tail="ffn", skew=True, **seq_kw)
    hs, up_s0, v_s0 = _front_call(hs, pool_tm, conv_tm, lws[0], tail="ffn", layer=0, **dec_kw)

    _, pst_m1, cst_m1 = _front_call(hm, zero_pp, zero_cp, lws[1], tail="none", **meta_kw)
    hp, pst_p1, cst_p1, hn_p, route_p, cnt_p = _front_call(hp, pst_m1, cst_m1, lws[1], ramp=False, tail="route", **seq_kw)
    hs, up_s1, v_s1, hn_s, route_s, cnt_s = _front_call(hs, pool_tm, conv_tm, lws[1], tail="route", layer=1, **dec_kw)

    per_tile = lambda c, tile: c[::8, :N_EXPERTS].reshape(-1, ROW_TILE // tile, N_EXPERTS).sum(axis=1)
    cnt = jnp.concatenate([per_tile(cnt_p, ROW_TILE), per_tile(cnt_s, DEC_ROW_TILE)], axis=0)
    plan = _routing_plan(cnt.astype(jnp.int32))
    s_tile0 = n_p // ROW_TILE

    xs, lpos = _dispatch_call(plan, hn_p, route_p, hn_s, route_s, tile=ROW_TILE)
    ys = _moe_call(plan["tile_expert"], plan["tile_valid"], plan["n_used"], xs,
                   moe_w_gate[0], moe_w_up[0], moe_w_down[0])

    fg = final_norm_g[None, :]
    y_p = _combine_call(plan, hp, route_p, lpos, fg, ys, tile=ROW_TILE, tile0=0)
    y_s = _combine_call(plan, hs, route_s, lpos, fg, ys, tile=ROW_TILE, tile0=s_tile0)

    new_pool_p = jnp.stack([pst_p0, pst_p1])[:, :, POOL_HALO - POOL_BUF:, :]
    new_conv_p = jnp.stack([cst_p0, cst_p1])[:, :, CONV_HALO - CONV_BUF:, :]
    new_pool_s = jnp.concatenate([pool_tm, jnp.stack([up_s0, up_s1])], axis=1)[:, -POOL_BUF:]
    new_conv_s = jnp.concatenate([conv_tm, jnp.stack([v_s0, v_s1])], axis=1)[:, -CONV_BUF:]
    new_pool_s = jnp.transpose(new_pool_s, (0, 2, 1, 3))
    new_conv_s = jnp.transpose(new_conv_s, (0, 2, 1, 3))
    return (y_p.reshape(batch, seq, D_MODEL), y_s.reshape(dec_batch, dec_seq, D_MODEL),
            new_pool_p, new_conv_p, new_pool_s, new_conv_s)
```

```python
import functools

import jax
import jax.numpy as jnp
from jax import lax
from jax.experimental import pallas as pl
from jax.experimental.pallas import tpu as pltpu

D_MODEL = 1024
N_META = 16
POOL_WIDTH = 512
CONV_WIDTH = 512
POOL_WINDOWS = (2, 4, 8, 16)
POOL_GROUP = 128
POOL_BUF = 15
CONV_KERNEL = 31
CONV_BUF = 30
IN_COLS = POOL_WIDTH + 2 * CONV_WIDTH
N_EXPERTS = 8
DEC_SEQ = 8
RMS_EPS = 1e-6
LN_EPS = 1e-5

POOL_HALO = 16
CONV_HALO = 32
CONV_SHIFT_PAD = CONV_HALO - 8
ROUTE_LANES = 128

ROW_TILE = 512
DEC_ROW_TILE = 256
MIX_CHUNK = 64
FFN_CHUNK = 512

MOE_ROW_TILE = 1024
MOE_SUB = 256
MOE_FF_TILE = 512
SEG_ALIGN = 8
SEG_CHUNK = 64
LOCAL_ROWS = -(-(2 * ROW_TILE + (SEG_ALIGN - 1) * N_EXPERTS) // 128) * 128

VMEM_LIMIT = 56 * 1024 * 1024

_BF16 = jnp.bfloat16
_F32 = jnp.float32


def _dot(a, b):
    return jnp.dot(a, b, preferred_element_type=_F32)


def _sigmoid(x):
    return 1.0 / (1.0 + jnp.exp(-x))


def _rmsnorm(x, g):
    return x * lax.rsqrt(jnp.mean(x * x, axis=-1, keepdims=True) + RMS_EPS) * g


def _mix_chunk(ep_ref, ec_ref, r0, rows, dw_ref, dwb_ref, lng_ref, lnb_ref, pos0, sh_ref=None):
    def conv_rows(start, c0):
        if sh_ref is None or start % 8 == 0:
            return ec_ref[start:start + rows, c0:c0 + 128]
        q, r = divmod(start, 8)
        return sh_ref[r - 1, 8 * q:8 * q + rows, c0:c0 + 128]

    pooled_cols = []
    for gi, win in enumerate(POOL_WINDOWS):
        c0 = gi * POOL_GROUP
        base = POOL_HALO + r0
        x0 = ep_ref[base:base + rows, c0:c0 + POOL_GROUP]
        s = x0
        for k in range(1, win):
            s = s + ep_ref[base - k:base - k + rows, c0:c0 + POOL_GROUP]
        if pos0 is None:
            pooled = s * (1.0 / win) - x0
        else:
            pos = pos0 + r0 + lax.broadcasted_iota(jnp.int32, (rows, POOL_GROUP), 0)
            cnt = jnp.minimum(pos + 1, win).astype(_F32)
            pooled = s / cnt - x0
        pooled_cols.append(pooled)
    pooled = jnp.concatenate(pooled_cols, axis=-1)

    conv_cols = []
    for cg in range(CONV_WIDTH // 128):
        c0 = cg * 128
        base = CONV_HALO - CONV_BUF + r0
        acc = conv_rows(base, c0) * dw_ref[0:1, c0:c0 + 128]
        for k in range(1, CONV_KERNEL):
            acc = acc + conv_rows(base + k, c0) * dw_ref[k:k + 1, c0:c0 + 128]
        conv_cols.append(acc + dwb_ref[:, c0:c0 + 128])
    y = jnp.concatenate(conv_cols, axis=-1)
    mu = jnp.mean(y, axis=-1, keepdims=True)
    d = y - mu
    var = jnp.mean(d * d, axis=-1, keepdims=True)
    yn = d * lax.rsqrt(var + LN_EPS) * lng_ref[...] + lnb_ref[...]
    return pooled, yn * _sigmoid(yn)


def _front_kernel(*refs, mode, ramp, tail, tile, n_t, skew):
    it = iter(refs)
    h_ref, ppast_ref, cpast_ref = next(it), next(it), next(it)
    g1_ref, win_ref, wpa_ref, wpb_ref, psc_ref = next(it), next(it), next(it), next(it), next(it)
    dw_ref, dwb_ref, lng_ref, lnb_ref, pw_ref, wout_ref, g2_ref = (next(it) for _ in range(7))
    if tail == "ffn":
        wg_ref, wu_ref, wd_ref = next(it), next(it), next(it)
    elif tail == "route":
        rw_ref = next(it)
    hout_ref = next(it)
    if mode == "seq":
        pst_ref, cst_ref = next(it), next(it)
    else:
        up_ref, v_ref = next(it), next(it)
    if tail == "route":
        hn2_ref, route_ref, cnt_ref = next(it), next(it), next(it)
    ep_ref, ec_ref, mixed_ref = next(it), next(it), next(it)
    if mode == "seq":
        sh_ref = next(it)
    if skew:
        h1_ref = next(it)

    def ffn_pieces(h_in):
        hb = _rmsnorm(h_in, g2_ref[...]).astype(_BF16)
        d_ff = wg_ref.shape[1]
        hout_ref[...] = h_in

        def piece(c0):
            cs = min(FFN_CHUNK, d_ff - c0)
            g = _dot(hb, wg_ref[:, c0:c0 + cs])
            a = (g * _sigmoid(g) * _dot(hb, wu_ref[:, c0:c0 + cs])).astype(_BF16)
            hout_ref[...] += _dot(a, wd_ref[c0:c0 + cs, :])

        return [functools.partial(piece, c0) for c0 in range(0, d_ff, FFN_CHUNK)]

    def ffn_tail(h_in):
        for piece in ffn_pieces(h_in):
            piece()

    if mode == "seq":
        if skew:
            step = pl.program_id(0)
            t = lax.rem(step, n_t)

            @pl.when(step == 0)
            def _():
                h1_ref[...] = jnp.zeros(h1_ref.shape, _F32)
        else:
            t = pl.program_id(1)

        @pl.when(t == 0)
        def _():
            ep_ref[0:POOL_HALO, :] = ppast_ref[0]
            ec_ref[0:CONV_HALO, :] = cpast_ref[0]

    pending = ffn_pieces(h1_ref[...]) if skew else []

    x = h_ref[...]
    hn = _rmsnorm(x, g1_ref[...]).astype(_BF16)
    u = _dot(hn, win_ref[...])
    up = u[:, :POOL_WIDTH]
    v = u[:, POOL_WIDTH:POOL_WIDTH + CONV_WIDTH] * _sigmoid(u[:, POOL_WIDTH + CONV_WIDTH:])
    if mode == "seq":
        ep_ref[POOL_HALO:, :] = up
        ec_ref[CONV_HALO:, :] = v
    else:
        for g in range(POOL_WIDTH // 128):
            ep_ref[g] = up[:, g * 128:(g + 1) * 128]
            ec_ref[g] = v[:, g * 128:(g + 1) * 128]

    if mode == "seq":
        for r in range(1, 8):
            sh_ref[r - 1] = ec_ref[r:r + tile + CONV_SHIFT_PAD, :]
        chunk = min(MIX_CHUNK, tile)
        for r0 in range(0, tile, chunk):
            pos0 = t * tile if ramp else None
            pooled, act = _mix_chunk(ep_ref, ec_ref, r0, chunk, dw_ref, dwb_ref, lng_ref, lnb_ref, pos0, sh_ref)
            mixed_ref[r0:r0 + chunk, 0:POOL_WIDTH] = pooled
            mixed_ref[r0:r0 + chunk, POOL_WIDTH:] = act
            if pending:
                pending.pop(0)()
        while pending:
            pending.pop(0)()
        ptail = ep_ref[tile:tile + POOL_HALO, :]
        ctail = ec_ref[tile:tile + CONV_HALO, :]
        ep_ref[0:POOL_HALO, :] = ptail
        ec_ref[0:CONV_HALO, :] = ctail
        pst_ref[0] = ptail
        cst_ref[0] = ctail
    else:
        spt = tile // DEC_SEQ
        for t in range(DEC_SEQ):
            for g in range(POOL_WIDTH // 128):
                up_ref[t, :, g * 128:(g + 1) * 128] = ep_ref[g, pl.ds(t, spt, stride=DEC_SEQ), :]
                v_ref[t, :, g * 128:(g + 1) * 128] = ec_ref[g, pl.ds(t, spt, stride=DEC_SEQ), :]

        def pool_row(i, cols):
            return ppast_ref[i, :, cols] if i < POOL_BUF else up_ref[i - POOL_BUF, :, cols]

        def conv_row(i):
            return cpast_ref[i] if i < CONV_BUF else v_ref[i - CONV_BUF]

        for t in range(DEC_SEQ):
            pooled_cols = []
            for gi, win in enumerate(POOL_WINDOWS):
                cols = slice(gi * POOL_GROUP, (gi + 1) * POOL_GROUP)
                x0 = pool_row(POOL_BUF + t, cols)
                s = x0
                for k in range(1, win):
                    s = s + pool_row(POOL_BUF + t - k, cols)
                pooled_cols.append(s * (1.0 / win) - x0)
            y = conv_row(t) * dw_ref[0:1, :]
            for k in range(1, CONV_KERNEL):
                y = y + conv_row(t + k) * dw_ref[k:k + 1, :]
            y = y + dwb_ref[...]
            mu = jnp.mean(y, axis=-1, keepdims=True)
            d = y - mu
            var = jnp.mean(d * d, axis=-1, keepdims=True)
            yn = d * lax.rsqrt(var + LN_EPS) * lng_ref[...] + lnb_ref[...]
            act = yn * _sigmoid(yn)
            for g in range(POOL_WIDTH // 128):
                mixed_ref[g, pl.ds(t, spt, stride=DEC_SEQ), :] = pooled_cols[g]
                mixed_ref[POOL_WIDTH // 128 + g, pl.ds(t, spt, stride=DEC_SEQ), :] = act[:, g * 128:(g + 1) * 128]

    def mixed_cols(c0, c1):
        if mode == "seq":
            return mixed_ref[:, c0:c1].astype(_BF16)
        return jnp.concatenate([mixed_ref[g] for g in range(c0 // 128, c1 // 128)], axis=-1).astype(_BF16)

    pa = _dot(mixed_cols(0, 256), wpa_ref[...])
    pb = _dot(mixed_cols(256, 512), wpb_ref[...])
    pool_out = jnp.concatenate([pa, pb], axis=-1) * psc_ref[...]
    conv_out = _dot(mixed_cols(POOL_WIDTH, POOL_WIDTH + CONV_WIDTH), pw_ref[...])
    h1 = (x + _dot(pool_out.astype(_BF16), wout_ref[0:POOL_WIDTH, :])
          + _dot(conv_out.astype(_BF16), wout_ref[POOL_WIDTH:, :]))

    if tail == "none":
        hout_ref[...] = h1
    elif skew:
        h1_ref[...] = h1
    elif tail == "ffn":
        ffn_tail(h1)
    else:
        hn2 = _rmsnorm(h1, g2_ref[...])
        hout_ref[...] = h1
        hn2_ref[...] = hn2.astype(_BF16)
        logits = _dot(hn2.astype(_BF16), rw_ref[...])
        lane = lax.broadcasted_iota(jnp.int32, logits.shape, 1).astype(_F32)
        neg = jnp.float32(-jnp.inf)
        lg = jnp.where(lane < N_EXPERTS, logits, neg)
        m1 = jnp.max(lg, axis=-1, keepdims=True)
        i1 = jnp.min(jnp.where(lg == m1, lane, float(ROUTE_LANES)), axis=-1, keepdims=True)
        lg2 = jnp.where(lane == i1, neg, lg)
        m2 = jnp.max(lg2, axis=-1, keepdims=True)
        i2 = jnp.min(jnp.where(lg2 == m2, lane, float(ROUTE_LANES)), axis=-1, keepdims=True)
        e21 = jnp.exp(m2 - m1)
        gate1 = 1.0 / (1.0 + e21)
        gate2 = e21 * gate1
        route_ref[...] = jnp.where(lane == 0, i1, jnp.where(lane == 1, i2, jnp.where(
            lane == 2, gate1, jnp.where(lane == 3, gate2, 0.0))))
        chosen = (lane == i1).astype(_F32) + (lane == i2).astype(_F32)
        cnt_ref[...] = jnp.broadcast_to(jnp.sum(chosen, axis=0, keepdims=True), cnt_ref.shape)


def _const_spec(shape):
    nd = len(shape)
    return pl.BlockSpec(shape, lambda *_: (0,) * nd, pipeline_mode=pl.Buffered(1))


def _front_call(h, ppast, cpast, lw, *, mode, ramp, tail, tile, n_seq, n_t, skew=False, layer=None):
    rows = h.shape[0]
    weights = [lw["g1"], lw["w_in"], lw["wpa"], lw["wpb"], lw["pscale"], lw["dw"], lw["dwb"], lw["lng"],
               lw["lnb"], lw["pw"], lw["w_out"], lw["g2"]]
    if tail == "ffn":
        weights += [lw["wg"], lw["wu"], lw["wd"]]
    elif tail == "route":
        weights += [lw["rw"]]
    w_specs = [_const_spec(w.shape) for w in weights]

    if mode == "seq" and skew:
        assert tail == "ffn" and ppast.shape[0] == 1
        n_tiles = n_seq * n_t
        grid = (n_tiles + 1,)
        row_map = lambda s: (jnp.minimum(s, n_tiles - 1), 0)
        seq_map = lambda s: (jnp.minimum(s, n_tiles - 1) // n_t, 0, 0)
        past_specs = [pl.BlockSpec((1, POOL_HALO, POOL_WIDTH), lambda s: (0, 0, 0)),
                      pl.BlockSpec((1, CONV_HALO, CONV_WIDTH), lambda s: (0, 0, 0))]
        out_shape = [jax.ShapeDtypeStruct((rows, D_MODEL), _F32),
                     jax.ShapeDtypeStruct((n_seq, POOL_HALO, POOL_WIDTH), _F32),
                     jax.ShapeDtypeStruct((n_seq, CONV_HALO, CONV_WIDTH), _F32)]
        out_specs = [pl.BlockSpec((tile, D_MODEL), lambda s: (jnp.maximum(s - 1, 0), 0)),
                     pl.BlockSpec((1, POOL_HALO, POOL_WIDTH), seq_map),
                     pl.BlockSpec((1, CONV_HALO, CONV_WIDTH), seq_map)]
        semantics = ("arbitrary",)
    elif mode == "seq":
        grid = (n_seq, n_t)
        row_map = lambda b, t: (b * n_t + t, 0)
        shared = ppast.shape[0] == 1
        past_map = (lambda b, t: (0, 0, 0)) if shared else (lambda b, t: (b, 0, 0))
        past_specs = [pl.BlockSpec((1, POOL_HALO, POOL_WIDTH), past_map),
                      pl.BlockSpec((1, CONV_HALO, CONV_WIDTH), past_map)]
        out_shape = [jax.ShapeDtypeStruct((rows, D_MODEL), _F32),
                     jax.ShapeDtypeStruct((n_seq, POOL_HALO, POOL_WIDTH), _F32),
                     jax.ShapeDtypeStruct((n_seq, CONV_HALO, CONV_WIDTH), _F32)]
        out_specs = [pl.BlockSpec((tile, D_MODEL), row_map),
                     pl.BlockSpec((1, POOL_HALO, POOL_WIDTH), lambda b, t: (b, 0, 0)),
                     pl.BlockSpec((1, CONV_HALO, CONV_WIDTH), lambda b, t: (b, 0, 0))]
        semantics = ("arbitrary", "arbitrary")
    else:
        grid = (rows // tile,)
        row_map = lambda i: (i, 0)
        spt = tile // DEC_SEQ
        past_specs = [pl.BlockSpec((None, POOL_BUF, spt, POOL_WIDTH), lambda i: (layer, 0, i, 0)),
                      pl.BlockSpec((None, CONV_BUF, spt, CONV_WIDTH), lambda i: (layer, 0, i, 0))]
        out_shape = [jax.ShapeDtypeStruct((rows, D_MODEL), _F32),
                     jax.ShapeDtypeStruct((DEC_SEQ, rows // DEC_SEQ, POOL_WIDTH), _F32),
                     jax.ShapeDtypeStruct((DEC_SEQ, rows // DEC_SEQ, CONV_WIDTH), _F32)]
        out_specs = [pl.BlockSpec((tile, D_MODEL), row_map),
                     pl.BlockSpec((DEC_SEQ, spt, POOL_WIDTH), lambda i: (0, i, 0)),
                     pl.BlockSpec((DEC_SEQ, spt, CONV_WIDTH), lambda i: (0, i, 0))]
        semantics = ("arbitrary",)
    if tail == "route":
        out_shape += [jax.ShapeDtypeStruct((rows, D_MODEL), _BF16),
                      jax.ShapeDtypeStruct((rows, ROUTE_LANES), _F32),
                      jax.ShapeDtypeStruct((rows // tile * 8, ROUTE_LANES), _F32)]
        out_specs += [pl.BlockSpec((tile, D_MODEL), row_map), pl.BlockSpec((tile, ROUTE_LANES), row_map),
                      pl.BlockSpec((8, ROUTE_LANES), row_map)]

    if mode == "seq":
        scratch = [pltpu.VMEM((POOL_HALO + tile, POOL_WIDTH), _F32),
                   pltpu.VMEM((CONV_HALO + tile, CONV_WIDTH), _F32),
                   pltpu.VMEM((tile, POOL_WIDTH + CONV_WIDTH), _F32),
                   pltpu.VMEM((7, tile + CONV_SHIFT_PAD, CONV_WIDTH), _F32)]
    else:
        scratch = [pltpu.VMEM((POOL_WIDTH // 128, tile, 128), _F32),
                   pltpu.VMEM((CONV_WIDTH // 128, tile, 128), _F32),
                   pltpu.VMEM(((POOL_WIDTH + CONV_WIDTH) // 128, tile, 128), _F32)]
    if skew:
        scratch += [pltpu.VMEM((tile, D_MODEL), _F32)]

    body = functools.partial(_front_kernel, mode=mode, ramp=ramp, tail=tail, tile=tile, n_t=n_t, skew=skew)
    return pl.pallas_call(
        body,
        grid=grid,
        in_specs=[pl.BlockSpec((tile, D_MODEL), row_map)] + past_specs + w_specs,
        out_specs=out_specs,
        out_shape=out_shape,
        scratch_shapes=scratch,
        compiler_params=pltpu.CompilerParams(dimension_semantics=semantics, vmem_limit_bytes=VMEM_LIMIT),
        name=f"front_{mode}_{tail}" + ("_skew" if skew else ""),
    )(h, ppast, cpast, *weights)


def _segment_copies(c8_ref, lo_ref, segoff_ref, tidx, make_copy, act):
    for e in range(N_EXPERTS):
        k = tidx * N_EXPERTS + e
        lo_e = lo_ref[k]
        so_e = segoff_ref[k]
        n_big = c8_ref[k] // SEG_CHUNK
        rest = n_big * SEG_CHUNK

        def big(c, carry):
            act(make_copy(pl.multiple_of(lo_e + c * SEG_CHUNK, SEG_ALIGN),
                          pl.multiple_of(so_e + c * SEG_CHUNK, SEG_ALIGN), SEG_CHUNK))
            return carry

        def small(c, carry):
            act(make_copy(pl.multiple_of(lo_e + rest + c * SEG_ALIGN, SEG_ALIGN),
                          pl.multiple_of(so_e + rest + c * SEG_ALIGN, SEG_ALIGN), SEG_ALIGN))
            return carry

        lax.fori_loop(0, n_big, big, 0)
        lax.fori_loop(0, (c8_ref[k] - rest) // SEG_ALIGN, small, 0)


def _dispatch_kernel(segoff_ref, c8_ref, lo_ref, filloff_ref, filln_ref, hn_a_ref, route_a_ref, hn_b_ref,
                     route_b_ref, xs_hbm, lpos_ref, xl_ref, sem, zero_ref, fill_sem, *, tile, n_a):
    t = pl.program_id(0)
    n = pl.num_programs(0)
    slot = t % 2
    tt = t
    in_a = t < n_a
    hn = jnp.where(in_a, hn_a_ref[...], hn_b_ref[...])
    route = jnp.where(in_a, route_a_ref[...], route_b_ref[...])

    @pl.when(t == 0)
    def _():
        zero_ref[...] = jnp.zeros(zero_ref.shape, _F32)

        def fill(act):
            for r in range(N_EXPERTS + 1):
                off = filloff_ref[r]
                n_big = filln_ref[r] // (SEG_CHUNK // SEG_ALIGN)
                rest = n_big * SEG_CHUNK

                def big(c, carry):
                    dst = xs_hbm.at[pl.ds(pl.multiple_of(off + c * SEG_CHUNK, SEG_ALIGN), SEG_CHUNK)]
                    act(pltpu.make_async_copy(zero_ref, dst, fill_sem))
                    return carry

                def small(c, carry):
                    dst = xs_hbm.at[pl.ds(pl.multiple_of(off + rest + c * SEG_ALIGN, SEG_ALIGN), SEG_ALIGN)]
                    act(pltpu.make_async_copy(zero_ref.at[pl.ds(0, SEG_ALIGN)], dst, fill_sem))
                    return carry

                lax.fori_loop(0, n_big, big, 0)
                lax.fori_loop(0, filln_ref[r] - n_big * (SEG_CHUNK // SEG_ALIGN), small, 0)

        fill(lambda c: c.start())
        fill(lambda c: c.wait())

    lane = lax.broadcasted_iota(jnp.int32, (tile, ROUTE_LANES), 1).astype(_F32)
    oh0 = (lane == route[:, 0:1]).astype(_F32)
    oh1 = (lane == route[:, 1:2]).astype(_F32)
    earlier = (lax.broadcasted_iota(jnp.int32, (tile, tile), 1)
               < lax.broadcasted_iota(jnp.int32, (tile, tile), 0)).astype(_BF16)
    rank = _dot(earlier, (oh0 + oh1).astype(_BF16))
    lane1 = lax.broadcasted_iota(jnp.int32, (1, ROUTE_LANES), 1)
    lo_vec = jnp.zeros((1, ROUTE_LANES), _F32)
    for e in range(N_EXPERTS):
        lo_vec = jnp.where(lane1 == e, lo_ref[tt * N_EXPERTS + e].astype(_F32), lo_vec)
    posv = rank + lo_vec
    p0 = jnp.sum(oh0 * posv, axis=-1, keepdims=True)
    p1 = jnp.sum(oh1 * posv, axis=-1, keepdims=True)
    slab = jnp.where(lane == 0, p0, jnp.where(lane == 1, p1, 0.0))
    lpos_ref[...] = slab
    slab_t = slab.T
    rowi = lax.broadcasted_iota(jnp.int32, (LOCAL_ROWS, tile), 0).astype(_F32)
    place = jnp.where(jnp.logical_or(rowi == slab_t[0:1, :], rowi == slab_t[1:2, :]), 1.0, 0.0).astype(_BF16)
    xl_ref[slot] = _dot(place, hn.astype(_BF16))

    def copies(tidx, s):
        def make_copy(lo, so, rows):
            return pltpu.make_async_copy(xl_ref.at[s, pl.ds(lo, rows)], xs_hbm.at[pl.ds(so, rows)], sem.at[s])
        return functools.partial(_segment_copies, c8_ref, lo_ref, segoff_ref, tidx, make_copy)

    copies(tt, slot)(lambda c: c.start())

    @pl.when(t > 0)
    def _():
        copies(tt - 1, 1 - slot)(lambda c: c.wait())

    @pl.when(t == n - 1)
    def _():
        copies(tt, slot)(lambda c: c.wait())


def _dispatch_call(plan, hn_a, route_a, hn_b, route_b, *, tile):
    n_a = hn_a.shape[0] // tile
    n_b = hn_b.shape[0] // tile
    amap = lambda i, *_: (jnp.minimum(i, n_a - 1), 0)
    bmap = lambda i, *_: (jnp.maximum(i - n_a, 0), 0)
    tables = (plan["seg_off"], plan["c8"], plan["lo"], plan["fill_off"], plan["fill_n"])
    return pl.pallas_call(
        functools.partial(_dispatch_kernel, tile=tile, n_a=n_a),
        grid_spec=pltpu.PrefetchScalarGridSpec(
            num_scalar_prefetch=len(tables),
            grid=(n_a + n_b,),
            in_specs=[pl.BlockSpec((tile, D_MODEL), amap), pl.BlockSpec((tile, ROUTE_LANES), amap),
                      pl.BlockSpec((tile, D_MODEL), bmap), pl.BlockSpec((tile, ROUTE_LANES), bmap)],
            out_specs=[pl.BlockSpec(memory_space=pl.ANY), pl.BlockSpec((tile, ROUTE_LANES), lambda i, *_: (i, 0))],
            scratch_shapes=[pltpu.VMEM((2, LOCAL_ROWS, D_MODEL), _F32), pltpu.SemaphoreType.DMA((2,)),
                            pltpu.VMEM((SEG_CHUNK, D_MODEL), _F32), pltpu.SemaphoreType.DMA(())],
        ),
        out_shape=[jax.ShapeDtypeStruct((plan["n_tiles"] * MOE_ROW_TILE, D_MODEL), _F32),
                   jax.ShapeDtypeStruct(((n_a + n_b) * tile, ROUTE_LANES), _F32)],
        compiler_params=pltpu.CompilerParams(
            dimension_semantics=("arbitrary",), vmem_limit_bytes=VMEM_LIMIT, has_side_effects=True),
        name="moe_dispatch",
    )(*tables, hn_a, route_a, hn_b, route_b)


def _moe_kernel(te_ref, nv_ref, nu_ref, x_ref, wg_ref, wu_ref, wd_ref, o_ref, xb_ref, wgb_ref, wub_ref, wdb_ref):
    del te_ref, nu_ref
    i = pl.program_id(0)
    j = pl.program_id(1)
    nv = nv_ref[i]

    @pl.when(j == 0)
    def _():
        o_ref[...] = jnp.zeros(o_ref.shape, _F32)
        xb_ref[...] = x_ref[...].astype(_BF16)

    def expert_ffn(xs, wg, wu, wd):
        g = _dot(xs, wg)
        a = (g * _sigmoid(g) * _dot(xs, wu)).astype(_BF16)
        return _dot(a, wd)

    @pl.when(nv == MOE_ROW_TILE)
    def _():
        o_ref[...] += expert_ffn(xb_ref[...], wg_ref[0].astype(_BF16), wu_ref[0].astype(_BF16),
                                 wd_ref[0].astype(_BF16))

    @pl.when(jnp.logical_and(nv > 0, nv < MOE_ROW_TILE))
    def _():
        wgb_ref[...] = wg_ref[0].astype(_BF16)
        wub_ref[...] = wu_ref[0].astype(_BF16)
        wdb_ref[...] = wd_ref[0].astype(_BF16)
        for s in range(MOE_ROW_TILE // MOE_SUB):
            rows = slice(s * MOE_SUB, (s + 1) * MOE_SUB)

            @pl.when(nv > s * MOE_SUB)
            def _():
                o_ref[rows, :] += expert_ffn(xb_ref[rows, :], wgb_ref[...], wub_ref[...], wdb_ref[...])


def _moe_call(tile_expert, tile_valid, n_used, xs, wg, wu, wd):
    n_tiles = xs.shape[0] // MOE_ROW_TILE
    eff = wg.shape[2]
    n_j = eff // MOE_FF_TILE

    def row_map(i, j, te, nv, nu):
        return (jnp.maximum(jnp.minimum(i, nu[0] - 1), 0), 0)

    def col_of(i, j, nu):
        return jnp.where(i < nu[0], j, n_j - 1)

    return pl.pallas_call(
        _moe_kernel,
        grid_spec=pltpu.PrefetchScalarGridSpec(
            num_scalar_prefetch=3,
            grid=(n_tiles, n_j),
            in_specs=[
                pl.BlockSpec((MOE_ROW_TILE, D_MODEL), row_map),
                pl.BlockSpec((1, D_MODEL, MOE_FF_TILE), lambda i, j, te, nv, nu: (te[i], 0, col_of(i, j, nu))),
                pl.BlockSpec((1, D_MODEL, MOE_FF_TILE), lambda i, j, te, nv, nu: (te[i], 0, col_of(i, j, nu))),
                pl.BlockSpec((1, MOE_FF_TILE, D_MODEL), lambda i, j, te, nv, nu: (te[i], col_of(i, j, nu), 0)),
            ],
            out_specs=pl.BlockSpec((MOE_ROW_TILE, D_MODEL), lambda i, j, te, nv, nu: (i, 0)),
            scratch_shapes=[
                pltpu.VMEM((MOE_ROW_TILE, D_MODEL), _BF16),
                pltpu.VMEM((D_MODEL, MOE_FF_TILE), _BF16),
                pltpu.VMEM((D_MODEL, MOE_FF_TILE), _BF16),
                pltpu.VMEM((MOE_FF_TILE, D_MODEL), _BF16),
            ],
        ),
        out_shape=jax.ShapeDtypeStruct(xs.shape, _F32),
        compiler_params=pltpu.CompilerParams(
            dimension_semantics=("arbitrary", "arbitrary"), vmem_limit_bytes=VMEM_LIMIT),
        name="moe_grouped",
    )(tile_expert, tile_valid, n_used, xs, wg, wu, wd)


def _combine_kernel(segoff_ref, c8_ref, lo_ref, h_ref, route_ref, lpos_ref, fg_ref, ys_hbm, o_ref, yl_ref, sem,
                    *, tile, tile0):
    t = pl.program_id(0)
    n = pl.num_programs(0)
    slot = t % 2
    tt = tile0 + t

    def copies(tidx, s):
        def make_copy(lo, so, rows):
            return pltpu.make_async_copy(ys_hbm.at[pl.ds(so, rows)], yl_ref.at[s, pl.ds(lo, rows)], sem.at[s])
        return functools.partial(_segment_copies, c8_ref, lo_ref, segoff_ref, tidx, make_copy)

    @pl.when(t == 0)
    def _():
        yl_ref[...] = jnp.zeros(yl_ref.shape, _F32)
        copies(tt, slot)(lambda c: c.start())

    @pl.when(t + 1 < n)
    def _():
        copies(tt + 1, 1 - slot)(lambda c: c.start())

    copies(tt, slot)(lambda c: c.wait())
    last = tt * N_EXPERTS + N_EXPERTS - 1
    used = lo_ref[last] + c8_ref[last]
    rowi = lax.broadcasted_iota(jnp.int32, (LOCAL_ROWS, 1), 0)
    ylb = jnp.where(rowi < used, yl_ref[slot], 0.0).astype(_BF16)
    lanel = lax.broadcasted_iota(jnp.int32, (tile, LOCAL_ROWS), 1).astype(_F32)
    take = (jnp.where(lanel == lpos_ref[:, 0:1], route_ref[:, 2:3], 0.0)
            + jnp.where(lanel == lpos_ref[:, 1:2], route_ref[:, 3:4], 0.0)).astype(_BF16)
    o_ref[...] = _rmsnorm(h_ref[...] + _dot(take, ylb), fg_ref[...])


def _combine_call(plan, h, route, lpos, fg, ys, *, tile, tile0):
    rows = h.shape[0]
    smap = lambda i, *_: (i, 0)
    return pl.pallas_call(
        functools.partial(_combine_kernel, tile=tile, tile0=tile0),
        grid_spec=pltpu.PrefetchScalarGridSpec(
            num_scalar_prefetch=3,
            grid=(rows // tile,),
            in_specs=[
                pl.BlockSpec((tile, D_MODEL), smap),
                pl.BlockSpec((tile, ROUTE_LANES), smap),
                pl.BlockSpec((tile, ROUTE_LANES), lambda i, *_: (i + tile0, 0)),
                pl.BlockSpec((1, D_MODEL), lambda i, *_: (0, 0)),
                pl.BlockSpec(memory_space=pl.ANY),
            ],
            out_specs=pl.BlockSpec((tile, D_MODEL), smap),
            scratch_shapes=[pltpu.VMEM((2, LOCAL_ROWS, D_MODEL), _F32), pltpu.SemaphoreType.DMA((2,))],
        ),
        out_shape=jax.ShapeDtypeStruct((rows, D_MODEL), _F32),
        compiler_params=pltpu.CompilerParams(
            dimension_semantics=("arbitrary",), vmem_limit_bytes=VMEM_LIMIT),
        name="moe_combine",
    )(plan["seg_off"], plan["c8"], plan["lo"], h, route, lpos, fg, ys)


def _layer_weights(l, norm1_g, w_in, pool_w, pool_scale, conv_dw_w, conv_dw_b, conv_ln_g, conv_ln_b,
                   conv_pw_w, w_out, norm2_g):
    zero = jnp.zeros((POOL_GROUP, POOL_GROUP), _F32)
    pwl = pool_w[l]
    wpa = jnp.block([[pwl[0], zero], [zero, pwl[1]]])
    wpb = jnp.block([[pwl[2], zero], [zero, pwl[3]]])
    return {
        "g1": norm1_g[l][None, :], "w_in": w_in[l].astype(_BF16),
        "wpa": wpa.astype(_BF16), "wpb": wpb.astype(_BF16), "pscale": pool_scale[l][None, :],
        "dw": conv_dw_w[l], "dwb": conv_dw_b[l][None, :], "lng": conv_ln_g[l][None, :],
        "lnb": conv_ln_b[l][None, :], "pw": conv_pw_w[l].astype(_BF16), "w_out": w_out[l].astype(_BF16),
        "g2": norm2_g[l][None, :],
    }


def _routing_plan(cnt):
    n_tok_tiles = cnt.shape[0]
    n = n_tok_tiles * ROW_TILE
    c8 = ((cnt + SEG_ALIGN - 1) // SEG_ALIGN) * SEG_ALIGN
    lo = jnp.cumsum(c8, axis=1) - c8
    within = jnp.cumsum(c8, axis=0) - c8
    glen = jnp.sum(c8, axis=0)
    gpad = ((glen + MOE_ROW_TILE - 1) // MOE_ROW_TILE) * MOE_ROW_TILE
    ends = jnp.cumsum(gpad)
    goff = ends - gpad
    seg_off = goff[None, :] + within
    max_rows = 2 * n + (SEG_ALIGN - 1) * N_EXPERTS * n_tok_tiles
    n_tiles = -(-max_rows // MOE_ROW_TILE) + N_EXPERTS
    starts = jnp.arange(n_tiles, dtype=jnp.int32) * MOE_ROW_TILE
    te = jnp.sum((starts[:, None] >= ends[None, :]).astype(jnp.int32), axis=1)
    n_used = (ends[-1] // MOE_ROW_TILE).astype(jnp.int32)
    te_c = jnp.minimum(te, N_EXPERTS - 1)
    valid = jnp.clip(goff[te_c] + glen[te_c] - starts, 0, MOE_ROW_TILE)
    valid = jnp.where(te < N_EXPERTS, valid, 0).astype(jnp.int32)
    last_e = te_c[jnp.maximum(n_used - 1, 0)]
    te_c = jnp.where(te < N_EXPERTS, te_c, last_e).astype(jnp.int32)
    flat = lambda a: a.reshape(-1).astype(jnp.int32)
    total = jnp.full((1,), n_tiles * MOE_ROW_TILE, jnp.int32)
    fill_off = jnp.concatenate([goff + glen, ends[-1:]])
    fill_n = jnp.concatenate([gpad - glen, total - ends[-1:]]) // SEG_ALIGN
    return {"seg_off": flat(seg_off), "c8": flat(c8), "lo": flat(lo), "fill_off": flat(fill_off),
            "fill_n": flat(fill_n), "tile_expert": te_c, "tile_valid": valid, "n_used": n_used.reshape(1),
            "n_tiles": n_tiles}


def kernel(x_prompt, x_sample, state_pool, state_conv, meta_tokens, norm1_g, w_in, pool_w, pool_scale,
           conv_dw_w, conv_dw_b, conv_ln_g, conv_ln_b, conv_pw_w, w_out, norm2_g, ffn_w_gate, ffn_w_up,
           ffn_w_down, router_w, moe_w_gate, moe_w_up, moe_w_down, final_norm_g):
    batch, seq, _ = x_prompt.shape
    dec_batch, dec_seq, _ = x_sample.shape
    assert dec_seq == DEC_SEQ and seq % ROW_TILE == 0 and (dec_batch * dec_seq) % ROW_TILE == 0
    assert (dec_batch * dec_seq) % DEC_ROW_TILE == 0
    lws = [_layer_weights(l, norm1_g, w_in, pool_w, pool_scale, conv_dw_w, conv_dw_b, conv_ln_g, conv_ln_b,
                          conv_pw_w, w_out, norm2_g) for l in range(2)]
    lws[0].update(wg=ffn_w_gate[0].astype(_BF16), wu=ffn_w_up[0].astype(_BF16), wd=ffn_w_down[0].astype(_BF16))
    rw = jnp.zeros((D_MODEL, ROUTE_LANES), _F32).at[:, :N_EXPERTS].set(router_w[0])
    lws[1].update(rw=rw.astype(_BF16))

    n_p = batch * seq
    n_s = dec_batch * dec_seq
    n_t = seq // ROW_TILE
    hp = x_prompt.reshape(n_p, D_MODEL)
    hs = x_sample.reshape(n_s, D_MODEL)
    hm = meta_tokens.astype(_F32)
    zero_pp = jnp.zeros((1, POOL_HALO, POOL_WIDTH), _F32)
    zero_cp = jnp.zeros((1, CONV_HALO, CONV_WIDTH), _F32)
    pool_tm = jnp.transpose(state_pool, (0, 2, 1, 3))
    conv_tm = jnp.transpose(state_conv, (0, 2, 1, 3))

    seq_kw = dict(mode="seq", tile=ROW_TILE, n_seq=batch, n_t=n_t)
    meta_kw = dict(mode="seq", ramp=True, tile=N_META, n_seq=1, n_t=1)
    dec_kw = dict(mode="dec", ramp=False, tile=DEC_ROW_TILE, n_seq=0, n_t=0)

    hm, pst_m0, cst_m0 = _front_call(hm, zero_pp, zero_cp, lws[0], tail="ffn", **meta_kw)
    hp, pst_p0, cst_p0 = _front_call(hp, pst_m0, cst_m0, lws[0], ramp=False, tail="ffn", skew=True, **seq_kw)
    hs, up_s0, v_s0 = _front_call(hs, pool_tm, conv_tm, lws[0], tail="ffn", layer=0, **dec_kw)

    _, pst_m1, cst_m1 = _front_call(hm, zero_pp, zero_cp, lws[1], tail="none", **meta_kw)
    hp, pst_p1, cst_p1, hn_p, route_p, cnt_p = _front_call(hp, pst_m1, cst_m1, lws[1], ramp=False, tail="route", **seq_kw)
    hs, up_s1, v_s1, hn_s, route_s, cnt_s = _front_call(hs, pool_tm, conv_tm, lws[1], tail="route", layer=1, **dec_kw)

    per_tile = lambda c, tile: c[::8, :N_EXPERTS].reshape(-1, ROW_TILE // tile, N_EXPERTS).sum(axis=1)
    cnt = jnp.concatenate([per_tile(cnt_p, ROW_TILE), per_tile(cnt_s, DEC_ROW_TILE)], axis=0)
    plan = _routing_plan(cnt.astype(jnp.int32))
    s_tile0 = n_p // ROW_TILE

    xs, lpos = _dispatch_call(plan, hn_p, route_p, hn_s, route_s, tile=ROW_TILE)
    ys = _moe_call(plan["tile_expert"], plan["tile_valid"], plan["n_used"], xs,
                   moe_w_gate[0], moe_w_up[0], moe_w_down[0])

    fg = final_norm_g[None, :]
    y_p = _combine_call(plan, hp, route_p, lpos, fg, ys, tile=ROW_TILE, tile0=0)
    y_s = _combine_call(plan, hs, route_s, lpos, fg, ys, tile=ROW_TILE, tile0=s_tile0)

    new_pool_p = jnp.stack([pst_p0, pst_p1])[:, :, POOL_HALO - POOL_BUF:, :]
    new_conv_p = jnp.stack([cst_p0, cst_p1])[:, :, CONV_HALO - CONV_BUF:, :]
    new_pool_s = jnp.concatenate([pool_tm, jnp.stack([up_s0, up_s1])], axis=1)[:, -POOL_BUF:]
    new_conv_s = jnp.concatenate([conv_tm, jnp.stack([v_s0, v_s1])], axis=1)[:, -CONV_BUF:]
    new_pool_s = jnp.transpose(new_pool_s, (0, 2, 1, 3))
    new_conv_s = jnp.transpose(new_conv_s, (0, 2, 1, 3))
    return (y_p.reshape(batch, seq, D_MODEL), y_s.reshape(dec_batch, dec_seq, D_MODEL),
            new_pool_p, new_conv_p, new_pool_s, new_conv_s)
```

```python
import functools

import jax
import jax.numpy as jnp
from jax import lax
from jax.experimental import pallas as pl
from jax.experimental.pallas import tpu as pltpu

D_MODEL = 1024
N_META = 16
POOL_WIDTH = 512
CONV_WIDTH = 512
POOL_WINDOWS = (2, 4, 8, 16)
POOL_GROUP = 128
POOL_BUF = 15
CONV_KERNEL = 31
CONV_BUF = 30
IN_COLS = POOL_WIDTH + 2 * CONV_WIDTH
N_EXPERTS = 8
DEC_SEQ = 8
RMS_EPS = 1e-6
LN_EPS = 1e-5

POOL_HALO = 16
CONV_HALO = 32
CONV_SHIFT_PAD = CONV_HALO - 8
ROUTE_LANES = 128

ROW_TILE = 512
DEC_ROW_TILE = 256
MIX_CHUNK = 64
FFN_CHUNK = 512

MOE_ROW_TILE = 1024
MOE_SUB = 256
MOE_FF_TILE = 512
SEG_ALIGN = 8
SEG_CHUNK = 64
LOCAL_ROWS = -(-(2 * ROW_TILE + (SEG_ALIGN - 1) * N_EXPERTS) // 128) * 128

VMEM_LIMIT = 56 * 1024 * 1024

_BF16 = jnp.bfloat16
_F32 = jnp.float32


def _dot(a, b):
    return jnp.dot(a, b, preferred_element_type=_F32)


def _sigmoid(x):
    return 1.0 / (1.0 + jnp.exp(-x))


def _rmsnorm(x, g):
    return x * lax.rsqrt(jnp.mean(x * x, axis=-1, keepdims=True) + RMS_EPS) * g


def _mix_chunk(ep_ref, ec_ref, r0, rows, dw_ref, dwb_ref, lng_ref, lnb_ref, pos0, sh_ref=None):
    def conv_rows(start, c0):
        if sh_ref is None or start % 8 == 0:
            return ec_ref[start:start + rows, c0:c0 + 128]
        q, r = divmod(start, 8)
        return sh_ref[r - 1, 8 * q:8 * q + rows, c0:c0 + 128]

    pooled_cols = []
    for gi, win in enumerate(POOL_WINDOWS):
        c0 = gi * POOL_GROUP
        base = POOL_HALO + r0
        x0 = ep_ref[base:base + rows, c0:c0 + POOL_GROUP]
        s = x0
        for k in range(1, win):
            s = s + ep_ref[base - k:base - k + rows, c0:c0 + POOL_GROUP]
        if pos0 is None:
            pooled = s * (1.0 / win) - x0
        else:
            pos = pos0 + r0 + lax.broadcasted_iota(jnp.int32, (rows, POOL_GROUP), 0)
            cnt = jnp.minimum(pos + 1, win).astype(_F32)
            pooled = s / cnt - x0
        pooled_cols.append(pooled)
    pooled = jnp.concatenate(pooled_cols, axis=-1)

    conv_cols = []
    for cg in range(CONV_WIDTH // 128):
        c0 = cg * 128
        base = CONV_HALO - CONV_BUF + r0
        acc = conv_rows(base, c0) * dw_ref[0:1, c0:c0 + 128]
        for k in range(1, CONV_KERNEL):
            acc = acc + conv_rows(base + k, c0) * dw_ref[k:k + 1, c0:c0 + 128]
        conv_cols.append(acc + dwb_ref[:, c0:c0 + 128])
    y = jnp.concatenate(conv_cols, axis=-1)
    mu = jnp.mean(y, axis=-1, keepdims=True)
    d = y - mu
    var = jnp.mean(d * d, axis=-1, keepdims=True)
    yn = d * lax.rsqrt(var + LN_EPS) * lng_ref[...] + lnb_ref[...]
    return pooled, yn * _sigmoid(yn)


def _front_kernel(*refs, mode, ramp, tail, tile, n_t, skew):
    it = iter(refs)
    h_ref, ppast_ref, cpast_ref = next(it), next(it), next(it)
    g1_ref, win_ref, wpa_ref, wpb_ref, psc_ref = next(it), next(it), next(it), next(it), next(it)
    dw_ref, dwb_ref, lng_ref, lnb_ref, pw_ref, wout_ref, g2_ref = (next(it) for _ in range(7))
    if tail == "ffn":
        wg_ref, wu_ref, wd_ref = next(it), next(it), next(it)
    elif tail == "route":
        rw_ref = next(it)
    hout_ref = next(it)
    if mode == "seq":
        pst_ref, cst_ref = next(it), next(it)
    else:
        up_ref, v_ref = next(it), next(it)
    if tail == "route":
        hn2_ref, route_ref, cnt_ref = next(it), next(it), next(it)
    ep_ref, ec_ref, mixed_ref = next(it), next(it), next(it)
    if mode == "seq":
        sh_ref = next(it)
    if skew:
        h1_ref = next(it)

    def ffn_pieces(h_in):
        hb = _rmsnorm(h_in, g2_ref[...]).astype(_BF16)
        d_ff = wg_ref.shape[1]
        hout_ref[...] = h_in

        def piece(c0):
            cs = min(FFN_CHUNK, d_ff - c0)
            g = _dot(hb, wg_ref[:, c0:c0 + cs])
            a = (g * _sigmoid(g) * _dot(hb, wu_ref[:, c0:c0 + cs])).astype(_BF16)
            hout_ref[...] += _dot(a, wd_ref[c0:c0 + cs, :])

        return [functools.partial(piece, c0) for c0 in range(0, d_ff, FFN_CHUNK)]

    def ffn_tail(h_in):
        for piece in ffn_pieces(h_in):
            piece()

    if mode == "seq":
        if skew:
            step = pl.program_id(0)
            t = lax.rem(step, n_t)

            @pl.when(step == 0)
            def _():
                h1_ref[...] = jnp.zeros(h1_ref.shape, _F32)
        else:
            t = pl.program_id(1)

        @pl.when(t == 0)
        def _():
            ep_ref[0:POOL_HALO, :] = ppast_ref[0]
            ec_ref[0:CONV_HALO, :] = cpast_ref[0]

    pending = ffn_pieces(h1_ref[...]) if skew else []

    x = h_ref[...]
    hn = _rmsnorm(x, g1_ref[...]).astype(_BF16)
    u = _dot(hn, win_ref[...])
    up = u[:, :POOL_WIDTH]
    v = u[:, POOL_WIDTH:POOL_WIDTH + CONV_WIDTH] * _sigmoid(u[:, POOL_WIDTH + CONV_WIDTH:])
    if mode == "seq":
        ep_ref[POOL_HALO:, :] = up
        ec_ref[CONV_HALO:, :] = v
    else:
        for g in range(POOL_WIDTH // 128):
            ep_ref[g] = up[:, g * 128:(g + 1) * 128]
            ec_ref[g] = v[:, g * 128:(g + 1) * 128]

    if mode == "seq":
        for r in range(1, 8):
            sh_ref[r - 1] = ec_ref[r:r + tile + CONV_SHIFT_PAD, :]
        chunk = min(MIX_CHUNK, tile)
        n_chunks = tile // chunk
        n_pieces = len(pending)
        for ci in range(n_chunks):
            r0 = ci * chunk
            pos0 = t * tile if ramp else None
            pooled, act = _mix_chunk(ep_ref, ec_ref, r0, chunk, dw_ref, dwb_ref, lng_ref, lnb_ref, pos0, sh_ref)
            mixed_ref[r0:r0 + chunk, 0:POOL_WIDTH] = pooled
            mixed_ref[r0:r0 + chunk, POOL_WIDTH:] = act
            while len(pending) > n_pieces - (ci + 1) * n_pieces // n_chunks:
                pending.pop(0)()
        ptail = ep_ref[tile:tile + POOL_HALO, :]
        ctail = ec_ref[tile:tile + CONV_HALO, :]
        ep_ref[0:POOL_HALO, :] = ptail
        ec_ref[0:CONV_HALO, :] = ctail
        pst_ref[0] = ptail
        cst_ref[0] = ctail
    else:
        spt = tile // DEC_SEQ
        for t in range(DEC_SEQ):
            for g in range(POOL_WIDTH // 128):
                up_ref[t, :, g * 128:(g + 1) * 128] = ep_ref[g, pl.ds(t, spt, stride=DEC_SEQ), :]
                v_ref[t, :, g * 128:(g + 1) * 128] = ec_ref[g, pl.ds(t, spt, stride=DEC_SEQ), :]

        def pool_row(i, cols):
            return ppast_ref[i, :, cols] if i < POOL_BUF else up_ref[i - POOL_BUF, :, cols]

        def conv_row(i):
            return cpast_ref[i] if i < CONV_BUF else v_ref[i - CONV_BUF]

        for t in range(DEC_SEQ):
            pooled_cols = []
            for gi, win in enumerate(POOL_WINDOWS):
                cols = slice(gi * POOL_GROUP, (gi + 1) * POOL_GROUP)
                x0 = pool_row(POOL_BUF + t, cols)
                s = x0
                for k in range(1, win):
                    s = s + pool_row(POOL_BUF + t - k, cols)
                pooled_cols.append(s * (1.0 / win) - x0)
            y = conv_row(t) * dw_ref[0:1, :]
            for k in range(1, CONV_KERNEL):
                y = y + conv_row(t + k) * dw_ref[k:k + 1, :]
            y = y + dwb_ref[...]
            mu = jnp.mean(y, axis=-1, keepdims=True)
            d = y - mu
            var = jnp.mean(d * d, axis=-1, keepdims=True)
            yn = d * lax.rsqrt(var + LN_EPS) * lng_ref[...] + lnb_ref[...]
            act = yn * _sigmoid(yn)
            for g in range(POOL_WIDTH // 128):
                mixed_ref[g, pl.ds(t, spt, stride=DEC_SEQ), :] = pooled_cols[g]
                mixed_ref[POOL_WIDTH // 128 + g, pl.ds(t, spt, stride=DEC_SEQ), :] = act[:, g * 128:(g + 1) * 128]

    def mixed_cols(c0, c1):
        if mode == "seq":
            return mixed_ref[:, c0:c1].astype(_BF16)
        return jnp.concatenate([mixed_ref[g] for g in range(c0 // 128, c1 // 128)], axis=-1).astype(_BF16)

    pa = _dot(mixed_cols(0, 256), wpa_ref[...])
    pb = _dot(mixed_cols(256, 512), wpb_ref[...])
    pool_out = jnp.concatenate([pa, pb], axis=-1) * psc_ref[...]
    conv_out = _dot(mixed_cols(POOL_WIDTH, POOL_WIDTH + CONV_WIDTH), pw_ref[...])
    h1 = (x + _dot(pool_out.astype(_BF16), wout_ref[0:POOL_WIDTH, :])
          + _dot(conv_out.astype(_BF16), wout_ref[POOL_WIDTH:, :]))

    if tail == "none":
        hout_ref[...] = h1
    elif skew:
        h1_ref[...] = h1
    elif tail == "ffn":
        ffn_tail(h1)
    else:
        hn2 = _rmsnorm(h1, g2_ref[...])
        hout_ref[...] = h1
        hn2_ref[...] = hn2.astype(_BF16)
        logits = _dot(hn2.astype(_BF16), rw_ref[...])
        lane = lax.broadcasted_iota(jnp.int32, logits.shape, 1).astype(_F32)
        neg = jnp.float32(-jnp.inf)
        lg = jnp.where(lane < N_EXPERTS, logits, neg)
        m1 = jnp.max(lg, axis=-1, keepdims=True)
        i1 = jnp.min(jnp.where(lg == m1, lane, float(ROUTE_LANES)), axis=-1, keepdims=True)
        lg2 = jnp.where(lane == i1, neg, lg)
        m2 = jnp.max(lg2, axis=-1, keepdims=True)
        i2 = jnp.min(jnp.where(lg2 == m2, lane, float(ROUTE_LANES)), axis=-1, keepdims=True)
        e21 = jnp.exp(m2 - m1)
        gate1 = 1.0 / (1.0 + e21)
        gate2 = e21 * gate1
        route_ref[...] = jnp.where(lane == 0, i1, jnp.where(lane == 1, i2, jnp.where(
            lane == 2, gate1, jnp.where(lane == 3, gate2, 0.0))))
        chosen = (lane == i1).astype(_F32) + (lane == i2).astype(_F32)
        cnt_ref[...] = jnp.broadcast_to(jnp.sum(chosen, axis=0, keepdims=True), cnt_ref.shape)


def _const_spec(shape):
    nd = len(shape)
    return pl.BlockSpec(shape, lambda *_: (0,) * nd, pipeline_mode=pl.Buffered(1))


def _front_call(h, ppast, cpast, lw, *, mode, ramp, tail, tile, n_seq, n_t, skew=False, layer=None):
    rows = h.shape[0]
    weights = [lw["g1"], lw["w_in"], lw["wpa"], lw["wpb"], lw["pscale"], lw["dw"], lw["dwb"], lw["lng"],
               lw["lnb"], lw["pw"], lw["w_out"], lw["g2"]]
    if tail == "ffn":
        weights += [lw["wg"], lw["wu"], lw["wd"]]
    elif tail == "route":
        weights += [lw["rw"]]
    w_specs = [_const_spec(w.shape) for w in weights]

    if mode == "seq" and skew:
        assert tail == "ffn" and ppast.shape[0] == 1
        n_tiles = n_seq * n_t
        grid = (n_tiles + 1,)
        row_map = lambda s: (jnp.minimum(s, n_tiles - 1), 0)
        seq_map = lambda s: (jnp.minimum(s, n_tiles - 1) // n_t, 0, 0)
        past_specs = [pl.BlockSpec((1, POOL_HALO, POOL_WIDTH), lambda s: (0, 0, 0)),
                      pl.BlockSpec((1, CONV_HALO, CONV_WIDTH), lambda s: (0, 0, 0))]
        out_shape = [jax.ShapeDtypeStruct((rows, D_MODEL), _F32),
                     jax.ShapeDtypeStruct((n_seq, POOL_HALO, POOL_WIDTH), _F32),
                     jax.ShapeDtypeStruct((n_seq, CONV_HALO, CONV_WIDTH), _F32)]
        out_specs = [pl.BlockSpec((tile, D_MODEL), lambda s: (jnp.maximum(s - 1, 0), 0)),
                     pl.BlockSpec((1, POOL_HALO, POOL_WIDTH), seq_map),
                     pl.BlockSpec((1, CONV_HALO, CONV_WIDTH), seq_map)]
        semantics = ("arbitrary",)
    elif mode == "seq":
        grid = (n_seq, n_t)
        row_map = lambda b, t: (b * n_t + t, 0)
        shared = ppast.shape[0] == 1
        past_map = (lambda b, t: (0, 0, 0)) if shared else (lambda b, t: (b, 0, 0))
        past_specs = [pl.BlockSpec((1, POOL_HALO, POOL_WIDTH), past_map),
                      pl.BlockSpec((1, CONV_HALO, CONV_WIDTH), past_map)]
        out_shape = [jax.ShapeDtypeStruct((rows, D_MODEL), _F32),
                     jax.ShapeDtypeStruct((n_seq, POOL_HALO, POOL_WIDTH), _F32),
                     jax.ShapeDtypeStruct((n_seq, CONV_HALO, CONV_WIDTH), _F32)]
        out_specs = [pl.BlockSpec((tile, D_MODEL), row_map),
                     pl.BlockSpec((1, POOL_HALO, POOL_WIDTH), lambda b, t: (b, 0, 0)),
                     pl.BlockSpec((1, CONV_HALO, CONV_WIDTH), lambda b, t: (b, 0, 0))]
        semantics = ("arbitrary", "arbitrary")
    else:
        grid = (rows // tile,)
        row_map = lambda i: (i, 0)
        spt = tile // DEC_SEQ
        past_specs = [pl.BlockSpec((None, POOL_BUF, spt, POOL_WIDTH), lambda i: (layer, 0, i, 0)),
                      pl.BlockSpec((None, CONV_BUF, spt, CONV_WIDTH), lambda i: (layer, 0, i, 0))]
        out_shape = [jax.ShapeDtypeStruct((rows, D_MODEL), _F32),
                     jax.ShapeDtypeStruct((DEC_SEQ, rows // DEC_SEQ, POOL_WIDTH), _F32),
                     jax.ShapeDtypeStruct((DEC_SEQ, rows // DEC_SEQ, CONV_WIDTH), _F32)]
        out_specs = [pl.BlockSpec((tile, D_MODEL), row_map),
                     pl.BlockSpec((DEC_SEQ, spt, POOL_WIDTH), lambda i: (0, i, 0)),
                     pl.BlockSpec((DEC_SEQ, spt, CONV_WIDTH), lambda i: (0, i, 0))]
        semantics = ("arbitrary",)
    if tail == "route":
        out_shape += [jax.ShapeDtypeStruct((rows, D_MODEL), _BF16),
                      jax.ShapeDtypeStruct((rows, ROUTE_LANES), _F32),
                      jax.ShapeDtypeStruct((rows // tile * 8, ROUTE_LANES), _F32)]
        out_specs += [pl.BlockSpec((tile, D_MODEL), row_map), pl.BlockSpec((tile, ROUTE_LANES), row_map),
                      pl.BlockSpec((8, ROUTE_LANES), row_map)]

    if mode == "seq":
        scratch = [pltpu.VMEM((POOL_HALO + tile, POOL_WIDTH), _F32),
                   pltpu.VMEM((CONV_HALO + tile, CONV_WIDTH), _F32),
                   pltpu.VMEM((tile, POOL_WIDTH + CONV_WIDTH), _F32),
                   pltpu.VMEM((7, tile + CONV_SHIFT_PAD, CONV_WIDTH), _F32)]
    else:
        scratch = [pltpu.VMEM((POOL_WIDTH // 128, tile, 128), _F32),
                   pltpu.VMEM((CONV_WIDTH // 128, tile, 128), _F32),
                   pltpu.VMEM(((POOL_WIDTH + CONV_WIDTH) // 128, tile, 128), _F32)]
    if skew:
        scratch += [pltpu.VMEM((tile, D_MODEL), _F32)]

    body = functools.partial(_front_kernel, mode=mode, ramp=ramp, tail=tail, tile=tile, n_t=n_t, skew=skew)
    return pl.pallas_call(
        body,
        grid=grid,
        in_specs=[pl.BlockSpec((tile, D_MODEL), row_map)] + past_specs + w_specs,
        out_specs=out_specs,
        out_shape=out_shape,
        scratch_shapes=scratch,
        compiler_params=pltpu.CompilerParams(dimension_semantics=semantics, vmem_limit_bytes=VMEM_LIMIT),
        name=f"front_{mode}_{tail}" + ("_skew" if skew else ""),
    )(h, ppast, cpast, *weights)


def _segment_copies(c8_ref, lo_ref, segoff_ref, tidx, make_copy, act):
    for e in range(N_EXPERTS):
        k = tidx * N_EXPERTS + e
        lo_e = lo_ref[k]
        so_e = segoff_ref[k]
        n_big = c8_ref[k] // SEG_CHUNK
        rest = n_big * SEG_CHUNK

        def big(c, carry):
            act(make_copy(pl.multiple_of(lo_e + c * SEG_CHUNK, SEG_ALIGN),
                          pl.multiple_of(so_e + c * SEG_CHUNK, SEG_ALIGN), SEG_CHUNK))
            return carry

        def small(c, carry):
            act(make_copy(pl.multiple_of(lo_e + rest + c * SEG_ALIGN, SEG_ALIGN),
                          pl.multiple_of(so_e + rest + c * SEG_ALIGN, SEG_ALIGN), SEG_ALIGN))
            return carry

        lax.fori_loop(0, n_big, big, 0)
        lax.fori_loop(0, (c8_ref[k] - rest) // SEG_ALIGN, small, 0)


def _dispatch_kernel(segoff_ref, c8_ref, lo_ref, filloff_ref, filln_ref, hn_a_ref, route_a_ref, hn_b_ref,
                     route_b_ref, xs_hbm, lpos_ref, xl_ref, sem, zero_ref, fill_sem, *, tile, n_a):
    t = pl.program_id(0)
    n = pl.num_programs(0)
    slot = t % 2
    tt = t
    in_a = t < n_a
    hn = jnp.where(in_a, hn_a_ref[...], hn_b_ref[...])
    route = jnp.where(in_a, route_a_ref[...], route_b_ref[...])

    @pl.when(t == 0)
    def _():
        zero_ref[...] = jnp.zeros(zero_ref.shape, _F32)

        def fill(act):
            for r in range(N_EXPERTS + 1):
                off = filloff_ref[r]
                n_big = filln_ref[r] // (SEG_CHUNK // SEG_ALIGN)
                rest = n_big * SEG_CHUNK

                def big(c, carry):
                    dst = xs_hbm.at[pl.ds(pl.multiple_of(off + c * SEG_CHUNK, SEG_ALIGN), SEG_CHUNK)]
                    act(pltpu.make_async_copy(zero_ref, dst, fill_sem))
                    return carry

                def small(c, carry):
                    dst = xs_hbm.at[pl.ds(pl.multiple_of(off + rest + c * SEG_ALIGN, SEG_ALIGN), SEG_ALIGN)]
                    act(pltpu.make_async_copy(zero_ref.at[pl.ds(0, SEG_ALIGN)], dst, fill_sem))
                    return carry

                lax.fori_loop(0, n_big, big, 0)
                lax.fori_loop(0, filln_ref[r] - n_big * (SEG_CHUNK // SEG_ALIGN), small, 0)

        fill(lambda c: c.start())
        fill(lambda c: c.wait())

    lane = lax.broadcasted_iota(jnp.int32, (tile, ROUTE_LANES), 1).astype(_F32)
    oh0 = (lane == route[:, 0:1]).astype(_F32)
    oh1 = (lane == route[:, 1:2]).astype(_F32)
    earlier = (lax.broadcasted_iota(jnp.int32, (tile, tile), 1)
               < lax.broadcasted_iota(jnp.int32, (tile, tile), 0)).astype(_BF16)
    rank = _dot(earlier, (oh0 + oh1).astype(_BF16))
    lane1 = lax.broadcasted_iota(jnp.int32, (1, ROUTE_LANES), 1)
    lo_vec = jnp.zeros((1, ROUTE_LANES), _F32)
    for e in range(N_EXPERTS):
        lo_vec = jnp.where(lane1 == e, lo_ref[tt * N_EXPERTS + e].astype(_F32), lo_vec)
    posv = rank + lo_vec
    p0 = jnp.sum(oh0 * posv, axis=-1, keepdims=True)
    p1 = jnp.sum(oh1 * posv, axis=-1, keepdims=True)
    slab = jnp.where(lane == 0, p0, jnp.where(lane == 1, p1, 0.0))
    lpos_ref[...] = slab
    slab_t = slab.T
    rowi = lax.broadcasted_iota(jnp.int32, (LOCAL_ROWS, tile), 0).astype(_F32)
    place = jnp.where(jnp.logical_or(rowi == slab_t[0:1, :], rowi == slab_t[1:2, :]), 1.0, 0.0).astype(_BF16)
    xl_ref[slot] = _dot(place, hn.astype(_BF16))

    def copies(tidx, s):
        def make_copy(lo, so, rows):
            return pltpu.make_async_copy(xl_ref.at[s, pl.ds(lo, rows)], xs_hbm.at[pl.ds(so, rows)], sem.at[s])
        return functools.partial(_segment_copies, c8_ref, lo_ref, segoff_ref, tidx, make_copy)

    copies(tt, slot)(lambda c: c.start())

    @pl.when(t > 0)
    def _():
        copies(tt - 1, 1 - slot)(lambda c: c.wait())

    @pl.when(t == n - 1)
    def _():
        copies(tt, slot)(lambda c: c.wait())


def _dispatch_call(plan, hn_a, route_a, hn_b, route_b, *, tile):
    n_a = hn_a.shape[0] // tile
    n_b = hn_b.shape[0] // tile
    amap = lambda i, *_: (jnp.minimum(i, n_a - 1), 0)
    bmap = lambda i, *_: (jnp.maximum(i - n_a, 0), 0)
    tables = (plan["seg_off"], plan["c8"], plan["lo"], plan["fill_off"], plan["fill_n"])
    return pl.pallas_call(
        functools.partial(_dispatch_kernel, tile=tile, n_a=n_a),
        grid_spec=pltpu.PrefetchScalarGridSpec(
            num_scalar_prefetch=len(tables),
            grid=(n_a + n_b,),
            in_specs=[pl.BlockSpec((tile, D_MODEL), amap), pl.BlockSpec((tile, ROUTE_LANES), amap),
                      pl.BlockSpec((tile, D_MODEL), bmap), pl.BlockSpec((tile, ROUTE_LANES), bmap)],
            out_specs=[pl.BlockSpec(memory_space=pl.ANY), pl.BlockSpec((tile, ROUTE_LANES), lambda i, *_: (i, 0))],
            scratch_shapes=[pltpu.VMEM((2, LOCAL_ROWS, D_MODEL), _F32), pltpu.SemaphoreType.DMA((2,)),
                            pltpu.VMEM((SEG_CHUNK, D_MODEL), _F32), pltpu.SemaphoreType.DMA(())],
        ),
        out_shape=[jax.ShapeDtypeStruct((plan["n_tiles"] * MOE_ROW_TILE, D_MODEL), _F32),
                   jax.ShapeDtypeStruct(((n_a + n_b) * tile, ROUTE_LANES), _F32)],
        compiler_params=pltpu.CompilerParams(
            dimension_semantics=("arbitrary",), vmem_limit_bytes=VMEM_LIMIT, has_side_effects=True),
        name="moe_dispatch",
    )(*tables, hn_a, route_a, hn_b, route_b)


def _moe_kernel(te_ref, nv_ref, nu_ref, x_ref, wg_ref, wu_ref, wd_ref, o_ref, xb_ref):
    del te_ref, nu_ref
    i = pl.program_id(0)
    j = pl.program_id(1)
    nv = nv_ref[i]

    @pl.when(j == 0)
    def _():
        o_ref[...] = jnp.zeros(o_ref.shape, _F32)
        xb_ref[...] = x_ref[...].astype(_BF16)

    def expert_ffn(xs, wg, wu, wd):
        g = _dot(xs, wg)
        a = (g * _sigmoid(g) * _dot(xs, wu)).astype(_BF16)
        return _dot(a, wd)

    for r in range(MOE_SUB, MOE_ROW_TILE + 1, MOE_SUB):
        @pl.when(jnp.logical_and(nv > r - MOE_SUB, nv <= r))
        def _():
            o_ref[0:r, :] += expert_ffn(xb_ref[0:r, :], wg_ref[0].astype(_BF16), wu_ref[0].astype(_BF16),
                                        wd_ref[0].astype(_BF16))


def _moe_call(tile_expert, tile_valid, n_used, xs, wg, wu, wd):
    n_tiles = xs.shape[0] // MOE_ROW_TILE
    eff = wg.shape[2]
    n_j = eff // MOE_FF_TILE

    def row_map(i, j, te, nv, nu):
        return (jnp.maximum(jnp.minimum(i, nu[0] - 1), 0), 0)

    def col_of(i, j, nu):
        return jnp.where(i < nu[0], j, n_j - 1)

    return pl.pallas_call(
        _moe_kernel,
        grid_spec=pltpu.PrefetchScalarGridSpec(
            num_scalar_prefetch=3,
            grid=(n_tiles, n_j),
            in_specs=[
                pl.BlockSpec((MOE_ROW_TILE, D_MODEL), row_map),
                pl.BlockSpec((1, D_MODEL, MOE_FF_TILE), lambda i, j, te, nv, nu: (te[i], 0, col_of(i, j, nu))),
                pl.BlockSpec((1, D_MODEL, MOE_FF_TILE), lambda i, j, te, nv, nu: (te[i], 0, col_of(i, j, nu))),
                pl.BlockSpec((1, MOE_FF_TILE, D_MODEL), lambda i, j, te, nv, nu: (te[i], col_of(i, j, nu), 0)),
            ],
            out_specs=pl.BlockSpec((MOE_ROW_TILE, D_MODEL), lambda i, j, te, nv, nu: (i, 0)),
            scratch_shapes=[pltpu.VMEM((MOE_ROW_TILE, D_MODEL), _BF16)],
        ),
        out_shape=jax.ShapeDtypeStruct(xs.shape, _F32),
        compiler_params=pltpu.CompilerParams(
            dimension_semantics=("arbitrary", "arbitrary"), vmem_limit_bytes=VMEM_LIMIT),
        name="moe_grouped",
    )(tile_expert, tile_valid, n_used, xs, wg, wu, wd)


def _combine_kernel(segoff_ref, c8_ref, lo_ref, h_ref, route_ref, lpos_ref, fg_ref, ys_hbm, o_ref, yl_ref, sem,
                    *, tile, tile0):
    t = pl.program_id(0)
    n = pl.num_programs(0)
    slot = t % 2
    tt = tile0 + t

    def copies(tidx, s):
        def make_copy(lo, so, rows):
            return pltpu.make_async_copy(ys_hbm.at[pl.ds(so, rows)], yl_ref.at[s, pl.ds(lo, rows)], sem.at[s])
        return functools.partial(_segment_copies, c8_ref, lo_ref, segoff_ref, tidx, make_copy)

    @pl.when(t == 0)
    def _():
        yl_ref[...] = jnp.zeros(yl_ref.shape, _F32)
        copies(tt, slot)(lambda c: c.start())

    @pl.when(t + 1 < n)
    def _():
        copies(tt + 1, 1 - slot)(lambda c: c.start())

    copies(tt, slot)(lambda c: c.wait())
    last = tt * N_EXPERTS + N_EXPERTS - 1
    used = lo_ref[last] + c8_ref[last]
    rowi = lax.broadcasted_iota(jnp.int32, (LOCAL_ROWS, 1), 0)
    ylb = jnp.where(rowi < used, yl_ref[slot], 0.0).astype(_BF16)
    lanel = lax.broadcasted_iota(jnp.int32, (tile, LOCAL_ROWS), 1).astype(_F32)
    take = (jnp.where(lanel == lpos_ref[:, 0:1], route_ref[:, 2:3], 0.0)
            + jnp.where(lanel == lpos_ref[:, 1:2], route_ref[:, 3:4], 0.0)).astype(_BF16)
    o_ref[...] = _rmsnorm(h_ref[...] + _dot(take, ylb), fg_ref[...])


def _combine_call(plan, h, route, lpos, fg, ys, *, tile, tile0):
    rows = h.shape[0]
    smap = lambda i, *_: (i, 0)
    return pl.pallas_call(
        functools.partial(_combine_kernel, tile=tile, tile0=tile0),
        grid_spec=pltpu.PrefetchScalarGridSpec(
            num_scalar_prefetch=3,
            grid=(rows // tile,),
            in_specs=[
                pl.BlockSpec((tile, D_MODEL), smap),
                pl.BlockSpec((tile, ROUTE_LANES), smap),
                pl.BlockSpec((tile, ROUTE_LANES), lambda i, *_: (i + tile0, 0)),
                pl.BlockSpec((1, D_MODEL), lambda i, *_: (0, 0)),
                pl.BlockSpec(memory_space=pl.ANY),
            ],
            out_specs=pl.BlockSpec((tile, D_MODEL), smap),
            scratch_shapes=[pltpu.VMEM((2, LOCAL_ROWS, D_MODEL), _F32), pltpu.SemaphoreType.DMA((2,))],
        ),
        out_shape=jax.ShapeDtypeStruct((rows, D_MODEL), _F32),
        compiler_params=pltpu.CompilerParams(
            dimension_semantics=("arbitrary",), vmem_limit_bytes=VMEM_LIMIT),
        name="moe_combine",
    )(plan["seg_off"], plan["c8"], plan["lo"], h, route, lpos, fg, ys)


def _layer_weights(l, norm1_g, w_in, pool_w, pool_scale, conv_dw_w, conv_dw_b, conv_ln_g, conv_ln_b,
                   conv_pw_w, w_out, norm2_g):
    zero = jnp.zeros((POOL_GROUP, POOL_GROUP), _F32)
    pwl = pool_w[l]
    wpa = jnp.block([[pwl[0], zero], [zero, pwl[1]]])
    wpb = jnp.block([[pwl[2], zero], [zero, pwl[3]]])
    return {
        "g1": norm1_g[l][None, :], "w_in": w_in[l].astype(_BF16),
        "wpa": wpa.astype(_BF16), "wpb": wpb.astype(_BF16), "pscale": pool_scale[l][None, :],
        "dw": conv_dw_w[l], "dwb": conv_dw_b[l][None, :], "lng": conv_ln_g[l][None, :],
        "lnb": conv_ln_b[l][None, :], "pw": conv_pw_w[l].astype(_BF16), "w_out": w_out[l].astype(_BF16),
        "g2": norm2_g[l][None, :],
    }


def _routing_plan(cnt):
    n_tok_tiles = cnt.shape[0]
    n = n_tok_tiles * ROW_TILE
    c8 = ((cnt + SEG_ALIGN - 1) // SEG_ALIGN) * SEG_ALIGN
    lo = jnp.cumsum(c8, axis=1) - c8
    within = jnp.cumsum(c8, axis=0) - c8
    glen = jnp.sum(c8, axis=0)
    gpad = ((glen + MOE_ROW_TILE - 1) // MOE_ROW_TILE) * MOE_ROW_TILE
    ends = jnp.cumsum(gpad)
    goff = ends - gpad
    seg_off = goff[None, :] + within
    max_rows = 2 * n + (SEG_ALIGN - 1) * N_EXPERTS * n_tok_tiles
    n_tiles = -(-max_rows // MOE_ROW_TILE) + N_EXPERTS
    starts = jnp.arange(n_tiles, dtype=jnp.int32) * MOE_ROW_TILE
    te = jnp.sum((starts[:, None] >= ends[None, :]).astype(jnp.int32), axis=1)
    n_used = (ends[-1] // MOE_ROW_TILE).astype(jnp.int32)
    te_c = jnp.minimum(te, N_EXPERTS - 1)
    valid = jnp.clip(goff[te_c] + glen[te_c] - starts, 0, MOE_ROW_TILE)
    valid = jnp.where(te < N_EXPERTS, valid, 0).astype(jnp.int32)
    last_e = te_c[jnp.maximum(n_used - 1, 0)]
    te_c = jnp.where(te < N_EXPERTS, te_c, last_e).astype(jnp.int32)
    flat = lambda a: a.reshape(-1).astype(jnp.int32)
    total = jnp.full((1,), n_tiles * MOE_ROW_TILE, jnp.int32)
    fill_off = jnp.concatenate([goff + glen, ends[-1:]])
    fill_n = jnp.concatenate([gpad - glen, total - ends[-1:]]) // SEG_ALIGN
    return {"seg_off": flat(seg_off), "c8": flat(c8), "lo": flat(lo), "fill_off": flat(fill_off),
            "fill_n": flat(fill_n), "tile_expert": te_c, "tile_valid": valid, "n_used": n_used.reshape(1),
            "n_tiles": n_tiles}


def kernel(x_prompt, x_sample, state_pool, state_conv, meta_tokens, norm1_g, w_in, pool_w, pool_scale,
           conv_dw_w, conv_dw_b, conv_ln_g, conv_ln_b, conv_pw_w, w_out, norm2_g, ffn_w_gate, ffn_w_up,
           ffn_w_down, router_w, moe_w_gate, moe_w_up, moe_w_down, final_norm_g):
    batch, seq, _ = x_prompt.shape
    dec_batch, dec_seq, _ = x_sample.shape
    assert dec_seq == DEC_SEQ and seq % ROW_TILE == 0 and (dec_batch * dec_seq) % ROW_TILE == 0
    assert (dec_batch * dec_seq) % DEC_ROW_TILE == 0
    lws = [_layer_weights(l, norm1_g, w_in, pool_w, pool_scale, conv_dw_w, conv_dw_b, conv_ln_g, conv_ln_b,
                          conv_pw_w, w_out, norm2_g) for l in range(2)]
    lws[0].update(wg=ffn_w_gate[0].astype(_BF16), wu=ffn_w_up[0].astype(_BF16), wd=ffn_w_down[0].astype(_BF16))
    rw = jnp.zeros((D_MODEL, ROUTE_LANES), _F32).at[:, :N_EXPERTS].set(router_w[0])
    lws[1].update(rw=rw.astype(_BF16))

    n_p = batch * seq
    n_s = dec_batch * dec_seq
    n_t = seq // ROW_TILE
    hp = x_prompt.reshape(n_p, D_MODEL)
    hs = x_sample.reshape(n_s, D_MODEL)
    hm = meta_tokens.astype(_F32)
    zero_pp = jnp.zeros((1, POOL_HALO, POOL_WIDTH), _F32)
    zero_cp = jnp.zeros((1, CONV_HALO, CONV_WIDTH), _F32)
    pool_tm = jnp.transpose(state_pool, (0, 2, 1, 3))
    conv_tm = jnp.transpose(state_conv, (0, 2, 1, 3))

    seq_kw = dict(mode="seq", tile=ROW_TILE, n_seq=batch, n_t=n_t)
    meta_kw = dict(mode="seq", ramp=True, tile=N_META, n_seq=1, n_t=1)
    dec_kw = dict(mode="dec", ramp=False, tile=DEC_ROW_TILE, n_seq=0, n_t=0)

    hm, pst_m0, cst_m0 = _front_call(hm, zero_pp, zero_cp, lws[0], tail="ffn", **meta_kw)
    hp, pst_p0, cst_p0 = _front_call(hp, pst_m0, cst_m0, lws[0], ramp=False, tail="ffn", skew=True, **seq_kw)
    hs, up_s0, v_s0 = _front_call(hs, pool_tm, conv_tm, lws[0], tail="ffn", layer=0, **dec_kw)

    _, pst_m1, cst_m1 = _front_call(hm, zero_pp, zero_cp, lws[1], tail="none", **meta_kw)
    hp, pst_p1, cst_p1, hn_p, route_p, cnt_p = _front_call(hp, pst_m1, cst_m1, lws[1], ramp=False, tail="route", **seq_kw)
    hs, up_s1, v_s1, hn_s, route_s, cnt_s = _front_call(hs, pool_tm, conv_tm, lws[1], tail="route", layer=1, **dec_kw)

    per_tile = lambda c, tile: c[::8, :N_EXPERTS].reshape(-1, ROW_TILE // tile, N_EXPERTS).sum(axis=1)
    cnt = jnp.concatenate([per_tile(cnt_p, ROW_TILE), per_tile(cnt_s, DEC_ROW_TILE)], axis=0)
    plan = _routing_plan(cnt.astype(jnp.int32))
    s_tile0 = n_p // ROW_TILE

    xs, lpos = _dispatch_call(plan, hn_p, route_p, hn_s, route_s, tile=ROW_TILE)
    ys = _moe_call(plan["tile_expert"], plan["tile_valid"], plan["n_used"], xs,
                   moe_w_gate[0], moe_w_up[0], moe_w_down[0])

    fg = final_norm_g[None, :]
    y_p = _combine_call(plan, hp, route_p, lpos, fg, ys, tile=ROW_TILE, tile0=0)
    y_s = _combine_call(plan, hs, route_s, lpos, fg, ys, tile=ROW_TILE, tile0=s_tile0)

    new_pool_p = jnp.stack([pst_p0, pst_p1])[:, :, POOL_HALO - POOL_BUF:, :]
    new_conv_p = jnp.stack([cst_p0, cst_p1])[:, :, CONV_HALO - CONV_BUF:, :]
    new_pool_s = jnp.concatenate([pool_tm, jnp.stack([up_s0, up_s1])], axis=1)[:, -POOL_BUF:]
    new_conv_s = jnp.concatenate([conv_tm, jnp.stack([v_s0, v_s1])], axis=1)[:, -CONV_BUF:]
    new_pool_s = jnp.transpose(new_pool_s, (0, 2, 1, 3))
    new_conv_s = jnp.transpose(new_conv_s, (0, 2, 1, 3))
    return (y_p.reshape(batch, seq, D_MODEL), y_s.reshape(dec_batch, dec_seq, D_MODEL),
            new_pool_p, new_conv_p, new_pool_s, new_conv_s)
```

```python
import functools

import jax
import jax.numpy as jnp
from jax import lax
from jax.experimental import pallas as pl
from jax.experimental.pallas import tpu as pltpu

D_MODEL = 1024
N_META = 16
POOL_WIDTH = 512
CONV_WIDTH = 512
POOL_WINDOWS = (2, 4, 8, 16)
POOL_GROUP = 128
POOL_BUF = 15
CONV_KERNEL = 31
CONV_BUF = 30
IN_COLS = POOL_WIDTH + 2 * CONV_WIDTH
N_EXPERTS = 8
DEC_SEQ = 8
RMS_EPS = 1e-6
LN_EPS = 1e-5

POOL_HALO = 16
CONV_HALO = 32
CONV_SHIFT_PAD = CONV_HALO - 8
ROUTE_LANES = 128

ROW_TILE = 512
DEC_ROW_TILE = 256
MIX_CHUNK = 32
FFN_CHUNK = 512

MOE_ROW_TILE = 1024
MOE_SUB = 256
MOE_FF_TILE = 512
SEG_ALIGN = 8
SEG_CHUNK = 64
LOCAL_ROWS = -(-(2 * ROW_TILE + (SEG_ALIGN - 1) * N_EXPERTS) // 128) * 128

VMEM_LIMIT = 56 * 1024 * 1024

_BF16 = jnp.bfloat16
_F32 = jnp.float32


def _dot(a, b):
    return jnp.dot(a, b, preferred_element_type=_F32)


def _sigmoid(x):
    return 1.0 / (1.0 + jnp.exp(-x))


def _rmsnorm(x, g):
    return x * lax.rsqrt(jnp.mean(x * x, axis=-1, keepdims=True) + RMS_EPS) * g


def _mix_chunk(ep_ref, ec_ref, r0, rows, dw_ref, dwb_ref, lng_ref, lnb_ref, pos0, sh_ref=None):
    def conv_rows(start, c0):
        if sh_ref is None or start % 8 == 0:
            return ec_ref[start:start + rows, c0:c0 + 128]
        q, r = divmod(start, 8)
        return sh_ref[r - 1, 8 * q:8 * q + rows, c0:c0 + 128]

    pooled_cols = []
    for gi, win in enumerate(POOL_WINDOWS):
        c0 = gi * POOL_GROUP
        base = POOL_HALO + r0
        x0 = ep_ref[base:base + rows, c0:c0 + POOL_GROUP]
        s = x0
        for k in range(1, win):
            s = s + ep_ref[base - k:base - k + rows, c0:c0 + POOL_GROUP]
        if pos0 is None:
            pooled = s * (1.0 / win) - x0
        else:
            pos = pos0 + r0 + lax.broadcasted_iota(jnp.int32, (rows, POOL_GROUP), 0)
            cnt = jnp.minimum(pos + 1, win).astype(_F32)
            pooled = s / cnt - x0
        pooled_cols.append(pooled)
    pooled = jnp.concatenate(pooled_cols, axis=-1)

    conv_cols = []
    for cg in range(CONV_WIDTH // 128):
        c0 = cg * 128
        base = CONV_HALO - CONV_BUF + r0
        acc = conv_rows(base, c0) * dw_ref[0:1, c0:c0 + 128]
        for k in range(1, CONV_KERNEL):
            acc = acc + conv_rows(base + k, c0) * dw_ref[k:k + 1, c0:c0 + 128]
        conv_cols.append(acc + dwb_ref[:, c0:c0 + 128])
    y = jnp.concatenate(conv_cols, axis=-1)
    mu = jnp.mean(y, axis=-1, keepdims=True)
    d = y - mu
    var = jnp.mean(d * d, axis=-1, keepdims=True)
    yn = d * lax.rsqrt(var + LN_EPS) * lng_ref[...] + lnb_ref[...]
    return pooled, yn * _sigmoid(yn)


def _front_kernel(*refs, mode, ramp, tail, tile, n_t, skew):
    it = iter(refs)
    h_ref, ppast_ref, cpast_ref = next(it), next(it), next(it)
    g1_ref, win_ref, wpa_ref, wpb_ref, psc_ref = next(it), next(it), next(it), next(it), next(it)
    dw_ref, dwb_ref, lng_ref, lnb_ref, pw_ref, wout_ref, g2_ref = (next(it) for _ in range(7))
    if tail == "ffn":
        wg_ref, wu_ref, wd_ref = next(it), next(it), next(it)
    elif tail == "route":
        rw_ref = next(it)
    hout_ref = next(it)
    if mode == "seq":
        pst_ref, cst_ref = next(it), next(it)
    else:
        up_ref, v_ref = next(it), next(it)
    if tail == "route":
        hn2_ref, route_ref, cnt_ref = next(it), next(it), next(it)
    ep_ref, ec_ref, mixed_ref = next(it), next(it), next(it)
    if mode == "seq":
        sh_ref = next(it)
    if skew:
        h1_ref = next(it)

    def ffn_pieces(h_in):
        hb = _rmsnorm(h_in, g2_ref[...]).astype(_BF16)
        d_ff = wg_ref.shape[1]
        hout_ref[...] = h_in

        def piece(c0):
            cs = min(FFN_CHUNK, d_ff - c0)
            g = _dot(hb, wg_ref[:, c0:c0 + cs])
            a = (g * _sigmoid(g) * _dot(hb, wu_ref[:, c0:c0 + cs])).astype(_BF16)
            hout_ref[...] += _dot(a, wd_ref[c0:c0 + cs, :])

        return [functools.partial(piece, c0) for c0 in range(0, d_ff, FFN_CHUNK)]

    def ffn_tail(h_in):
        for piece in ffn_pieces(h_in):
            piece()

    if mode == "seq":
        if skew:
            step = pl.program_id(0)
            t = lax.rem(step, n_t)

            @pl.when(step == 0)
            def _():
                h1_ref[...] = jnp.zeros(h1_ref.shape, _F32)
        else:
            t = pl.program_id(1)

        @pl.when(t == 0)
        def _():
            ep_ref[0:POOL_HALO, :] = ppast_ref[0]
            ec_ref[0:CONV_HALO, :] = cpast_ref[0]

    pending = ffn_pieces(h1_ref[...]) if skew else []

    x = h_ref[...]
    hn = _rmsnorm(x, g1_ref[...]).astype(_BF16)
    u = _dot(hn, win_ref[...])
    up = u[:, :POOL_WIDTH]
    v = u[:, POOL_WIDTH:POOL_WIDTH + CONV_WIDTH] * _sigmoid(u[:, POOL_WIDTH + CONV_WIDTH:])
    if mode == "seq":
        ep_ref[POOL_HALO:, :] = up
        ec_ref[CONV_HALO:, :] = v
    else:
        for g in range(POOL_WIDTH // 128):
            ep_ref[g] = up[:, g * 128:(g + 1) * 128]
            ec_ref[g] = v[:, g * 128:(g + 1) * 128]

    if mode == "seq":
        for r in range(1, 8):
            sh_ref[r - 1] = ec_ref[r:r + tile + CONV_SHIFT_PAD, :]
        chunk = min(MIX_CHUNK, tile)
        n_chunks = tile // chunk
        n_pieces = len(pending)
        for ci in range(n_chunks):
            r0 = ci * chunk
            pos0 = t * tile if ramp else None
            pooled, act = _mix_chunk(ep_ref, ec_ref, r0, chunk, dw_ref, dwb_ref, lng_ref, lnb_ref, pos0, sh_ref)
            mixed_ref[r0:r0 + chunk, 0:POOL_WIDTH] = pooled
            mixed_ref[r0:r0 + chunk, POOL_WIDTH:] = act
            while len(pending) > n_pieces - (ci + 1) * n_pieces // n_chunks:
                pending.pop(0)()
        ptail = ep_ref[tile:tile + POOL_HALO, :]
        ctail = ec_ref[tile:tile + CONV_HALO, :]
        ep_ref[0:POOL_HALO, :] = ptail
        ec_ref[0:CONV_HALO, :] = ctail
        pst_ref[0] = ptail
        cst_ref[0] = ctail
    else:
        spt = tile // DEC_SEQ
        for t in range(DEC_SEQ):
            for g in range(POOL_WIDTH // 128):
                up_ref[t, :, g * 128:(g + 1) * 128] = ep_ref[g, pl.ds(t, spt, stride=DEC_SEQ), :]
                v_ref[t, :, g * 128:(g + 1) * 128] = ec_ref[g, pl.ds(t, spt, stride=DEC_SEQ), :]

        def pool_row(i, cols):
            return ppast_ref[i, :, cols] if i < POOL_BUF else up_ref[i - POOL_BUF, :, cols]

        def conv_row(i):
            return cpast_ref[i] if i < CONV_BUF else v_ref[i - CONV_BUF]

        for t in range(DEC_SEQ):
            pooled_cols = []
            for gi, win in enumerate(POOL_WINDOWS):
                cols = slice(gi * POOL_GROUP, (gi + 1) * POOL_GROUP)
                x0 = pool_row(POOL_BUF + t, cols)
                s = x0
                for k in range(1, win):
                    s = s + pool_row(POOL_BUF + t - k, cols)
                pooled_cols.append(s * (1.0 / win) - x0)
            y = conv_row(t) * dw_ref[0:1, :]
            for k in range(1, CONV_KERNEL):
                y = y + conv_row(t + k) * dw_ref[k:k + 1, :]
            y = y + dwb_ref[...]
            mu = jnp.mean(y, axis=-1, keepdims=True)
            d = y - mu
            var = jnp.mean(d * d, axis=-1, keepdims=True)
            yn = d * lax.rsqrt(var + LN_EPS) * lng_ref[...] + lnb_ref[...]
            act = yn * _sigmoid(yn)
            for g in range(POOL_WIDTH // 128):
                mixed_ref[g, pl.ds(t, spt, stride=DEC_SEQ), :] = pooled_cols[g]
                mixed_ref[POOL_WIDTH // 128 + g, pl.ds(t, spt, stride=DEC_SEQ), :] = act[:, g * 128:(g + 1) * 128]

    def mixed_cols(c0, c1):
        if mode == "seq":
            return mixed_ref[:, c0:c1].astype(_BF16)
        return jnp.concatenate([mixed_ref[g] for g in range(c0 // 128, c1 // 128)], axis=-1).astype(_BF16)

    pa = _dot(mixed_cols(0, 256), wpa_ref[...])
    pb = _dot(mixed_cols(256, 512), wpb_ref[...])
    pool_out = jnp.concatenate([pa, pb], axis=-1) * psc_ref[...]
    conv_out = _dot(mixed_cols(POOL_WIDTH, POOL_WIDTH + CONV_WIDTH), pw_ref[...])
    h1 = (x + _dot(pool_out.astype(_BF16), wout_ref[0:POOL_WIDTH, :])
          + _dot(conv_out.astype(_BF16), wout_ref[POOL_WIDTH:, :]))

    if tail == "none":
        hout_ref[...] = h1
    elif skew:
        h1_ref[...] = h1
    elif tail == "ffn":
        ffn_tail(h1)
    else:
        hn2 = _rmsnorm(h1, g2_ref[...])
        hout_ref[...] = h1
        hn2_ref[...] = hn2.astype(_BF16)
        logits = _dot(hn2.astype(_BF16), rw_ref[...])
        lane = lax.broadcasted_iota(jnp.int32, logits.shape, 1).astype(_F32)
        neg = jnp.float32(-jnp.inf)
        lg = jnp.where(lane < N_EXPERTS, logits, neg)
        m1 = jnp.max(lg, axis=-1, keepdims=True)
        i1 = jnp.min(jnp.where(lg == m1, lane, float(ROUTE_LANES)), axis=-1, keepdims=True)
        lg2 = jnp.where(lane == i1, neg, lg)
        m2 = jnp.max(lg2, axis=-1, keepdims=True)
        i2 = jnp.min(jnp.where(lg2 == m2, lane, float(ROUTE_LANES)), axis=-1, keepdims=True)
        e21 = jnp.exp(m2 - m1)
        gate1 = 1.0 / (1.0 + e21)
        gate2 = e21 * gate1
        route_ref[...] = jnp.where(lane == 0, i1, jnp.where(lane == 1, i2, jnp.where(
            lane == 2, gate1, jnp.where(lane == 3, gate2, 0.0))))
        chosen = (lane == i1).astype(_F32) + (lane == i2).astype(_F32)
        cnt_ref[...] = jnp.broadcast_to(jnp.sum(chosen, axis=0, keepdims=True), cnt_ref.shape)


def _const_spec(shape):
    nd = len(shape)
    return pl.BlockSpec(shape, lambda *_: (0,) * nd, pipeline_mode=pl.Buffered(1))


def _front_call(h, ppast, cpast, lw, *, mode, ramp, tail, tile, n_seq, n_t, skew=False, layer=None):
    rows = h.shape[0]
    weights = [lw["g1"], lw["w_in"], lw["wpa"], lw["wpb"], lw["pscale"], lw["dw"], lw["dwb"], lw["lng"],
               lw["lnb"], lw["pw"], lw["w_out"], lw["g2"]]
    if tail == "ffn":
        weights += [lw["wg"], lw["wu"], lw["wd"]]
    elif tail == "route":
        weights += [lw["rw"]]
    w_specs = [_const_spec(w.shape) for w in weights]

    if mode == "seq" and skew:
        assert tail == "ffn" and ppast.shape[0] == 1
        n_tiles = n_seq * n_t
        grid = (n_tiles + 1,)
        row_map = lambda s: (jnp.minimum(s, n_tiles - 1), 0)
        seq_map = lambda s: (jnp.minimum(s, n_tiles - 1) // n_t, 0, 0)
        past_specs = [pl.BlockSpec((1, POOL_HALO, POOL_WIDTH), lambda s: (0, 0, 0)),
                      pl.BlockSpec((1, CONV_HALO, CONV_WIDTH), lambda s: (0, 0, 0))]
        out_shape = [jax.ShapeDtypeStruct((rows, D_MODEL), _F32),
                     jax.ShapeDtypeStruct((n_seq, POOL_HALO, POOL_WIDTH), _F32),
                     jax.ShapeDtypeStruct((n_seq, CONV_HALO, CONV_WIDTH), _F32)]
        out_specs = [pl.BlockSpec((tile, D_MODEL), lambda s: (jnp.maximum(s - 1, 0), 0)),
                     pl.BlockSpec((1, POOL_HALO, POOL_WIDTH), seq_map),
                     pl.BlockSpec((1, CONV_HALO, CONV_WIDTH), seq_map)]
        semantics = ("arbitrary",)
    elif mode == "seq":
        grid = (n_seq, n_t)
        row_map = lambda b, t: (b * n_t + t, 0)
        shared = ppast.shape[0] == 1
        past_map = (lambda b, t: (0, 0, 0)) if shared else (lambda b, t: (b, 0, 0))
        past_specs = [pl.BlockSpec((1, POOL_HALO, POOL_WIDTH), past_map),
                      pl.BlockSpec((1, CONV_HALO, CONV_WIDTH), past_map)]
        out_shape = [jax.ShapeDtypeStruct((rows, D_MODEL), _F32),
                     jax.ShapeDtypeStruct((n_seq, POOL_HALO, POOL_WIDTH), _F32),
                     jax.ShapeDtypeStruct((n_seq, CONV_HALO, CONV_WIDTH), _F32)]
        out_specs = [pl.BlockSpec((tile, D_MODEL), row_map),
                     pl.BlockSpec((1, POOL_HALO, POOL_WIDTH), lambda b, t: (b, 0, 0)),
                     pl.BlockSpec((1, CONV_HALO, CONV_WIDTH), lambda b, t: (b, 0, 0))]
        semantics = ("arbitrary", "arbitrary")
    else:
        grid = (rows // tile,)
        row_map = lambda i: (i, 0)
        spt = tile // DEC_SEQ
        past_specs = [pl.BlockSpec((None, POOL_BUF, spt, POOL_WIDTH), lambda i: (layer, 0, i, 0)),
                      pl.BlockSpec((None, CONV_BUF, spt, CONV_WIDTH), lambda i: (layer, 0, i, 0))]
        out_shape = [jax.ShapeDtypeStruct((rows, D_MODEL), _F32),
                     jax.ShapeDtypeStruct((DEC_SEQ, rows // DEC_SEQ, POOL_WIDTH), _F32),
                     jax.ShapeDtypeStruct((DEC_SEQ, rows // DEC_SEQ, CONV_WIDTH), _F32)]
        out_specs = [pl.BlockSpec((tile, D_MODEL), row_map),
                     pl.BlockSpec((DEC_SEQ, spt, POOL_WIDTH), lambda i: (0, i, 0)),
                     pl.BlockSpec((DEC_SEQ, spt, CONV_WIDTH), lambda i: (0, i, 0))]
        semantics = ("arbitrary",)
    if tail == "route":
        out_shape += [jax.ShapeDtypeStruct((rows, D_MODEL), _BF16),
                      jax.ShapeDtypeStruct((rows, ROUTE_LANES), _F32),
                      jax.ShapeDtypeStruct((rows // tile * 8, ROUTE_LANES), _F32)]
        out_specs += [pl.BlockSpec((tile, D_MODEL), row_map), pl.BlockSpec((tile, ROUTE_LANES), row_map),
                      pl.BlockSpec((8, ROUTE_LANES), row_map)]

    if mode == "seq":
        scratch = [pltpu.VMEM((POOL_HALO + tile, POOL_WIDTH), _F32),
                   pltpu.VMEM((CONV_HALO + tile, CONV_WIDTH), _F32),
                   pltpu.VMEM((tile, POOL_WIDTH + CONV_WIDTH), _F32),
                   pltpu.VMEM((7, tile + CONV_SHIFT_PAD, CONV_WIDTH), _F32)]
    else:
        scratch = [pltpu.VMEM((POOL_WIDTH // 128, tile, 128), _F32),
                   pltpu.VMEM((CONV_WIDTH // 128, tile, 128), _F32),
                   pltpu.VMEM(((POOL_WIDTH + CONV_WIDTH) // 128, tile, 128), _F32)]
    if skew:
        scratch += [pltpu.VMEM((tile, D_MODEL), _F32)]

    body = functools.partial(_front_kernel, mode=mode, ramp=ramp, tail=tail, tile=tile, n_t=n_t, skew=skew)
    return pl.pallas_call(
        body,
        grid=grid,
        in_specs=[pl.BlockSpec((tile, D_MODEL), row_map)] + past_specs + w_specs,
        out_specs=out_specs,
        out_shape=out_shape,
        scratch_shapes=scratch,
        compiler_params=pltpu.CompilerParams(dimension_semantics=semantics, vmem_limit_bytes=VMEM_LIMIT),
        name=f"front_{mode}_{tail}" + ("_skew" if skew else ""),
    )(h, ppast, cpast, *weights)


def _segment_copies(c8_ref, lo_ref, segoff_ref, tidx, make_copy, act):
    for e in range(N_EXPERTS):
        k = tidx * N_EXPERTS + e
        lo_e = lo_ref[k]
        so_e = segoff_ref[k]
        n_big = c8_ref[k] // SEG_CHUNK
        rest = n_big * SEG_CHUNK

        def big(c, carry):
            act(make_copy(pl.multiple_of(lo_e + c * SEG_CHUNK, SEG_ALIGN),
                          pl.multiple_of(so_e + c * SEG_CHUNK, SEG_ALIGN), SEG_CHUNK))
            return carry

        def small(c, carry):
            act(make_copy(pl.multiple_of(lo_e + rest + c * SEG_ALIGN, SEG_ALIGN),
                          pl.multiple_of(so_e + rest + c * SEG_ALIGN, SEG_ALIGN), SEG_ALIGN))
            return carry

        lax.fori_loop(0, n_big, big, 0)
        lax.fori_loop(0, (c8_ref[k] - rest) // SEG_ALIGN, small, 0)


def _dispatch_kernel(segoff_ref, c8_ref, lo_ref, filloff_ref, filln_ref, hn_a_ref, route_a_ref, hn_b_ref,
                     route_b_ref, xs_hbm, lpos_ref, xl_ref, sem, zero_ref, fill_sem, *, tile, n_a):
    t = pl.program_id(0)
    n = pl.num_programs(0)
    slot = t % 2
    tt = t
    in_a = t < n_a
    hn = jnp.where(in_a, hn_a_ref[...], hn_b_ref[...])
    route = jnp.where(in_a, route_a_ref[...], route_b_ref[...])

    @pl.when(t == 0)
    def _():
        zero_ref[...] = jnp.zeros(zero_ref.shape, _F32)

        def fill(act):
            for r in range(N_EXPERTS + 1):
                off = filloff_ref[r]
                n_big = filln_ref[r] // (SEG_CHUNK // SEG_ALIGN)
                rest = n_big * SEG_CHUNK

                def big(c, carry):
                    dst = xs_hbm.at[pl.ds(pl.multiple_of(off + c * SEG_CHUNK, SEG_ALIGN), SEG_CHUNK)]
                    act(pltpu.make_async_copy(zero_ref, dst, fill_sem))
                    return carry

                def small(c, carry):
                    dst = xs_hbm.at[pl.ds(pl.multiple_of(off + rest + c * SEG_ALIGN, SEG_ALIGN), SEG_ALIGN)]
                    act(pltpu.make_async_copy(zero_ref.at[pl.ds(0, SEG_ALIGN)], dst, fill_sem))
                    return carry

                lax.fori_loop(0, n_big, big, 0)
                lax.fori_loop(0, filln_ref[r] - n_big * (SEG_CHUNK // SEG_ALIGN), small, 0)

        fill(lambda c: c.start())
        fill(lambda c: c.wait())

    lane = lax.broadcasted_iota(jnp.int32, (tile, ROUTE_LANES), 1).astype(_F32)
    oh0 = (lane == route[:, 0:1]).astype(_F32)
    oh1 = (lane == route[:, 1:2]).astype(_F32)
    earlier = (lax.broadcasted_iota(jnp.int32, (tile, tile), 1)
               < lax.broadcasted_iota(jnp.int32, (tile, tile), 0)).astype(_BF16)
    rank = _dot(earlier, (oh0 + oh1).astype(_BF16))
    lane1 = lax.broadcasted_iota(jnp.int32, (1, ROUTE_LANES), 1)
    lo_vec = jnp.zeros((1, ROUTE_LANES), _F32)
    for e in range(N_EXPERTS):
        lo_vec = jnp.where(lane1 == e, lo_ref[tt * N_EXPERTS + e].astype(_F32), lo_vec)
    posv = rank + lo_vec
    p0 = jnp.sum(oh0 * posv, axis=-1, keepdims=True)
    p1 = jnp.sum(oh1 * posv, axis=-1, keepdims=True)
    slab = jnp.where(lane == 0, p0, jnp.where(lane == 1, p1, 0.0))
    lpos_ref[...] = slab
    slab_t = slab.T
    rowi = lax.broadcasted_iota(jnp.int32, (LOCAL_ROWS, tile), 0).astype(_F32)
    place = jnp.where(jnp.logical_or(rowi == slab_t[0:1, :], rowi == slab_t[1:2, :]), 1.0, 0.0).astype(_BF16)
    xl_ref[slot] = _dot(place, hn.astype(_BF16))

    def copies(tidx, s):
        def make_copy(lo, so, rows):
            return pltpu.make_async_copy(xl_ref.at[s, pl.ds(lo, rows)], xs_hbm.at[pl.ds(so, rows)], sem.at[s])
        return functools.partial(_segment_copies, c8_ref, lo_ref, segoff_ref, tidx, make_copy)

    copies(tt, slot)(lambda c: c.start())

    @pl.when(t > 0)
    def _():
        copies(tt - 1, 1 - slot)(lambda c: c.wait())

    @pl.when(t == n - 1)
    def _():
        copies(tt, slot)(lambda c: c.wait())


def _dispatch_call(plan, hn_a, route_a, hn_b, route_b, *, tile):
    n_a = hn_a.shape[0] // tile
    n_b = hn_b.shape[0] // tile
    amap = lambda i, *_: (jnp.minimum(i, n_a - 1), 0)
    bmap = lambda i, *_: (jnp.maximum(i - n_a, 0), 0)
    tables = (plan["seg_off"], plan["c8"], plan["lo"], plan["fill_off"], plan["fill_n"])
    return pl.pallas_call(
        functools.partial(_dispatch_kernel, tile=tile, n_a=n_a),
        grid_spec=pltpu.PrefetchScalarGridSpec(
            num_scalar_prefetch=len(tables),
            grid=(n_a + n_b,),
            in_specs=[pl.BlockSpec((tile, D_MODEL), amap), pl.BlockSpec((tile, ROUTE_LANES), amap),
                      pl.BlockSpec((tile, D_MODEL), bmap), pl.BlockSpec((tile, ROUTE_LANES), bmap)],
            out_specs=[pl.BlockSpec(memory_space=pl.ANY), pl.BlockSpec((tile, ROUTE_LANES), lambda i, *_: (i, 0))],
            scratch_shapes=[pltpu.VMEM((2, LOCAL_ROWS, D_MODEL), _F32), pltpu.SemaphoreType.DMA((2,)),
                            pltpu.VMEM((SEG_CHUNK, D_MODEL), _F32), pltpu.SemaphoreType.DMA(())],
        ),
        out_shape=[jax.ShapeDtypeStruct((plan["n_tiles"] * MOE_ROW_TILE, D_MODEL), _F32),
                   jax.ShapeDtypeStruct(((n_a + n_b) * tile, ROUTE_LANES), _F32)],
        compiler_params=pltpu.CompilerParams(
            dimension_semantics=("arbitrary",), vmem_limit_bytes=VMEM_LIMIT, has_side_effects=True),
        name="moe_dispatch",
    )(*tables, hn_a, route_a, hn_b, route_b)


def _moe_kernel(te_ref, nv_ref, nu_ref, x_ref, wg_ref, wu_ref, wd_ref, o_ref, xb_ref):
    del te_ref, nu_ref
    i = pl.program_id(0)
    j = pl.program_id(1)
    nv = nv_ref[i]

    @pl.when(j == 0)
    def _():
        o_ref[...] = jnp.zeros(o_ref.shape, _F32)
        xb_ref[...] = x_ref[...].astype(_BF16)

    def expert_ffn(xs, wg, wu, wd):
        g = _dot(xs, wg)
        a = (g * _sigmoid(g) * _dot(xs, wu)).astype(_BF16)
        return _dot(a, wd)

    for r in range(MOE_SUB, MOE_ROW_TILE + 1, MOE_SUB):
        @pl.when(jnp.logical_and(nv > r - MOE_SUB, nv <= r))
        def _():
            o_ref[0:r, :] += expert_ffn(xb_ref[0:r, :], wg_ref[0].astype(_BF16), wu_ref[0].astype(_BF16),
                                        wd_ref[0].astype(_BF16))


def _moe_call(tile_expert, tile_valid, n_used, xs, wg, wu, wd):
    n_tiles = xs.shape[0] // MOE_ROW_TILE
    eff = wg.shape[2]
    n_j = eff // MOE_FF_TILE

    def row_map(i, j, te, nv, nu):
        return (jnp.maximum(jnp.minimum(i, nu[0] - 1), 0), 0)

    def col_of(i, j, nu):
        return jnp.where(i < nu[0], j, n_j - 1)

    return pl.pallas_call(
        _moe_kernel,
        grid_spec=pltpu.PrefetchScalarGridSpec(
            num_scalar_prefetch=3,
            grid=(n_tiles, n_j),
            in_specs=[
                pl.BlockSpec((MOE_ROW_TILE, D_MODEL), row_map),
                pl.BlockSpec((1, D_MODEL, MOE_FF_TILE), lambda i, j, te, nv, nu: (te[i], 0, col_of(i, j, nu))),
                pl.BlockSpec((1, D_MODEL, MOE_FF_TILE), lambda i, j, te, nv, nu: (te[i], 0, col_of(i, j, nu))),
                pl.BlockSpec((1, MOE_FF_TILE, D_MODEL), lambda i, j, te, nv, nu: (te[i], col_of(i, j, nu), 0)),
            ],
            out_specs=pl.BlockSpec((MOE_ROW_TILE, D_MODEL), lambda i, j, te, nv, nu: (i, 0)),
            scratch_shapes=[pltpu.VMEM((MOE_ROW_TILE, D_MODEL), _BF16)],
        ),
        out_shape=jax.ShapeDtypeStruct(xs.shape, _F32),
        compiler_params=pltpu.CompilerParams(
            dimension_semantics=("arbitrary", "arbitrary"), vmem_limit_bytes=VMEM_LIMIT),
        name="moe_grouped",
    )(tile_expert, tile_valid, n_used, xs, wg, wu, wd)


def _combine_kernel(segoff_ref, c8_ref, lo_ref, h_ref, route_ref, lpos_ref, fg_ref, ys_hbm, o_ref, yl_ref, sem,
                    *, tile, tile0):
    t = pl.program_id(0)
    n = pl.num_programs(0)
    slot = t % 2
    tt = tile0 + t

    def copies(tidx, s):
        def make_copy(lo, so, rows):
            return pltpu.make_async_copy(ys_hbm.at[pl.ds(so, rows)], yl_ref.at[s, pl.ds(lo, rows)], sem.at[s])
        return functools.partial(_segment_copies, c8_ref, lo_ref, segoff_ref, tidx, make_copy)

    @pl.when(t == 0)
    def _():
        yl_ref[...] = jnp.zeros(yl_ref.shape, _F32)
        copies(tt, slot)(lambda c: c.start())

    @pl.when(t + 1 < n)
    def _():
        copies(tt + 1, 1 - slot)(lambda c: c.start())

    copies(tt, slot)(lambda c: c.wait())
    last = tt * N_EXPERTS + N_EXPERTS - 1
    used = lo_ref[last] + c8_ref[last]
    rowi = lax.broadcasted_iota(jnp.int32, (LOCAL_ROWS, 1), 0)
    ylb = jnp.where(rowi < used, yl_ref[slot], 0.0).astype(_BF16)
    lanel = lax.broadcasted_iota(jnp.int32, (tile, LOCAL_ROWS), 1).astype(_F32)
    take = (jnp.where(lanel == lpos_ref[:, 0:1], route_ref[:, 2:3], 0.0)
            + jnp.where(lanel == lpos_ref[:, 1:2], route_ref[:, 3:4], 0.0)).astype(_BF16)
    o_ref[...] = _rmsnorm(h_ref[...] + _dot(take, ylb), fg_ref[...])


def _combine_call(plan, h, route, lpos, fg, ys, *, tile, tile0):
    rows = h.shape[0]
    smap = lambda i, *_: (i, 0)
    return pl.pallas_call(
        functools.partial(_combine_kernel, tile=tile, tile0=tile0),
        grid_spec=pltpu.PrefetchScalarGridSpec(
            num_scalar_prefetch=3,
            grid=(rows // tile,),
            in_specs=[
                pl.BlockSpec((tile, D_MODEL), smap),
                pl.BlockSpec((tile, ROUTE_LANES), smap),
                pl.BlockSpec((tile, ROUTE_LANES), lambda i, *_: (i + tile0, 0)),
                pl.BlockSpec((1, D_MODEL), lambda i, *_: (0, 0)),
                pl.BlockSpec(memory_space=pl.ANY),
            ],
            out_specs=pl.BlockSpec((tile, D_MODEL), smap),
            scratch_shapes=[pltpu.VMEM((2, LOCAL_ROWS, D_MODEL), _F32), pltpu.SemaphoreType.DMA((2,))],
        ),
        out_shape=jax.ShapeDtypeStruct((rows, D_MODEL), _F32),
        compiler_params=pltpu.CompilerParams(
            dimension_semantics=("arbitrary",), vmem_limit_bytes=VMEM_LIMIT),
        name="moe_combine",
    )(plan["seg_off"], plan["c8"], plan["lo"], h, route, lpos, fg, ys)


def _layer_weights(l, norm1_g, w_in, pool_w, pool_scale, conv_dw_w, conv_dw_b, conv_ln_g, conv_ln_b,
                   conv_pw_w, w_out, norm2_g):
    zero = jnp.zeros((POOL_GROUP, POOL_GROUP), _F32)
    pwl = pool_w[l]
    wpa = jnp.block([[pwl[0], zero], [zero, pwl[1]]])
    wpb = jnp.block([[pwl[2], zero], [zero, pwl[3]]])
    return {
        "g1": norm1_g[l][None, :], "w_in": w_in[l].astype(_BF16),
        "wpa": wpa.astype(_BF16), "wpb": wpb.astype(_BF16), "pscale": pool_scale[l][None, :],
        "dw": conv_dw_w[l], "dwb": conv_dw_b[l][None, :], "lng": conv_ln_g[l][None, :],
        "lnb": conv_ln_b[l][None, :], "pw": conv_pw_w[l].astype(_BF16), "w_out": w_out[l].astype(_BF16),
        "g2": norm2_g[l][None, :],
    }


def _routing_plan(cnt):
    n_tok_tiles = cnt.shape[0]
    n = n_tok_tiles * ROW_TILE
    c8 = ((cnt + SEG_ALIGN - 1) // SEG_ALIGN) * SEG_ALIGN
    lo = jnp.cumsum(c8, axis=1) - c8
    within = jnp.cumsum(c8, axis=0) - c8
    glen = jnp.sum(c8, axis=0)
    gpad = ((glen + MOE_ROW_TILE - 1) // MOE_ROW_TILE) * MOE_ROW_TILE
    ends = jnp.cumsum(gpad)
    goff = ends - gpad
    seg_off = goff[None, :] + within
    max_rows = 2 * n + (SEG_ALIGN - 1) * N_EXPERTS * n_tok_tiles
    n_tiles = -(-max_rows // MOE_ROW_TILE) + N_EXPERTS
    starts = jnp.arange(n_tiles, dtype=jnp.int32) * MOE_ROW_TILE
    te = jnp.sum((starts[:, None] >= ends[None, :]).astype(jnp.int32), axis=1)
    n_used = (ends[-1] // MOE_ROW_TILE).astype(jnp.int32)
    te_c = jnp.minimum(te, N_EXPERTS - 1)
    valid = jnp.clip(goff[te_c] + glen[te_c] - starts, 0, MOE_ROW_TILE)
    valid = jnp.where(te < N_EXPERTS, valid, 0).astype(jnp.int32)
    last_e = te_c[jnp.maximum(n_used - 1, 0)]
    te_c = jnp.where(te < N_EXPERTS, te_c, last_e).astype(jnp.int32)
    flat = lambda a: a.reshape(-1).astype(jnp.int32)
    total = jnp.full((1,), n_tiles * MOE_ROW_TILE, jnp.int32)
    fill_off = jnp.concatenate([goff + glen, ends[-1:]])
    fill_n = jnp.concatenate([gpad - glen, total - ends[-1:]]) // SEG_ALIGN
    return {"seg_off": flat(seg_off), "c8": flat(c8), "lo": flat(lo), "fill_off": flat(fill_off),
            "fill_n": flat(fill_n), "tile_expert": te_c, "tile_valid": valid, "n_used": n_used.reshape(1),
            "n_tiles": n_tiles}


def kernel(x_prompt, x_sample, state_pool, state_conv, meta_tokens, norm1_g, w_in, pool_w, pool_scale,
           conv_dw_w, conv_dw_b, conv_ln_g, conv_ln_b, conv_pw_w, w_out, norm2_g, ffn_w_gate, ffn_w_up,
           ffn_w_down, router_w, moe_w_gate, moe_w_up, moe_w_down, final_norm_g):
    batch, seq, _ = x_prompt.shape
    dec_batch, dec_seq, _ = x_sample.shape
    assert dec_seq == DEC_SEQ and seq % ROW_TILE == 0 and (dec_batch * dec_seq) % ROW_TILE == 0
    assert (dec_batch * dec_seq) % DEC_ROW_TILE == 0
    lws = [_layer_weights(l, norm1_g, w_in, pool_w, pool_scale, conv_dw_w, conv_dw_b, conv_ln_g, conv_ln_b,
                          conv_pw_w, w_out, norm2_g) for l in range(2)]
    lws[0].update(wg=ffn_w_gate[0].astype(_BF16), wu=ffn_w_up[0].astype(_BF16), wd=ffn_w_down[0].astype(_BF16))
    rw = jnp.zeros((D_MODEL, ROUTE_LANES), _F32).at[:, :N_EXPERTS].set(router_w[0])
    lws[1].update(rw=rw.astype(_BF16))

    n_p = batch * seq
    n_s = dec_batch * dec_seq
    n_t = seq // ROW_TILE
    hp = x_prompt.reshape(n_p, D_MODEL)
    hs = x_sample.reshape(n_s, D_MODEL)
    hm = meta_tokens.astype(_F32)
    zero_pp = jnp.zeros((1, POOL_HALO, POOL_WIDTH), _F32)
    zero_cp = jnp.zeros((1, CONV_HALO, CONV_WIDTH), _F32)
    pool_tm = jnp.transpose(state_pool, (0, 2, 1, 3))
    conv_tm = jnp.transpose(state_conv, (0, 2, 1, 3))

    seq_kw = dict(mode="seq", tile=ROW_TILE, n_seq=batch, n_t=n_t)
    meta_kw = dict(mode="seq", ramp=True, tile=N_META, n_seq=1, n_t=1)
    dec_kw = dict(mode="dec", ramp=False, tile=DEC_ROW_TILE, n_seq=0, n_t=0)

    hm, pst_m0, cst_m0 = _front_call(hm, zero_pp, zero_cp, lws[0], tail="ffn", **meta_kw)
    hp, pst_p0, cst_p0 = _front_call(hp, pst_m0, cst_m0, lws[0], ramp=False, tail="ffn", skew=True, **seq_kw)
    hs, up_s0, v_s0 = _front_call(hs, pool_tm, conv_tm, lws[0], tail="ffn", layer=0, **dec_kw)

    _, pst_m1, cst_m1 = _front_call(hm, zero_pp, zero_cp, lws[1], tail="none", **meta_kw)
    hp, pst_p1, cst_p1, hn_p, route_p, cnt_p = _front_call(hp, pst_m1, cst_m1, lws[1], ramp=False, tail="route", **seq_kw)
    hs, up_s1, v_s1, hn_s, route_s, cnt_s = _front_call(hs, pool_tm, conv_tm, lws[1], tail="route", layer=1, **dec_kw)

    per_tile = lambda c, tile: c[::8, :N_EXPERTS].reshape(-1, ROW_TILE // tile, N_EXPERTS).sum(axis=1)
    cnt = jnp.concatenate([per_tile(cnt_p, ROW_TILE), per_tile(cnt_s, DEC_ROW_TILE)], axis=0)
    plan = _routing_plan(cnt.astype(jnp.int32))
    s_tile0 = n_p // ROW_TILE

    xs, lpos = _dispatch_call(plan, hn_p, route_p, hn_s, route_s, tile=ROW_TILE)
    ys = _moe_call(plan["tile_expert"], plan["tile_valid"], plan["n_used"], xs,
                   moe_w_gate[0], moe_w_up[0], moe_w_down[0])

    fg = final_norm_g[None, :]
    y_p = _combine_call(plan, hp, route_p, lpos, fg, ys, tile=ROW_TILE, tile0=0)
    y_s = _combine_call(plan, hs, route_s, lpos, fg, ys, tile=ROW_TILE, tile0=s_tile0)

    new_pool_p = jnp.stack([pst_p0, pst_p1])[:, :, POOL_HALO - POOL_BUF:, :]
    new_conv_p = jnp.stack([cst_p0, cst_p1])[:, :, CONV_HALO - CONV_BUF:, :]
    new_pool_s = jnp.concatenate([pool_tm, jnp.stack([up_s0, up_s1])], axis=1)[:, -POOL_BUF:]
    new_conv_s = jnp.concatenate([conv_tm, jnp.stack([v_s0, v_s1])], axis=1)[:, -CONV_BUF:]
    new_pool_s = jnp.transpose(new_pool_s, (0, 2, 1, 3))
    new_conv_s = jnp.transpose(new_conv_s, (0, 2, 1, 3))
    return (y_p.reshape(batch, seq, D_MODEL), y_s.reshape(dec_batch, dec_seq, D_MODEL),
            new_pool_p, new_conv_p, new_pool_s, new_conv_s)
```

```python
import functools

import jax
import jax.numpy as jnp
from jax import lax
from jax.experimental import pallas as pl
from jax.experimental.pallas import tpu as pltpu

D_MODEL = 1024
N_META = 16
POOL_WIDTH = 512
CONV_WIDTH = 512
POOL_WINDOWS = (2, 4, 8, 16)
POOL_GROUP = 128
POOL_BUF = 15
CONV_KERNEL = 31
CONV_BUF = 30
IN_COLS = POOL_WIDTH + 2 * CONV_WIDTH
N_EXPERTS = 8
DEC_SEQ = 8
RMS_EPS = 1e-6
LN_EPS = 1e-5

LANES = 128
SUBLANES = 8
POOL_HALO = 16
CONV_HALO = 32
CONV_SHIFT_PAD = CONV_HALO - SUBLANES
ROUTE_LANES = LANES

ROW_TILE = 512
DEC_ROW_TILE = 256
MIX_CHUNK = 64
FFN_CHUNK = 512

MOE_ROW_TILE = 1024
MOE_SUB = 256
MOE_FF_TILE = 512
SEG_ALIGN = SUBLANES
SEG_CHUNK = 64
LOCAL_ROWS = -(-(2 * ROW_TILE + (SEG_ALIGN - 1) * N_EXPERTS) // LANES) * LANES

VMEM_LIMIT = 56 * 1024 * 1024

_BF16 = jnp.bfloat16
_F32 = jnp.float32


def _dot(a, b):
    return jnp.dot(a, b, preferred_element_type=_F32)


def _sigmoid(x):
    return 1.0 / (1.0 + jnp.exp(-x))


def _rmsnorm(x, g):
    return x * lax.rsqrt(jnp.mean(x * x, axis=-1, keepdims=True) + RMS_EPS) * g


def _mix_chunk(ep_ref, ec_ref, r0, rows, dw_ref, dwb_ref, lng_ref, lnb_ref, pos0, sh_ref=None):
    def conv_rows(start, c0):
        if sh_ref is None or start % SUBLANES == 0:
            return ec_ref[start:start + rows, c0:c0 + LANES]
        q, r = divmod(start, SUBLANES)
        return sh_ref[r - 1, SUBLANES * q:SUBLANES * q + rows, c0:c0 + LANES]

    pooled_cols = []
    for gi, win in enumerate(POOL_WINDOWS):
        c0 = gi * POOL_GROUP
        base = POOL_HALO + r0
        x0 = ep_ref[base:base + rows, c0:c0 + POOL_GROUP]
        s = x0
        for k in range(1, win):
            s = s + ep_ref[base - k:base - k + rows, c0:c0 + POOL_GROUP]
        if pos0 is None:
            pooled = s * (1.0 / win) - x0
        else:
            pos = pos0 + r0 + lax.broadcasted_iota(jnp.int32, (rows, POOL_GROUP), 0)
            cnt = jnp.minimum(pos + 1, win).astype(_F32)
            pooled = s / cnt - x0
        pooled_cols.append(pooled)
    pooled = jnp.concatenate(pooled_cols, axis=-1)

    conv_cols = []
    for cg in range(CONV_WIDTH // LANES):
        c0 = cg * LANES
        base = CONV_HALO - CONV_BUF + r0
        acc = conv_rows(base, c0) * dw_ref[0:1, c0:c0 + LANES]
        for k in range(1, CONV_KERNEL):
            acc = acc + conv_rows(base + k, c0) * dw_ref[k:k + 1, c0:c0 + LANES]
        conv_cols.append(acc + dwb_ref[:, c0:c0 + LANES])
    y = jnp.concatenate(conv_cols, axis=-1)
    mu = jnp.mean(y, axis=-1, keepdims=True)
    d = y - mu
    var = jnp.mean(d * d, axis=-1, keepdims=True)
    yn = d * lax.rsqrt(var + LN_EPS) * lng_ref[...] + lnb_ref[...]
    return pooled, yn * _sigmoid(yn)


def _front_kernel(*refs, mode, ramp, tail, tile, n_t, skew):
    it = iter(refs)
    h_ref, ppast_ref, cpast_ref = next(it), next(it), next(it)
    g1_ref, win_ref, wpa_ref, wpb_ref, psc_ref = next(it), next(it), next(it), next(it), next(it)
    dw_ref, dwb_ref, lng_ref, lnb_ref, pw_ref, wout_ref, g2_ref = (next(it) for _ in range(7))
    if tail == "ffn":
        wg_ref, wu_ref, wd_ref = next(it), next(it), next(it)
    elif tail == "route":
        rw_ref = next(it)
    hout_ref = next(it)
    if mode == "seq":
        pst_ref, cst_ref = next(it), next(it)
    else:
        up_ref, v_ref = next(it), next(it)
    if tail == "route":
        hn2_ref, route_ref, cnt_ref = next(it), next(it), next(it)
    ep_ref, ec_ref, mixed_ref = next(it), next(it), next(it)
    if mode == "seq":
        sh_ref = next(it)
    if skew:
        h1_ref = next(it)

    def ffn_pieces(h_in):
        hb = _rmsnorm(h_in, g2_ref[...]).astype(_BF16)
        d_ff = wg_ref.shape[1]
        hout_ref[...] = h_in

        def piece(c0):
            cs = min(FFN_CHUNK, d_ff - c0)
            g = _dot(hb, wg_ref[:, c0:c0 + cs])
            a = (g * _sigmoid(g) * _dot(hb, wu_ref[:, c0:c0 + cs])).astype(_BF16)
            hout_ref[...] += _dot(a, wd_ref[c0:c0 + cs, :])

        return [functools.partial(piece, c0) for c0 in range(0, d_ff, FFN_CHUNK)]

    def ffn_tail(h_in):
        for piece in ffn_pieces(h_in):
            piece()

    if mode == "seq":
        if skew:
            step = pl.program_id(0)
            t = lax.rem(step, n_t)

            @pl.when(step == 0)
            def _():
                h1_ref[...] = jnp.zeros(h1_ref.shape, _F32)
        else:
            t = pl.program_id(1)

        @pl.when(t == 0)
        def _():
            ep_ref[0:POOL_HALO, :] = ppast_ref[0]
            ec_ref[0:CONV_HALO, :] = cpast_ref[0]

    pending = ffn_pieces(h1_ref[...]) if skew else []

    x = h_ref[...]
    hn = _rmsnorm(x, g1_ref[...]).astype(_BF16)
    u = _dot(hn, win_ref[...])
    up = u[:, :POOL_WIDTH]
    v = u[:, POOL_WIDTH:POOL_WIDTH + CONV_WIDTH] * _sigmoid(u[:, POOL_WIDTH + CONV_WIDTH:])
    if mode == "seq":
        ep_ref[POOL_HALO:, :] = up
        ec_ref[CONV_HALO:, :] = v
    else:
        for g in range(POOL_WIDTH // LANES):
            ep_ref[g] = up[:, g * LANES:(g + 1) * LANES]
            ec_ref[g] = v[:, g * LANES:(g + 1) * LANES]

    if mode == "seq":
        for r in range(1, SUBLANES):
            sh_ref[r - 1] = ec_ref[r:r + tile + CONV_SHIFT_PAD, :]
        chunk = min(MIX_CHUNK, tile)
        n_chunks = tile // chunk
        n_pieces = len(pending)
        for ci in range(n_chunks):
            r0 = ci * chunk
            pos0 = t * tile if ramp else None
            pooled, act = _mix_chunk(ep_ref, ec_ref, r0, chunk, dw_ref, dwb_ref, lng_ref, lnb_ref, pos0, sh_ref)
            mixed_ref[r0:r0 + chunk, 0:POOL_WIDTH] = pooled
            mixed_ref[r0:r0 + chunk, POOL_WIDTH:] = act
            while len(pending) > n_pieces - (ci + 1) * n_pieces // n_chunks:
                pending.pop(0)()
        ptail = ep_ref[tile:tile + POOL_HALO, :]
        ctail = ec_ref[tile:tile + CONV_HALO, :]
        ep_ref[0:POOL_HALO, :] = ptail
        ec_ref[0:CONV_HALO, :] = ctail
        pst_ref[0] = ptail
        cst_ref[0] = ctail
    else:
        spt = tile // DEC_SEQ
        for t in range(DEC_SEQ):
            for g in range(POOL_WIDTH // LANES):
                up_ref[t, :, g * LANES:(g + 1) * LANES] = ep_ref[g, pl.ds(t, spt, stride=DEC_SEQ), :]
                v_ref[t, :, g * LANES:(g + 1) * LANES] = ec_ref[g, pl.ds(t, spt, stride=DEC_SEQ), :]

        def pool_row(i, cols):
            return ppast_ref[i, :, cols] if i < POOL_BUF else up_ref[i - POOL_BUF, :, cols]

        def conv_row(i):
            return cpast_ref[i] if i < CONV_BUF else v_ref[i - CONV_BUF]

        for t in range(DEC_SEQ):
            pooled_cols = []
            for gi, win in enumerate(POOL_WINDOWS):
                cols = slice(gi * POOL_GROUP, (gi + 1) * POOL_GROUP)
                x0 = pool_row(POOL_BUF + t, cols)
                s = x0
                for k in range(1, win):
                    s = s + pool_row(POOL_BUF + t - k, cols)
                pooled_cols.append(s * (1.0 / win) - x0)
            y = conv_row(t) * dw_ref[0:1, :]
            for k in range(1, CONV_KERNEL):
                y = y + conv_row(t + k) * dw_ref[k:k + 1, :]
            y = y + dwb_ref[...]
            mu = jnp.mean(y, axis=-1, keepdims=True)
            d = y - mu
            var = jnp.mean(d * d, axis=-1, keepdims=True)
            yn = d * lax.rsqrt(var + LN_EPS) * lng_ref[...] + lnb_ref[...]
            act = yn * _sigmoid(yn)
            for g in range(POOL_WIDTH // LANES):
                mixed_ref[g, pl.ds(t, spt, stride=DEC_SEQ), :] = pooled_cols[g]
                mixed_ref[POOL_WIDTH // LANES + g, pl.ds(t, spt, stride=DEC_SEQ), :] = act[:, g * LANES:(g + 1) * LANES]

    def mixed_cols(c0, c1):
        if mode == "seq":
            return mixed_ref[:, c0:c1].astype(_BF16)
        return jnp.concatenate([mixed_ref[g] for g in range(c0 // LANES, c1 // LANES)], axis=-1).astype(_BF16)

    pa = _dot(mixed_cols(0, 2 * POOL_GROUP), wpa_ref[...])
    pb = _dot(mixed_cols(2 * POOL_GROUP, POOL_WIDTH), wpb_ref[...])
    pool_out = jnp.concatenate([pa, pb], axis=-1) * psc_ref[...]
    conv_out = _dot(mixed_cols(POOL_WIDTH, POOL_WIDTH + CONV_WIDTH), pw_ref[...])
    h1 = (x + _dot(pool_out.astype(_BF16), wout_ref[0:POOL_WIDTH, :])
          + _dot(conv_out.astype(_BF16), wout_ref[POOL_WIDTH:, :]))

    if tail == "none":
        hout_ref[...] = h1
    elif skew:
        h1_ref[...] = h1
    elif tail == "ffn":
        ffn_tail(h1)
    else:
        hn2 = _rmsnorm(h1, g2_ref[...])
        hout_ref[...] = h1
        hn2_ref[...] = hn2.astype(_BF16)
        logits = _dot(hn2.astype(_BF16), rw_ref[...])
        lane = lax.broadcasted_iota(jnp.int32, logits.shape, 1).astype(_F32)
        neg = jnp.float32(-jnp.inf)
        lg = jnp.where(lane < N_EXPERTS, logits, neg)
        m1 = jnp.max(lg, axis=-1, keepdims=True)
        i1 = jnp.min(jnp.where(lg == m1, lane, float(ROUTE_LANES)), axis=-1, keepdims=True)
        lg2 = jnp.where(lane == i1, neg, lg)
        m2 = jnp.max(lg2, axis=-1, keepdims=True)
        i2 = jnp.min(jnp.where(lg2 == m2, lane, float(ROUTE_LANES)), axis=-1, keepdims=True)
        e21 = jnp.exp(m2 - m1)
        gate1 = 1.0 / (1.0 + e21)
        gate2 = e21 * gate1
        route_ref[...] = jnp.where(lane == 0, i1, jnp.where(lane == 1, i2, jnp.where(
            lane == 2, gate1, jnp.where(lane == 3, gate2, 0.0))))
        chosen = (lane == i1).astype(_F32) + (lane == i2).astype(_F32)
        cnt_ref[...] = jnp.broadcast_to(jnp.sum(chosen, axis=0, keepdims=True), cnt_ref.shape)


def _const_spec(shape):
    nd = len(shape)
    return pl.BlockSpec(shape, lambda *_: (0,) * nd, pipeline_mode=pl.Buffered(1))


def _front_call(h, ppast, cpast, lw, *, mode, ramp, tail, tile, n_seq, n_t, skew=False, layer=None):
    rows = h.shape[0]
    weights = [lw["g1"], lw["w_in"], lw["wpa"], lw["wpb"], lw["pscale"], lw["dw"], lw["dwb"], lw["lng"],
               lw["lnb"], lw["pw"], lw["w_out"], lw["g2"]]
    if tail == "ffn":
        weights += [lw["wg"], lw["wu"], lw["wd"]]
    elif tail == "route":
        weights += [lw["rw"]]
    w_specs = [_const_spec(w.shape) for w in weights]

    if mode == "seq" and skew:
        assert tail == "ffn" and ppast.shape[0] == 1
        n_tiles = n_seq * n_t
        grid = (n_tiles + 1,)
        row_map = lambda s: (jnp.minimum(s, n_tiles - 1), 0)
        seq_map = lambda s: (jnp.minimum(s, n_tiles - 1) // n_t, 0, 0)
        past_specs = [pl.BlockSpec((1, POOL_HALO, POOL_WIDTH), lambda s: (0, 0, 0)),
                      pl.BlockSpec((1, CONV_HALO, CONV_WIDTH), lambda s: (0, 0, 0))]
        out_shape = [jax.ShapeDtypeStruct((rows, D_MODEL), _F32),
                     jax.ShapeDtypeStruct((n_seq, POOL_HALO, POOL_WIDTH), _F32),
                     jax.ShapeDtypeStruct((n_seq, CONV_HALO, CONV_WIDTH), _F32)]
        out_specs = [pl.BlockSpec((tile, D_MODEL), lambda s: (jnp.maximum(s - 1, 0), 0)),
                     pl.BlockSpec((1, POOL_HALO, POOL_WIDTH), seq_map),
                     pl.BlockSpec((1, CONV_HALO, CONV_WIDTH), seq_map)]
        semantics = ("arbitrary",)
    elif mode == "seq":
        grid = (n_seq, n_t)
        row_map = lambda b, t: (b * n_t + t, 0)
        shared = ppast.shape[0] == 1
        past_map = (lambda b, t: (0, 0, 0)) if shared else (lambda b, t: (b, 0, 0))
        past_specs = [pl.BlockSpec((1, POOL_HALO, POOL_WIDTH), past_map),
                      pl.BlockSpec((1, CONV_HALO, CONV_WIDTH), past_map)]
        out_shape = [jax.ShapeDtypeStruct((rows, D_MODEL), _F32),
                     jax.ShapeDtypeStruct((n_seq, POOL_HALO, POOL_WIDTH), _F32),
                     jax.ShapeDtypeStruct((n_seq, CONV_HALO, CONV_WIDTH), _F32)]
        out_specs = [pl.BlockSpec((tile, D_MODEL), row_map),
                     pl.BlockSpec((1, POOL_HALO, POOL_WIDTH), lambda b, t: (b, 0, 0)),
                     pl.BlockSpec((1, CONV_HALO, CONV_WIDTH), lambda b, t: (b, 0, 0))]
        semantics = ("arbitrary", "arbitrary")
    else:
        grid = (rows // tile,)
        row_map = lambda i: (i, 0)
        spt = tile // DEC_SEQ
        past_specs = [pl.BlockSpec((None, POOL_BUF, spt, POOL_WIDTH), lambda i: (layer, 0, i, 0)),
                      pl.BlockSpec((None, CONV_BUF, spt, CONV_WIDTH), lambda i: (layer, 0, i, 0))]
        out_shape = [jax.ShapeDtypeStruct((rows, D_MODEL), _F32),
                     jax.ShapeDtypeStruct((DEC_SEQ, rows // DEC_SEQ, POOL_WIDTH), _F32),
                     jax.ShapeDtypeStruct((DEC_SEQ, rows // DEC_SEQ, CONV_WIDTH), _F32)]
        out_specs = [pl.BlockSpec((tile, D_MODEL), row_map),
                     pl.BlockSpec((DEC_SEQ, spt, POOL_WIDTH), lambda i: (0, i, 0)),
                     pl.BlockSpec((DEC_SEQ, spt, CONV_WIDTH), lambda i: (0, i, 0))]
        semantics = ("arbitrary",)
    if tail == "route":
        out_shape += [jax.ShapeDtypeStruct((rows, D_MODEL), _BF16),
                      jax.ShapeDtypeStruct((rows, ROUTE_LANES), _F32),
                      jax.ShapeDtypeStruct((rows // tile * SUBLANES, ROUTE_LANES), _F32)]
        out_specs += [pl.BlockSpec((tile, D_MODEL), row_map), pl.BlockSpec((tile, ROUTE_LANES), row_map),
                      pl.BlockSpec((SUBLANES, ROUTE_LANES), row_map)]

    if mode == "seq":
        scratch = [pltpu.VMEM((POOL_HALO + tile, POOL_WIDTH), _F32),
                   pltpu.VMEM((CONV_HALO + tile, CONV_WIDTH), _F32),
                   pltpu.VMEM((tile, POOL_WIDTH + CONV_WIDTH), _F32),
                   pltpu.VMEM((SUBLANES - 1, tile + CONV_SHIFT_PAD, CONV_WIDTH), _F32)]
    else:
        scratch = [pltpu.VMEM((POOL_WIDTH // LANES, tile, LANES), _F32),
                   pltpu.VMEM((CONV_WIDTH // LANES, tile, LANES), _F32),
                   pltpu.VMEM(((POOL_WIDTH + CONV_WIDTH) // LANES, tile, LANES), _F32)]
    if skew:
        scratch += [pltpu.VMEM((tile, D_MODEL), _F32)]

    body = functools.partial(_front_kernel, mode=mode, ramp=ramp, tail=tail, tile=tile, n_t=n_t, skew=skew)
    return pl.pallas_call(
        body,
        grid=grid,
        in_specs=[pl.BlockSpec((tile, D_MODEL), row_map)] + past_specs + w_specs,
        out_specs=out_specs,
        out_shape=out_shape,
        scratch_shapes=scratch,
        compiler_params=pltpu.CompilerParams(dimension_semantics=semantics, vmem_limit_bytes=VMEM_LIMIT),
        name=f"front_{mode}_{tail}" + ("_skew" if skew else ""),
    )(h, ppast, cpast, *weights)


def _segment_copies(c8_ref, lo_ref, segoff_ref, tidx, make_copy, act):
    for e in range(N_EXPERTS):
        k = tidx * N_EXPERTS + e
        lo_e = lo_ref[k]
        so_e = segoff_ref[k]
        n_big = c8_ref[k] // SEG_CHUNK
        rest = n_big * SEG_CHUNK

        def big(c, carry):
            act(make_copy(pl.multiple_of(lo_e + c * SEG_CHUNK, SEG_ALIGN),
                          pl.multiple_of(so_e + c * SEG_CHUNK, SEG_ALIGN), SEG_CHUNK))
            return carry

        def small(c, carry):
            act(make_copy(pl.multiple_of(lo_e + rest + c * SEG_ALIGN, SEG_ALIGN),
                          pl.multiple_of(so_e + rest + c * SEG_ALIGN, SEG_ALIGN), SEG_ALIGN))
            return carry

        lax.fori_loop(0, n_big, big, 0)
        lax.fori_loop(0, (c8_ref[k] - rest) // SEG_ALIGN, small, 0)


def _dispatch_kernel(segoff_ref, c8_ref, lo_ref, filloff_ref, filln_ref, hn_a_ref, route_a_ref, hn_b_ref,
                     route_b_ref, xs_hbm, lpos_ref, xl_ref, sem, zero_ref, fill_sem, *, tile, n_a):
    t = pl.program_id(0)
    n = pl.num_programs(0)
    slot = t % 2
    tt = t
    in_a = t < n_a
    hn = jnp.where(in_a, hn_a_ref[...], hn_b_ref[...])
    route = jnp.where(in_a, route_a_ref[...], route_b_ref[...])

    @pl.when(t == 0)
    def _():
        zero_ref[...] = jnp.zeros(zero_ref.shape, _F32)

        def fill(act):
            for r in range(N_EXPERTS + 1):
                off = filloff_ref[r]
                n_big = filln_ref[r] // (SEG_CHUNK // SEG_ALIGN)
                rest = n_big * SEG_CHUNK

                def big(c, carry):
                    dst = xs_hbm.at[pl.ds(pl.multiple_of(off + c * SEG_CHUNK, SEG_ALIGN), SEG_CHUNK)]
                    act(pltpu.make_async_copy(zero_ref, dst, fill_sem))
                    return carry

                def small(c, carry):
                    dst = xs_hbm.at[pl.ds(pl.multiple_of(off + rest + c * SEG_ALIGN, SEG_ALIGN), SEG_ALIGN)]
                    act(pltpu.make_async_copy(zero_ref.at[pl.ds(0, SEG_ALIGN)], dst, fill_sem))
                    return carry

                lax.fori_loop(0, n_big, big, 0)
                lax.fori_loop(0, filln_ref[r] - n_big * (SEG_CHUNK // SEG_ALIGN), small, 0)

        fill(lambda c: c.start())
        fill(lambda c: c.wait())

    lane = lax.broadcasted_iota(jnp.int32, (tile, ROUTE_LANES), 1).astype(_F32)
    oh0 = (lane == route[:, 0:1]).astype(_F32)
    oh1 = (lane == route[:, 1:2]).astype(_F32)
    earlier = (lax.broadcasted_iota(jnp.int32, (tile, tile), 1)
               < lax.broadcasted_iota(jnp.int32, (tile, tile), 0)).astype(_BF16)
    rank = _dot(earlier, (oh0 + oh1).astype(_BF16))
    lane1 = lax.broadcasted_iota(jnp.int32, (1, ROUTE_LANES), 1)
    lo_vec = jnp.zeros((1, ROUTE_LANES), _F32)
    for e in range(N_EXPERTS):
        lo_vec = jnp.where(lane1 == e, lo_ref[tt * N_EXPERTS + e].astype(_F32), lo_vec)
    posv = rank + lo_vec
    p0 = jnp.sum(oh0 * posv, axis=-1, keepdims=True)
    p1 = jnp.sum(oh1 * posv, axis=-1, keepdims=True)
    slab = jnp.where(lane == 0, p0, jnp.where(lane == 1, p1, 0.0))
    lpos_ref[...] = slab
    slab_t = slab.T
    rowi = lax.broadcasted_iota(jnp.int32, (LOCAL_ROWS, tile), 0).astype(_F32)
    place = jnp.where(jnp.logical_or(rowi == slab_t[0:1, :], rowi == slab_t[1:2, :]), 1.0, 0.0).astype(_BF16)
    xl_ref[slot] = _dot(place, hn.astype(_BF16))

    def copies(tidx, s):
        def make_copy(lo, so, rows):
            return pltpu.make_async_copy(xl_ref.at[s, pl.ds(lo, rows)], xs_hbm.at[pl.ds(so, rows)], sem.at[s])
        return functools.partial(_segment_copies, c8_ref, lo_ref, segoff_ref, tidx, make_copy)

    copies(tt, slot)(lambda c: c.start())

    @pl.when(t > 0)
    def _():
        copies(tt - 1, 1 - slot)(lambda c: c.wait())

    @pl.when(t == n - 1)
    def _():
        copies(tt, slot)(lambda c: c.wait())


def _dispatch_call(plan, hn_a, route_a, hn_b, route_b, *, tile):
    n_a = hn_a.shape[0] // tile
    n_b = hn_b.shape[0] // tile
    amap = lambda i, *_: (jnp.minimum(i, n_a - 1), 0)
    bmap = lambda i, *_: (jnp.maximum(i - n_a, 0), 0)
    tables = (plan["seg_off"], plan["c8"], plan["lo"], plan["fill_off"], plan["fill_n"])
    return pl.pallas_call(
        functools.partial(_dispatch_kernel, tile=tile, n_a=n_a),
        grid_spec=pltpu.PrefetchScalarGridSpec(
            num_scalar_prefetch=len(tables),
            grid=(n_a + n_b,),
            in_specs=[pl.BlockSpec((tile, D_MODEL), amap), pl.BlockSpec((tile, ROUTE_LANES), amap),
                      pl.BlockSpec((tile, D_MODEL), bmap), pl.BlockSpec((tile, ROUTE_LANES), bmap)],
            out_specs=[pl.BlockSpec(memory_space=pl.ANY), pl.BlockSpec((tile, ROUTE_LANES), lambda i, *_: (i, 0))],
            scratch_shapes=[pltpu.VMEM((2, LOCAL_ROWS, D_MODEL), _F32), pltpu.SemaphoreType.DMA((2,)),
                            pltpu.VMEM((SEG_CHUNK, D_MODEL), _F32), pltpu.SemaphoreType.DMA(())],
        ),
        out_shape=[jax.ShapeDtypeStruct((plan["n_tiles"] * MOE_ROW_TILE, D_MODEL), _F32),
                   jax.ShapeDtypeStruct(((n_a + n_b) * tile, ROUTE_LANES), _F32)],
        compiler_params=pltpu.CompilerParams(
            dimension_semantics=("arbitrary",), vmem_limit_bytes=VMEM_LIMIT, has_side_effects=True),
        name="moe_dispatch",
    )(*tables, hn_a, route_a, hn_b, route_b)


def _moe_kernel(te_ref, nv_ref, nu_ref, x_ref, wg_ref, wu_ref, wd_ref, o_ref, xb_ref):
    del te_ref, nu_ref
    i = pl.program_id(0)
    j = pl.program_id(1)
    nv = nv_ref[i]

    @pl.when(j == 0)
    def _():
        o_ref[...] = jnp.zeros(o_ref.shape, _F32)
        xb_ref[...] = x_ref[...].astype(_BF16)

    def expert_ffn(xs, wg, wu, wd):
        g = _dot(xs, wg)
        a = (g * _sigmoid(g) * _dot(xs, wu)).astype(_BF16)
        return _dot(a, wd)

    for r in range(MOE_SUB, MOE_ROW_TILE + 1, MOE_SUB):
        @pl.when(jnp.logical_and(nv > r - MOE_SUB, nv <= r))
        def _():
            o_ref[0:r, :] += expert_ffn(xb_ref[0:r, :], wg_ref[0].astype(_BF16), wu_ref[0].astype(_BF16),
                                        wd_ref[0].astype(_BF16))


def _moe_call(tile_expert, tile_valid, n_used, xs, wg, wu, wd):
    n_tiles = xs.shape[0] // MOE_ROW_TILE
    eff = wg.shape[2]
    n_j = eff // MOE_FF_TILE

    def row_map(i, j, te, nv, nu):
        return (jnp.maximum(jnp.minimum(i, nu[0] - 1), 0), 0)

    def col_of(i, j, nu):
        return jnp.where(i < nu[0], j, n_j - 1)

    return pl.pallas_call(
        _moe_kernel,
        grid_spec=pltpu.PrefetchScalarGridSpec(
            num_scalar_prefetch=3,
            grid=(n_tiles, n_j),
            in_specs=[
                pl.BlockSpec((MOE_ROW_TILE, D_MODEL), row_map),
                pl.BlockSpec((1, D_MODEL, MOE_FF_TILE), lambda i, j, te, nv, nu: (te[i], 0, col_of(i, j, nu))),
                pl.BlockSpec((1, D_MODEL, MOE_FF_TILE), lambda i, j, te, nv, nu: (te[i], 0, col_of(i, j, nu))),
                pl.BlockSpec((1, MOE_FF_TILE, D_MODEL), lambda i, j, te, nv, nu: (te[i], col_of(i, j, nu), 0)),
            ],
            out_specs=pl.BlockSpec((MOE_ROW_TILE, D_MODEL), lambda i, j, te, nv, nu: (i, 0)),
            scratch_shapes=[pltpu.VMEM((MOE_ROW_TILE, D_MODEL), _BF16)],
        ),
        out_shape=jax.ShapeDtypeStruct(xs.shape, _F32),
        compiler_params=pltpu.CompilerParams(
            dimension_semantics=("arbitrary", "arbitrary"), vmem_limit_bytes=VMEM_LIMIT),
        name="moe_grouped",
    )(tile_expert, tile_valid, n_used, xs, wg, wu, wd)


def _combine_kernel(segoff_ref, c8_ref, lo_ref, h_ref, route_ref, lpos_ref, fg_ref, ys_hbm, o_ref, yl_ref, sem,
                    *, tile, tile0):
    t = pl.program_id(0)
    n = pl.num_programs(0)
    slot = t % 2
    tt = tile0 + t

    def copies(tidx, s):
        def make_copy(lo, so, rows):
            return pltpu.make_async_copy(ys_hbm.at[pl.ds(so, rows)], yl_ref.at[s, pl.ds(lo, rows)], sem.at[s])
        return functools.partial(_segment_copies, c8_ref, lo_ref, segoff_ref, tidx, make_copy)

    @pl.when(t == 0)
    def _():
        yl_ref[...] = jnp.zeros(yl_ref.shape, _F32)
        copies(tt, slot)(lambda c: c.start())

    @pl.when(t + 1 < n)
    def _():
        copies(tt + 1, 1 - slot)(lambda c: c.start())

    copies(tt, slot)(lambda c: c.wait())
    last = tt * N_EXPERTS + N_EXPERTS - 1
    used = lo_ref[last] + c8_ref[last]
    rowi = lax.broadcasted_iota(jnp.int32, (LOCAL_ROWS, 1), 0)
    ylb = jnp.where(rowi < used, yl_ref[slot], 0.0).astype(_BF16)
    lanel = lax.broadcasted_iota(jnp.int32, (tile, LOCAL_ROWS), 1).astype(_F32)
    take = (jnp.where(lanel == lpos_ref[:, 0:1], route_ref[:, 2:3], 0.0)
            + jnp.where(lanel == lpos_ref[:, 1:2], route_ref[:, 3:4], 0.0)).astype(_BF16)
    o_ref[...] = _rmsnorm(h_ref[...] + _dot(take, ylb), fg_ref[...])


def _combine_call(plan, h, route, lpos, fg, ys, *, tile, tile0):
    rows = h.shape[0]
    smap = lambda i, *_: (i, 0)
    return pl.pallas_call(
        functools.partial(_combine_kernel, tile=tile, tile0=tile0),
        grid_spec=pltpu.PrefetchScalarGridSpec(
            num_scalar_prefetch=3,
            grid=(rows // tile,),
            in_specs=[
                pl.BlockSpec((tile, D_MODEL), smap),
                pl.BlockSpec((tile, ROUTE_LANES), smap),
                pl.BlockSpec((tile, ROUTE_LANES), lambda i, *_: (i + tile0, 0)),
                pl.BlockSpec((1, D_MODEL), lambda i, *_: (0, 0)),
                pl.BlockSpec(memory_space=pl.ANY),
            ],
            out_specs=pl.BlockSpec((tile, D_MODEL), smap),
            scratch_shapes=[pltpu.VMEM((2, LOCAL_ROWS, D_MODEL), _F32), pltpu.SemaphoreType.DMA((2,))],
        ),
        out_shape=jax.ShapeDtypeStruct((rows, D_MODEL), _F32),
        compiler_params=pltpu.CompilerParams(
            dimension_semantics=("arbitrary",), vmem_limit_bytes=VMEM_LIMIT),
        name="moe_combine",
    )(plan["seg_off"], plan["c8"], plan["lo"], h, route, lpos, fg, ys)


def _layer_weights(l, norm1_g, w_in, pool_w, pool_scale, conv_dw_w, conv_dw_b, conv_ln_g, conv_ln_b,
                   conv_pw_w, w_out, norm2_g):
    zero = jnp.zeros((POOL_GROUP, POOL_GROUP), _F32)
    pwl = pool_w[l]
    wpa = jnp.block([[pwl[0], zero], [zero, pwl[1]]])
    wpb = jnp.block([[pwl[2], zero], [zero, pwl[3]]])
    return {
        "g1": norm1_g[l][None, :], "w_in": w_in[l].astype(_BF16),
        "wpa": wpa.astype(_BF16), "wpb": wpb.astype(_BF16), "pscale": pool_scale[l][None, :],
        "dw": conv_dw_w[l], "dwb": conv_dw_b[l][None, :], "lng": conv_ln_g[l][None, :],
        "lnb": conv_ln_b[l][None, :], "pw": conv_pw_w[l].astype(_BF16), "w_out": w_out[l].astype(_BF16),
        "g2": norm2_g[l][None, :],
    }


def _routing_plan(cnt):
    n_tok_tiles = cnt.shape[0]
    n = n_tok_tiles * ROW_TILE
    c8 = ((cnt + SEG_ALIGN - 1) // SEG_ALIGN) * SEG_ALIGN
    lo = jnp.cumsum(c8, axis=1) - c8
    within = jnp.cumsum(c8, axis=0) - c8
    glen = jnp.sum(c8, axis=0)
    gpad = ((glen + MOE_ROW_TILE - 1) // MOE_ROW_TILE) * MOE_ROW_TILE
    ends = jnp.cumsum(gpad)
    goff = ends - gpad
    seg_off = goff[None, :] + within
    max_rows = 2 * n + (SEG_ALIGN - 1) * N_EXPERTS * n_tok_tiles
    n_tiles = max_rows // MOE_ROW_TILE + N_EXPERTS
    starts = jnp.arange(n_tiles, dtype=jnp.int32) * MOE_ROW_TILE
    te = jnp.sum((starts[:, None] >= ends[None, :]).astype(jnp.int32), axis=1)
    n_used = (ends[-1] // MOE_ROW_TILE).astype(jnp.int32)
    te_c = jnp.minimum(te, N_EXPERTS - 1)
    valid = jnp.clip(goff[te_c] + glen[te_c] - starts, 0, MOE_ROW_TILE)
    valid = jnp.where(te < N_EXPERTS, valid, 0).astype(jnp.int32)
    last_e = te_c[jnp.maximum(n_used - 1, 0)]
    te_c = jnp.where(te < N_EXPERTS, te_c, last_e).astype(jnp.int32)
    flat = lambda a: a.reshape(-1).astype(jnp.int32)
    total = jnp.full((1,), n_tiles * MOE_ROW_TILE, jnp.int32)
    fill_off = jnp.concatenate([goff + glen, ends[-1:]])
    fill_n = jnp.concatenate([gpad - glen, total - ends[-1:]]) // SEG_ALIGN
    return {"seg_off": flat(seg_off), "c8": flat(c8), "lo": flat(lo), "fill_off": flat(fill_off),
            "fill_n": flat(fill_n), "tile_expert": te_c, "tile_valid": valid, "n_used": n_used.reshape(1),
            "n_tiles": n_tiles}


def kernel(x_prompt, x_sample, state_pool, state_conv, meta_tokens, norm1_g, w_in, pool_w, pool_scale,
           conv_dw_w, conv_dw_b, conv_ln_g, conv_ln_b, conv_pw_w, w_out, norm2_g, ffn_w_gate, ffn_w_up,
           ffn_w_down, router_w, moe_w_gate, moe_w_up, moe_w_down, final_norm_g):
    batch, seq, _ = x_prompt.shape
    dec_batch, dec_seq, _ = x_sample.shape
    assert dec_seq == DEC_SEQ and seq % ROW_TILE == 0 and (dec_batch * dec_seq) % ROW_TILE == 0
    assert (dec_batch * dec_seq) % DEC_ROW_TILE == 0
    lws = [_layer_weights(l, norm1_g, w_in, pool_w, pool_scale, conv_dw_w, conv_dw_b, conv_ln_g, conv_ln_b,
                          conv_pw_w, w_out, norm2_g) for l in range(2)]
    lws[0].update(wg=ffn_w_gate[0].astype(_BF16), wu=ffn_w_up[0].astype(_BF16), wd=ffn_w_down[0].astype(_BF16))
    rw = jnp.zeros((D_MODEL, ROUTE_LANES), _F32).at[:, :N_EXPERTS].set(router_w[0])
    lws[1].update(rw=rw.astype(_BF16))

    n_p = batch * seq
    n_s = dec_batch * dec_seq
    n_t = seq // ROW_TILE
    hp = x_prompt.reshape(n_p, D_MODEL)
    hs = x_sample.reshape(n_s, D_MODEL)
    hm = meta_tokens.astype(_F32)
    zero_pp = jnp.zeros((1, POOL_HALO, POOL_WIDTH), _F32)
    zero_cp = jnp.zeros((1, CONV_HALO, CONV_WIDTH), _F32)
    pool_tm = jnp.transpose(state_pool, (0, 2, 1, 3))
    conv_tm = jnp.transpose(state_conv, (0, 2, 1, 3))

    seq_kw = dict(mode="seq", tile=ROW_TILE, n_seq=batch, n_t=n_t)
    meta_kw = dict(mode="seq", ramp=True, tile=N_META, n_seq=1, n_t=1)
    dec_kw = dict(mode="dec", ramp=False, tile=DEC_ROW_TILE, n_seq=0, n_t=0)

    hm, pst_m0, cst_m0 = _front_call(hm, zero_pp, zero_cp, lws[0], tail="ffn", **meta_kw)
    hp, pst_p0, cst_p0 = _front_call(hp, pst_m0, cst_m0, lws[0], ramp=False, tail="ffn", skew=True, **seq_kw)
    hs, up_s0, v_s0 = _front_call(hs, pool_tm, conv_tm, lws[0], tail="ffn", layer=0, **dec_kw)

    _, pst_m1, cst_m1 = _front_call(hm, zero_pp, zero_cp, lws[1], tail="none", **meta_kw)
    hp, pst_p1, cst_p1, hn_p, route_p, cnt_p = _front_call(hp, pst_m1, cst_m1, lws[1], ramp=False, tail="route", **seq_kw)
    hs, up_s1, v_s1, hn_s, route_s, cnt_s = _front_call(hs, pool_tm, conv_tm, lws[1], tail="route", layer=1, **dec_kw)

    per_tile = lambda c, tile: c[::SUBLANES, :N_EXPERTS].reshape(-1, ROW_TILE // tile, N_EXPERTS).sum(axis=1)
    cnt = jnp.concatenate([per_tile(cnt_p, ROW_TILE), per_tile(cnt_s, DEC_ROW_TILE)], axis=0)
    plan = _routing_plan(cnt.astype(jnp.int32))
    s_tile0 = n_p // ROW_TILE

    xs, lpos = _dispatch_call(plan, hn_p, route_p, hn_s, route_s, tile=ROW_TILE)
    ys = _moe_call(plan["tile_expert"], plan["tile_valid"], plan["n_used"], xs,
                   moe_w_gate[0], moe_w_up[0], moe_w_down[0])

    fg = final_norm_g[None, :]
    y_p = _combine_call(plan, hp, route_p, lpos, fg, ys, tile=ROW_TILE, tile0=0)
    y_s = _combine_call(plan, hs, route_s, lpos, fg, ys, tile=ROW_TILE, tile0=s_tile0)

    new_pool_p = jnp.stack([pst_p0, pst_p1])[:, :, POOL_HALO - POOL_BUF:, :]
    new_conv_p = jnp.stack([cst_p0, cst_p1])[:, :, CONV_HALO - CONV_BUF:, :]
    new_pool_s = jnp.concatenate([pool_tm, jnp.stack([up_s0, up_s1])], axis=1)[:, -POOL_BUF:]
    new_conv_s = jnp.concatenate([conv_tm, jnp.stack([v_s0, v_s1])], axis=1)[:, -CONV_BUF:]
    new_pool_s = jnp.transpose(new_pool_s, (0, 2, 1, 3))
    new_conv_s = jnp.transpose(new_conv_s, (0, 2, 1, 3))
    return (y_p.reshape(batch, seq, D_MODEL), y_s.reshape(dec_batch, dec_seq, D_MODEL),
            new_pool_p, new_conv_p, new_pool_s, new_conv_s)
```

```python
import functools

import jax
import jax.numpy as jnp
from jax import lax
from jax.experimental import pallas as pl
from jax.experimental.pallas import tpu as pltpu

D_MODEL = 1024
N_META = 16
POOL_WIDTH = 512
CONV_WIDTH = 512
POOL_WINDOWS = (2, 4, 8, 16)
POOL_GROUP = 128
POOL_BUF = 15
CONV_KERNEL = 31
CONV_BUF = 30
IN_COLS = POOL_WIDTH + 2 * CONV_WIDTH
N_EXPERTS = 8
DEC_SEQ = 8
RMS_EPS = 1e-6
LN_EPS = 1e-5

POOL_HALO = 16
CONV_HALO = 32
CONV_SHIFT_PAD = CONV_HALO - 8
ROUTE_LANES = 128

ROW_TILE = 512
DEC_ROW_TILE = 256
MIX_CHUNK = 64
FFN_CHUNK = 512

MOE_ROW_TILE = 1024
MOE_SUB = 256
MOE_FF_TILE = 512
SEG_ALIGN = 8
SEG_CHUNK = 64
LOCAL_ROWS = -(-(2 * ROW_TILE + (SEG_ALIGN - 1) * N_EXPERTS) // 128) * 128

VMEM_LIMIT = 56 * 1024 * 1024

_BF16 = jnp.bfloat16
_F32 = jnp.float32


def _dot(a, b):
    return jnp.dot(a, b, preferred_element_type=_F32)


def _sigmoid(x):
    return 1.0 / (1.0 + jnp.exp(-x))


def _rmsnorm(x, g):
    return x * lax.rsqrt(jnp.mean(x * x, axis=-1, keepdims=True) + RMS_EPS) * g


def _mix_chunk(ep_ref, ec_ref, r0, rows, dw_ref, dwb_ref, lng_ref, lnb_ref, pos0, sh_ref=None):
    def conv_rows(start, c0):
        if sh_ref is None or start % 8 == 0:
            return ec_ref[start:start + rows, c0:c0 + 128]
        q, r = divmod(start, 8)
        return sh_ref[r - 1, 8 * q:8 * q + rows, c0:c0 + 128]

    pooled_cols = []
    for gi, win in enumerate(POOL_WINDOWS):
        c0 = gi * POOL_GROUP
        base = POOL_HALO + r0
        x0 = ep_ref[base:base + rows, c0:c0 + POOL_GROUP]
        s = x0
        for k in range(1, win):
            s = s + ep_ref[base - k:base - k + rows, c0:c0 + POOL_GROUP]
        if pos0 is None:
            pooled = s * (1.0 / win) - x0
        else:
            pos = pos0 + r0 + lax.broadcasted_iota(jnp.int32, (rows, POOL_GROUP), 0)
            cnt = jnp.minimum(pos + 1, win).astype(_F32)
            pooled = s / cnt - x0
        pooled_cols.append(pooled)
    pooled = jnp.concatenate(pooled_cols, axis=-1)

    conv_cols = []
    for cg in range(CONV_WIDTH // 128):
        c0 = cg * 128
        base = CONV_HALO - CONV_BUF + r0
        acc = conv_rows(base, c0) * dw_ref[0:1, c0:c0 + 128]
        for k in range(1, CONV_KERNEL):
            acc = acc + conv_rows(base + k, c0) * dw_ref[k:k + 1, c0:c0 + 128]
        conv_cols.append(acc + dwb_ref[:, c0:c0 + 128])
    y = jnp.concatenate(conv_cols, axis=-1)
    mu = jnp.mean(y, axis=-1, keepdims=True)
    d = y - mu
    var = jnp.mean(d * d, axis=-1, keepdims=True)
    yn = d * lax.rsqrt(var + LN_EPS) * lng_ref[...] + lnb_ref[...]
    return pooled, yn * _sigmoid(yn)


def _front_kernel(*refs, mode, ramp, tail, tile, n_t, skew):
    it = iter(refs)
    h_ref, ppast_ref, cpast_ref = next(it), next(it), next(it)
    g1_ref, win_ref, wpa_ref, wpb_ref, psc_ref = next(it), next(it), next(it), next(it), next(it)
    dw_ref, dwb_ref, lng_ref, lnb_ref, pw_ref, wout_ref, g2_ref = (next(it) for _ in range(7))
    if tail == "ffn":
        wg_ref, wu_ref, wd_ref = next(it), next(it), next(it)
    elif tail == "route":
        rw_ref = next(it)
    hout_ref = next(it)
    if mode == "seq":
        pst_ref, cst_ref = next(it), next(it)
    else:
        up_ref, v_ref = next(it), next(it)
    if tail == "route":
        hn2_ref, route_ref, cnt_ref = next(it), next(it), next(it)
    ep_ref, ec_ref, mixed_ref = next(it), next(it), next(it)
    if mode == "seq":
        sh_ref = next(it)
    if skew:
        h1_ref = next(it)

    def ffn_pieces(h_in):
        hb = _rmsnorm(h_in, g2_ref[...]).astype(_BF16)
        d_ff = wg_ref.shape[1]
        hout_ref[...] = h_in

        def piece(c0):
            cs = min(FFN_CHUNK, d_ff - c0)
            g = _dot(hb, wg_ref[:, c0:c0 + cs])
            a = (g * _sigmoid(g) * _dot(hb, wu_ref[:, c0:c0 + cs])).astype(_BF16)
            hout_ref[...] += _dot(a, wd_ref[c0:c0 + cs, :])

        return [functools.partial(piece, c0) for c0 in range(0, d_ff, FFN_CHUNK)]

    def ffn_tail(h_in):
        for piece in ffn_pieces(h_in):
            piece()

    if mode == "seq":
        if skew:
            step = pl.program_id(0)
            t = lax.rem(step, n_t)

            @pl.when(step == 0)
            def _():
                h1_ref[...] = jnp.zeros(h1_ref.shape, _F32)
        else:
            t = pl.program_id(1)

        @pl.when(t == 0)
        def _():
            ep_ref[0:POOL_HALO, :] = ppast_ref[0]
            ec_ref[0:CONV_HALO, :] = cpast_ref[0]

    pending = ffn_pieces(h1_ref[...]) if skew else []

    x = h_ref[...]
    hn = _rmsnorm(x, g1_ref[...]).astype(_BF16)
    u = _dot(hn, win_ref[...])
    up = u[:, :POOL_WIDTH]
    v = u[:, POOL_WIDTH:POOL_WIDTH + CONV_WIDTH] * _sigmoid(u[:, POOL_WIDTH + CONV_WIDTH:])
    if mode == "seq":
        ep_ref[POOL_HALO:, :] = up
        ec_ref[CONV_HALO:, :] = v
    else:
        for g in range(POOL_WIDTH // 128):
            ep_ref[g] = up[:, g * 128:(g + 1) * 128]
            ec_ref[g] = v[:, g * 128:(g + 1) * 128]

    if mode == "seq":
        for r in range(1, 8):
            sh_ref[r - 1] = ec_ref[r:r + tile + CONV_SHIFT_PAD, :]
        chunk = min(MIX_CHUNK, tile)
        n_chunks = tile // chunk
        n_pieces = len(pending)
        for ci in range(n_chunks):
            r0 = ci * chunk
            pos0 = t * tile if ramp else None
            pooled, act = _mix_chunk(ep_ref, ec_ref, r0, chunk, dw_ref, dwb_ref, lng_ref, lnb_ref, pos0, sh_ref)
            mixed_ref[r0:r0 + chunk, 0:POOL_WIDTH] = pooled
            mixed_ref[r0:r0 + chunk, POOL_WIDTH:] = act
            while len(pending) > n_pieces - (ci + 1) * n_pieces // n_chunks:
                pending.pop(0)()
        ptail = ep_ref[tile:tile + POOL_HALO, :]
        ctail = ec_ref[tile:tile + CONV_HALO, :]
        ep_ref[0:POOL_HALO, :] = ptail
        ec_ref[0:CONV_HALO, :] = ctail
        pst_ref[0] = ptail
        cst_ref[0] = ctail
    else:
        spt = tile // DEC_SEQ
        for t in range(DEC_SEQ):
            for g in range(POOL_WIDTH // 128):
                up_ref[t, :, g * 128:(g + 1) * 128] = ep_ref[g, pl.ds(t, spt, stride=DEC_SEQ), :]
                v_ref[t, :, g * 128:(g + 1) * 128] = ec_ref[g, pl.ds(t, spt, stride=DEC_SEQ), :]

        def pool_row(i, cols):
            return ppast_ref[i, :, cols] if i < POOL_BUF else up_ref[i - POOL_BUF, :, cols]

        def conv_row(i):
            return cpast_ref[i] if i < CONV_BUF else v_ref[i - CONV_BUF]

        for t in range(DEC_SEQ):
            pooled_cols = []
            for gi, win in enumerate(POOL_WINDOWS):
                cols = slice(gi * POOL_GROUP, (gi + 1) * POOL_GROUP)
                x0 = pool_row(POOL_BUF + t, cols)
                s = x0
                for k in range(1, win):
                    s = s + pool_row(POOL_BUF + t - k, cols)
                pooled_cols.append(s * (1.0 / win) - x0)
            y = conv_row(t) * dw_ref[0:1, :]
            for k in range(1, CONV_KERNEL):
                y = y + conv_row(t + k) * dw_ref[k:k + 1, :]
            y = y + dwb_ref[...]
            mu = jnp.mean(y, axis=-1, keepdims=True)
            d = y - mu
            var = jnp.mean(d * d, axis=-1, keepdims=True)
            yn = d * lax.rsqrt(var + LN_EPS) * lng_ref[...] + lnb_ref[...]
            act = yn * _sigmoid(yn)
            for g in range(POOL_WIDTH // 128):
                mixed_ref[g, pl.ds(t, spt, stride=DEC_SEQ), :] = pooled_cols[g]
                mixed_ref[POOL_WIDTH // 128 + g, pl.ds(t, spt, stride=DEC_SEQ), :] = act[:, g * 128:(g + 1) * 128]

    def mixed_cols(c0, c1):
        if mode == "seq":
            return mixed_ref[:, c0:c1].astype(_BF16)
        return jnp.concatenate([mixed_ref[g] for g in range(c0 // 128, c1 // 128)], axis=-1).astype(_BF16)

    pa = _dot(mixed_cols(0, 256), wpa_ref[...])
    pb = _dot(mixed_cols(256, 512), wpb_ref[...])
    pool_out = jnp.concatenate([pa, pb], axis=-1) * psc_ref[...]
    conv_out = _dot(mixed_cols(POOL_WIDTH, POOL_WIDTH + CONV_WIDTH), pw_ref[...])
    h1 = (x + _dot(pool_out.astype(_BF16), wout_ref[0:POOL_WIDTH, :])
          + _dot(conv_out.astype(_BF16), wout_ref[POOL_WIDTH:, :]))

    if tail == "none":
        hout_ref[...] = h1
    elif skew:
        h1_ref[...] = h1
    elif tail == "ffn":
        ffn_tail(h1)
    else:
        hn2 = _rmsnorm(h1, g2_ref[...])
        hout_ref[...] = h1
        hn2_ref[...] = hn2.astype(_BF16)
        logits = _dot(hn2.astype(_BF16), rw_ref[...])
        lane = lax.broadcasted_iota(jnp.int32, logits.shape, 1).astype(_F32)
        neg = jnp.float32(-jnp.inf)
        lg = jnp.where(lane < N_EXPERTS, logits, neg)
        m1 = jnp.max(lg, axis=-1, keepdims=True)
        i1 = jnp.min(jnp.where(lg == m1, lane, float(ROUTE_LANES)), axis=-1, keepdims=True)
        lg2 = jnp.where(lane == i1, neg, lg)
        m2 = jnp.max(lg2, axis=-1, keepdims=True)
        i2 = jnp.min(jnp.where(lg2 == m2, lane, float(ROUTE_LANES)), axis=-1, keepdims=True)
        e21 = jnp.exp(m2 - m1)
        gate1 = 1.0 / (1.0 + e21)
        gate2 = e21 * gate1
        route_ref[...] = jnp.where(lane == 0, i1, jnp.where(lane == 1, i2, jnp.where(
            lane == 2, gate1, jnp.where(lane == 3, gate2, 0.0))))
        chosen = (lane == i1).astype(_F32) + (lane == i2).astype(_F32)
        cnt_ref[...] = jnp.broadcast_to(jnp.sum(chosen, axis=0, keepdims=True), cnt_ref.shape)


def _const_spec(shape):
    nd = len(shape)
    return pl.BlockSpec(shape, lambda *_: (0,) * nd, pipeline_mode=pl.Buffered(1))


def _front_call(h, ppast, cpast, lw, *, mode, ramp, tail, tile, n_seq, n_t, skew=False, layer=None):
    rows = h.shape[0]
    weights = [lw["g1"], lw["w_in"], lw["wpa"], lw["wpb"], lw["pscale"], lw["dw"], lw["dwb"], lw["lng"],
               lw["lnb"], lw["pw"], lw["w_out"], lw["g2"]]
    if tail == "ffn":
        weights += [lw["wg"], lw["wu"], lw["wd"]]
    elif tail == "route":
        weights += [lw["rw"]]
    w_specs = [_const_spec(w.shape) for w in weights]

    if mode == "seq" and skew:
        assert tail == "ffn" and ppast.shape[0] == 1
        n_tiles = n_seq * n_t
        grid = (n_tiles + 1,)
        row_map = lambda s: (jnp.minimum(s, n_tiles - 1), 0)
        seq_map = lambda s: (jnp.minimum(s, n_tiles - 1) // n_t, 0, 0)
        past_specs = [pl.BlockSpec((1, POOL_HALO, POOL_WIDTH), lambda s: (0, 0, 0)),
                      pl.BlockSpec((1, CONV_HALO, CONV_WIDTH), lambda s: (0, 0, 0))]
        out_shape = [jax.ShapeDtypeStruct((rows, D_MODEL), _F32),
                     jax.ShapeDtypeStruct((n_seq, POOL_HALO, POOL_WIDTH), _F32),
                     jax.ShapeDtypeStruct((n_seq, CONV_HALO, CONV_WIDTH), _F32)]
        out_specs = [pl.BlockSpec((tile, D_MODEL), lambda s: (jnp.maximum(s - 1, 0), 0)),
                     pl.BlockSpec((1, POOL_HALO, POOL_WIDTH), seq_map),
                     pl.BlockSpec((1, CONV_HALO, CONV_WIDTH), seq_map)]
        semantics = ("arbitrary",)
    elif mode == "seq":
        grid = (n_seq, n_t)
        row_map = lambda b, t: (b * n_t + t, 0)
        shared = ppast.shape[0] == 1
        past_map = (lambda b, t: (0, 0, 0)) if shared else (lambda b, t: (b, 0, 0))
        past_specs = [pl.BlockSpec((1, POOL_HALO, POOL_WIDTH), past_map),
                      pl.BlockSpec((1, CONV_HALO, CONV_WIDTH), past_map)]
        out_shape = [jax.ShapeDtypeStruct((rows, D_MODEL), _F32),
                     jax.ShapeDtypeStruct((n_seq, POOL_HALO, POOL_WIDTH), _F32),
                     jax.ShapeDtypeStruct((n_seq, CONV_HALO, CONV_WIDTH), _F32)]
        out_specs = [pl.BlockSpec((tile, D_MODEL), row_map),
                     pl.BlockSpec((1, POOL_HALO, POOL_WIDTH), lambda b, t: (b, 0, 0)),
                     pl.BlockSpec((1, CONV_HALO, CONV_WIDTH), lambda b, t: (b, 0, 0))]
        semantics = ("arbitrary", "arbitrary")
    else:
        grid = (rows // tile,)
        row_map = lambda i: (i, 0)
        spt = tile // DEC_SEQ
        past_specs = [pl.BlockSpec((None, POOL_BUF, spt, POOL_WIDTH), lambda i: (layer, 0, i, 0)),
                      pl.BlockSpec((None, CONV_BUF, spt, CONV_WIDTH), lambda i: (layer, 0, i, 0))]
        out_shape = [jax.ShapeDtypeStruct((rows, D_MODEL), _F32),
                     jax.ShapeDtypeStruct((DEC_SEQ, rows // DEC_SEQ, POOL_WIDTH), _F32),
                     jax.ShapeDtypeStruct((DEC_SEQ, rows // DEC_SEQ, CONV_WIDTH), _F32)]
        out_specs = [pl.BlockSpec((tile, D_MODEL), row_map),
                     pl.BlockSpec((DEC_SEQ, spt, POOL_WIDTH), lambda i: (0, i, 0)),
                     pl.BlockSpec((DEC_SEQ, spt, CONV_WIDTH), lambda i: (0, i, 0))]
        semantics = ("arbitrary",)
    if tail == "route":
        out_shape += [jax.ShapeDtypeStruct((rows, D_MODEL), _BF16),
                      jax.ShapeDtypeStruct((rows, ROUTE_LANES), _F32),
                      jax.ShapeDtypeStruct((rows // tile * 8, ROUTE_LANES), _F32)]
        out_specs += [pl.BlockSpec((tile, D_MODEL), row_map), pl.BlockSpec((tile, ROUTE_LANES), row_map),
                      pl.BlockSpec((8, ROUTE_LANES), row_map)]

    if mode == "seq":
        scratch = [pltpu.VMEM((POOL_HALO + tile, POOL_WIDTH), _F32),
                   pltpu.VMEM((CONV_HALO + tile, CONV_WIDTH), _F32),
                   pltpu.VMEM((tile, POOL_WIDTH + CONV_WIDTH), _F32),
                   pltpu.VMEM((7, tile + CONV_SHIFT_PAD, CONV_WIDTH), _F32)]
    else:
        scratch = [pltpu.VMEM((POOL_WIDTH // 128, tile, 128), _F32),
                   pltpu.VMEM((CONV_WIDTH // 128, tile, 128), _F32),
                   pltpu.VMEM(((POOL_WIDTH + CONV_WIDTH) // 128, tile, 128), _F32)]
    if skew:
        scratch += [pltpu.VMEM((tile, D_MODEL), _F32)]

    body = functools.partial(_front_kernel, mode=mode, ramp=ramp, tail=tail, tile=tile, n_t=n_t, skew=skew)
    return pl.pallas_call(
        body,
        grid=grid,
        in_specs=[pl.BlockSpec((tile, D_MODEL), row_map)] + past_specs + w_specs,
        out_specs=out_specs,
        out_shape=out_shape,
        scratch_shapes=scratch,
        compiler_params=pltpu.CompilerParams(dimension_semantics=semantics, vmem_limit_bytes=VMEM_LIMIT),
        name=f"front_{mode}_{tail}" + ("_skew" if skew else ""),
    )(h, ppast, cpast, *weights)


def _segment_copies(c8_ref, lo_ref, segoff_ref, tidx, make_copy, act):
    for e in range(N_EXPERTS):
        k = tidx * N_EXPERTS + e
        lo_e = lo_ref[k]
        so_e = segoff_ref[k]
        n_big = c8_ref[k] // SEG_CHUNK
        rest = n_big * SEG_CHUNK

        def big(c, carry):
            act(make_copy(pl.multiple_of(lo_e + c * SEG_CHUNK, SEG_ALIGN),
                          pl.multiple_of(so_e + c * SEG_CHUNK, SEG_ALIGN), SEG_CHUNK), e % 2)
            return carry

        def small(c, carry):
            act(make_copy(pl.multiple_of(lo_e + rest + c * SEG_ALIGN, SEG_ALIGN),
                          pl.multiple_of(so_e + rest + c * SEG_ALIGN, SEG_ALIGN), SEG_ALIGN), e % 2)
            return carry

        lax.fori_loop(0, n_big, big, 0)
        lax.fori_loop(0, (c8_ref[k] - rest) // SEG_ALIGN, small, 0)


def _dispatch_kernel(segoff_ref, c8_ref, lo_ref, filloff_ref, filln_ref, hn_a_ref, route_a_ref, hn_b_ref,
                     route_b_ref, xs_hbm, lpos_ref, xl_ref, sem, zero_ref, fill_sem, *, tile, n_a):
    t = pl.program_id(0)
    n = pl.num_programs(0)
    slot = t % 2
    tt = t
    in_a = t < n_a
    hn = jnp.where(in_a, hn_a_ref[...], hn_b_ref[...])
    route = jnp.where(in_a, route_a_ref[...], route_b_ref[...])

    @pl.when(t == 0)
    def _():
        zero_ref[...] = jnp.zeros(zero_ref.shape, _F32)

        def fill(act):
            for r in range(N_EXPERTS + 1):
                off = filloff_ref[r]
                n_big = filln_ref[r] // (SEG_CHUNK // SEG_ALIGN)
                rest = n_big * SEG_CHUNK

                def big(c, carry):
                    dst = xs_hbm.at[pl.ds(pl.multiple_of(off + c * SEG_CHUNK, SEG_ALIGN), SEG_CHUNK)]
                    act(pltpu.make_async_copy(zero_ref, dst, fill_sem))
                    return carry

                def small(c, carry):
                    dst = xs_hbm.at[pl.ds(pl.multiple_of(off + rest + c * SEG_ALIGN, SEG_ALIGN), SEG_ALIGN)]
                    act(pltpu.make_async_copy(zero_ref.at[pl.ds(0, SEG_ALIGN)], dst, fill_sem))
                    return carry

                lax.fori_loop(0, n_big, big, 0)
                lax.fori_loop(0, filln_ref[r] - n_big * (SEG_CHUNK // SEG_ALIGN), small, 0)

        fill(lambda c: c.start())
        fill(lambda c: c.wait())

    lane = lax.broadcasted_iota(jnp.int32, (tile, ROUTE_LANES), 1).astype(_F32)
    oh0 = (lane == route[:, 0:1]).astype(_F32)
    oh1 = (lane == route[:, 1:2]).astype(_F32)
    earlier = (lax.broadcasted_iota(jnp.int32, (tile, tile), 1)
               < lax.broadcasted_iota(jnp.int32, (tile, tile), 0)).astype(_BF16)
    rank = _dot(earlier, (oh0 + oh1).astype(_BF16))
    lane1 = lax.broadcasted_iota(jnp.int32, (1, ROUTE_LANES), 1)
    lo_vec = jnp.zeros((1, ROUTE_LANES), _F32)
    for e in range(N_EXPERTS):
        lo_vec = jnp.where(lane1 == e, lo_ref[tt * N_EXPERTS + e].astype(_F32), lo_vec)
    posv = rank + lo_vec
    p0 = jnp.sum(oh0 * posv, axis=-1, keepdims=True)
    p1 = jnp.sum(oh1 * posv, axis=-1, keepdims=True)
    slab = jnp.where(lane == 0, p0, jnp.where(lane == 1, p1, 0.0))
    lpos_ref[...] = slab
    slab_t = slab.T
    rowi = lax.broadcasted_iota(jnp.int32, (LOCAL_ROWS, tile), 0).astype(_F32)
    place = jnp.where(jnp.logical_or(rowi == slab_t[0:1, :], rowi == slab_t[1:2, :]), 1.0, 0.0).astype(_BF16)
    xl_ref[slot] = _dot(place, hn.astype(_BF16))

    def copies(tidx, s):
        def make_copy(lo, so, rows):
            return pltpu.make_async_copy(xl_ref.at[s, pl.ds(lo, rows)], xs_hbm.at[pl.ds(so, rows)], sem.at[s])
        return functools.partial(_segment_copies, c8_ref, lo_ref, segoff_ref, tidx, make_copy)

    copies(tt, slot)(lambda c, queue: c.start(priority=queue))

    @pl.when(t > 0)
    def _():
        copies(tt - 1, 1 - slot)(lambda c, queue: c.wait())

    @pl.when(t == n - 1)
    def _():
        copies(tt, slot)(lambda c, queue: c.wait())


def _dispatch_call(plan, hn_a, route_a, hn_b, route_b, *, tile):
    n_a = hn_a.shape[0] // tile
    n_b = hn_b.shape[0] // tile
    amap = lambda i, *_: (jnp.minimum(i, n_a - 1), 0)
    bmap = lambda i, *_: (jnp.maximum(i - n_a, 0), 0)
    tables = (plan["seg_off"], plan["c8"], plan["lo"], plan["fill_off"], plan["fill_n"])
    return pl.pallas_call(
        functools.partial(_dispatch_kernel, tile=tile, n_a=n_a),
        grid_spec=pltpu.PrefetchScalarGridSpec(
            num_scalar_prefetch=len(tables),
            grid=(n_a + n_b,),
            in_specs=[pl.BlockSpec((tile, D_MODEL), amap), pl.BlockSpec((tile, ROUTE_LANES), amap),
                      pl.BlockSpec((tile, D_MODEL), bmap), pl.BlockSpec((tile, ROUTE_LANES), bmap)],
            out_specs=[pl.BlockSpec(memory_space=pl.ANY), pl.BlockSpec((tile, ROUTE_LANES), lambda i, *_: (i, 0))],
            scratch_shapes=[pltpu.VMEM((2, LOCAL_ROWS, D_MODEL), _F32), pltpu.SemaphoreType.DMA((2,)),
                            pltpu.VMEM((SEG_CHUNK, D_MODEL), _F32), pltpu.SemaphoreType.DMA(())],
        ),
        out_shape=[jax.ShapeDtypeStruct((plan["n_tiles"] * MOE_ROW_TILE, D_MODEL), _F32),
                   jax.ShapeDtypeStruct(((n_a + n_b) * tile, ROUTE_LANES), _F32)],
        compiler_params=pltpu.CompilerParams(
            dimension_semantics=("arbitrary",), vmem_limit_bytes=VMEM_LIMIT, has_side_effects=True),
        name="moe_dispatch",
    )(*tables, hn_a, route_a, hn_b, route_b)


def _moe_kernel(te_ref, nv_ref, nu_ref, x_ref, wg_ref, wu_ref, wd_ref, o_ref, xb_ref):
    del te_ref, nu_ref
    i = pl.program_id(0)
    j = pl.program_id(1)
    nv = nv_ref[i]

    @pl.when(j == 0)
    def _():
        o_ref[...] = jnp.zeros(o_ref.shape, _F32)
        xb_ref[...] = x_ref[...].astype(_BF16)

    def expert_ffn(xs, wg, wu, wd):
        g = _dot(xs, wg)
        a = (g * _sigmoid(g) * _dot(xs, wu)).astype(_BF16)
        return _dot(a, wd)

    for r in range(MOE_SUB, MOE_ROW_TILE + 1, MOE_SUB):
        @pl.when(jnp.logical_and(nv > r - MOE_SUB, nv <= r))
        def _():
            o_ref[0:r, :] += expert_ffn(xb_ref[0:r, :], wg_ref[0].astype(_BF16), wu_ref[0].astype(_BF16),
                                        wd_ref[0].astype(_BF16))


def _moe_call(tile_expert, tile_valid, n_used, xs, wg, wu, wd):
    n_tiles = xs.shape[0] // MOE_ROW_TILE
    eff = wg.shape[2]
    n_j = eff // MOE_FF_TILE

    def row_map(i, j, te, nv, nu):
        return (jnp.maximum(jnp.minimum(i, nu[0] - 1), 0), 0)

    def col_of(i, j, nu):
        return jnp.where(i < nu[0], j, n_j - 1)

    return pl.pallas_call(
        _moe_kernel,
        grid_spec=pltpu.PrefetchScalarGridSpec(
            num_scalar_prefetch=3,
            grid=(n_tiles, n_j),
            in_specs=[
                pl.BlockSpec((MOE_ROW_TILE, D_MODEL), row_map),
                pl.BlockSpec((1, D_MODEL, MOE_FF_TILE), lambda i, j, te, nv, nu: (te[i], 0, col_of(i, j, nu))),
                pl.BlockSpec((1, D_MODEL, MOE_FF_TILE), lambda i, j, te, nv, nu: (te[i], 0, col_of(i, j, nu))),
                pl.BlockSpec((1, MOE_FF_TILE, D_MODEL), lambda i, j, te, nv, nu: (te[i], col_of(i, j, nu), 0)),
            ],
            out_specs=pl.BlockSpec((MOE_ROW_TILE, D_MODEL), lambda i, j, te, nv, nu: (i, 0)),
            scratch_shapes=[pltpu.VMEM((MOE_ROW_TILE, D_MODEL), _BF16)],
        ),
        out_shape=jax.ShapeDtypeStruct(xs.shape, _F32),
        compiler_params=pltpu.CompilerParams(
            dimension_semantics=("arbitrary", "arbitrary"), vmem_limit_bytes=VMEM_LIMIT),
        name="moe_grouped",
    )(tile_expert, tile_valid, n_used, xs, wg, wu, wd)


def _combine_kernel(segoff_ref, c8_ref, lo_ref, h_ref, route_ref, lpos_ref, fg_ref, ys_hbm, o_ref, yl_ref, sem,
                    *, tile, tile0):
    t = pl.program_id(0)
    n = pl.num_programs(0)
    slot = t % 2
    tt = tile0 + t

    def copies(tidx, s):
        def make_copy(lo, so, rows):
            return pltpu.make_async_copy(ys_hbm.at[pl.ds(so, rows)], yl_ref.at[s, pl.ds(lo, rows)], sem.at[s])
        return functools.partial(_segment_copies, c8_ref, lo_ref, segoff_ref, tidx, make_copy)

    @pl.when(t == 0)
    def _():
        yl_ref[...] = jnp.zeros(yl_ref.shape, _F32)
        copies(tt, slot)(lambda c, queue: c.start(priority=queue))

    @pl.when(t + 1 < n)
    def _():
        copies(tt + 1, 1 - slot)(lambda c, queue: c.start(priority=queue))

    copies(tt, slot)(lambda c, queue: c.wait())
    last = tt * N_EXPERTS + N_EXPERTS - 1
    used = lo_ref[last] + c8_ref[last]
    rowi = lax.broadcasted_iota(jnp.int32, (LOCAL_ROWS, 1), 0)
    ylb = jnp.where(rowi < used, yl_ref[slot], 0.0).astype(_BF16)
    lanel = lax.broadcasted_iota(jnp.int32, (tile, LOCAL_ROWS), 1).astype(_F32)
    take = (jnp.where(lanel == lpos_ref[:, 0:1], route_ref[:, 2:3], 0.0)
            + jnp.where(lanel == lpos_ref[:, 1:2], route_ref[:, 3:4], 0.0)).astype(_BF16)
    o_ref[...] = _rmsnorm(h_ref[...] + _dot(take, ylb), fg_ref[...])


def _combine_call(plan, h, route, lpos, fg, ys, *, tile, tile0):
    rows = h.shape[0]
    smap = lambda i, *_: (i, 0)
    return pl.pallas_call(
        functools.partial(_combine_kernel, tile=tile, tile0=tile0),
        grid_spec=pltpu.PrefetchScalarGridSpec(
            num_scalar_prefetch=3,
            grid=(rows // tile,),
            in_specs=[
                pl.BlockSpec((tile, D_MODEL), smap),
                pl.BlockSpec((tile, ROUTE_LANES), smap),
                pl.BlockSpec((tile, ROUTE_LANES), lambda i, *_: (i + tile0, 0)),
                pl.BlockSpec((1, D_MODEL), lambda i, *_: (0, 0)),
                pl.BlockSpec(memory_space=pl.ANY),
            ],
            out_specs=pl.BlockSpec((tile, D_MODEL), smap),
            scratch_shapes=[pltpu.VMEM((2, LOCAL_ROWS, D_MODEL), _F32), pltpu.SemaphoreType.DMA((2,))],
        ),
        out_shape=jax.ShapeDtypeStruct((rows, D_MODEL), _F32),
        compiler_params=pltpu.CompilerParams(
            dimension_semantics=("arbitrary",), vmem_limit_bytes=VMEM_LIMIT),
        name="moe_combine",
    )(plan["seg_off"], plan["c8"], plan["lo"], h, route, lpos, fg, ys)


def _layer_weights(l, norm1_g, w_in, pool_w, pool_scale, conv_dw_w, conv_dw_b, conv_ln_g, conv_ln_b,
                   conv_pw_w, w_out, norm2_g):
    zero = jnp.zeros((POOL_GROUP, POOL_GROUP), _F32)
    pwl = pool_w[l]
    wpa = jnp.block([[pwl[0], zero], [zero, pwl[1]]])
    wpb = jnp.block([[pwl[2], zero], [zero, pwl[3]]])
    return {
        "g1": norm1_g[l][None, :], "w_in": w_in[l].astype(_BF16),
        "wpa": wpa.astype(_BF16), "wpb": wpb.astype(_BF16), "pscale": pool_scale[l][None, :],
        "dw": conv_dw_w[l], "dwb": conv_dw_b[l][None, :], "lng": conv_ln_g[l][None, :],
        "lnb": conv_ln_b[l][None, :], "pw": conv_pw_w[l].astype(_BF16), "w_out": w_out[l].astype(_BF16),
        "g2": norm2_g[l][None, :],
    }


def _routing_plan(cnt):
    n_tok_tiles = cnt.shape[0]
    n = n_tok_tiles * ROW_TILE
    c8 = ((cnt + SEG_ALIGN - 1) // SEG_ALIGN) * SEG_ALIGN
    lo = jnp.cumsum(c8, axis=1) - c8
    within = jnp.cumsum(c8, axis=0) - c8
    glen = jnp.sum(c8, axis=0)
    gpad = ((glen + MOE_ROW_TILE - 1) // MOE_ROW_TILE) * MOE_ROW_TILE
    ends = jnp.cumsum(gpad)
    goff = ends - gpad
    seg_off = goff[None, :] + within
    max_rows = 2 * n + (SEG_ALIGN - 1) * N_EXPERTS * n_tok_tiles
    n_tiles = -(-max_rows // MOE_ROW_TILE) + N_EXPERTS
    starts = jnp.arange(n_tiles, dtype=jnp.int32) * MOE_ROW_TILE
    te = jnp.sum((starts[:, None] >= ends[None, :]).astype(jnp.int32), axis=1)
    n_used = (ends[-1] // MOE_ROW_TILE).astype(jnp.int32)
    te_c = jnp.minimum(te, N_EXPERTS - 1)
    valid = jnp.clip(goff[te_c] + glen[te_c] - starts, 0, MOE_ROW_TILE)
    valid = jnp.where(te < N_EXPERTS, valid, 0).astype(jnp.int32)
    last_e = te_c[jnp.maximum(n_used - 1, 0)]
    te_c = jnp.where(te < N_EXPERTS, te_c, last_e).astype(jnp.int32)
    flat = lambda a: a.reshape(-1).astype(jnp.int32)
    total = jnp.full((1,), n_tiles * MOE_ROW_TILE, jnp.int32)
    fill_off = jnp.concatenate([goff + glen, ends[-1:]])
    fill_n = jnp.concatenate([gpad - glen, total - ends[-1:]]) // SEG_ALIGN
    return {"seg_off": flat(seg_off), "c8": flat(c8), "lo": flat(lo), "fill_off": flat(fill_off),
            "fill_n": flat(fill_n), "tile_expert": te_c, "tile_valid": valid, "n_used": n_used.reshape(1),
            "n_tiles": n_tiles}


def kernel(x_prompt, x_sample, state_pool, state_conv, meta_tokens, norm1_g, w_in, pool_w, pool_scale,
           conv_dw_w, conv_dw_b, conv_ln_g, conv_ln_b, conv_pw_w, w_out, norm2_g, ffn_w_gate, ffn_w_up,
           ffn_w_down, router_w, moe_w_gate, moe_w_up, moe_w_down, final_norm_g):
    batch, seq, _ = x_prompt.shape
    dec_batch, dec_seq, _ = x_sample.shape
    assert dec_seq == DEC_SEQ and seq % ROW_TILE == 0 and (dec_batch * dec_seq) % ROW_TILE == 0
    assert (dec_batch * dec_seq) % DEC_ROW_TILE == 0
    lws = [_layer_weights(l, norm1_g, w_in, pool_w, pool_scale, conv_dw_w, conv_dw_b, conv_ln_g, conv_ln_b,
                          conv_pw_w, w_out, norm2_g) for l in range(2)]
    lws[0].update(wg=ffn_w_gate[0].astype(_BF16), wu=ffn_w_up[0].astype(_BF16), wd=ffn_w_down[0].astype(_BF16))
    rw = jnp.zeros((D_MODEL, ROUTE_LANES), _F32).at[:, :N_EXPERTS].set(router_w[0])
    lws[1].update(rw=rw.astype(_BF16))

    n_p = batch * seq
    n_s = dec_batch * dec_seq
    n_t = seq // ROW_TILE
    hp = x_prompt.reshape(n_p, D_MODEL)
    hs = x_sample.reshape(n_s, D_MODEL)
    hm = meta_tokens.astype(_F32)
    zero_pp = jnp.zeros((1, POOL_HALO, POOL_WIDTH), _F32)
    zero_cp = jnp.zeros((1, CONV_HALO, CONV_WIDTH), _F32)
    pool_tm = jnp.transpose(state_pool, (0, 2, 1, 3))
    conv_tm = jnp.transpose(state_conv, (0, 2, 1, 3))

    seq_kw = dict(mode="seq", tile=ROW_TILE, n_seq=batch, n_t=n_t)
    meta_kw = dict(mode="seq", ramp=True, tile=N_META, n_seq=1, n_t=1)
    dec_kw = dict(mode="dec", ramp=False, tile=DEC_ROW_TILE, n_seq=0, n_t=0)

    hm, pst_m0, cst_m0 = _front_call(hm, zero_pp, zero_cp, lws[0], tail="ffn", **meta_kw)
    hp, pst_p0, cst_p0 = _front_call(hp, pst_m0, cst_m0, lws[0], ramp=False, tail="ffn", skew=True, **seq_kw)
    hs, up_s0, v_s0 = _front_call(hs, pool_tm, conv_tm, lws[0], tail="ffn", layer=0, **dec_kw)

    _, pst_m1, cst_m1 = _front_call(hm, zero_pp, zero_cp, lws[1], tail="none", **meta_kw)
    hp, pst_p1, cst_p1, hn_p, route_p, cnt_p = _front_call(hp, pst_m1, cst_m1, lws[1], ramp=False, tail="route", **seq_kw)
    hs, up_s1, v_s1, hn_s, route_s, cnt_s = _front_call(hs, pool_tm, conv_tm, lws[1], tail="route", layer=1, **dec_kw)

    per_tile = lambda c, tile: c[::8, :N_EXPERTS].reshape(-1, ROW_TILE // tile, N_EXPERTS).sum(axis=1)
    cnt = jnp.concatenate([per_tile(cnt_p, ROW_TILE), per_tile(cnt_s, DEC_ROW_TILE)], axis=0)
    plan = _routing_plan(cnt.astype(jnp.int32))
    s_tile0 = n_p // ROW_TILE

    xs, lpos = _dispatch_call(plan, hn_p, route_p, hn_s, route_s, tile=ROW_TILE)
    ys = _moe_call(plan["tile_expert"], plan["tile_valid"], plan["n_used"], xs,
                   moe_w_gate[0], moe_w_up[0], moe_w_down[0])

    fg = final_norm_g[None, :]
    y_p = _combine_call(plan, hp, route_p, lpos, fg, ys, tile=ROW_TILE, tile0=0)
    y_s = _combine_call(plan, hs, route_s, lpos, fg, ys, tile=ROW_TILE, tile0=s_tile0)

    new_pool_p = jnp.stack([pst_p0, pst_p1])[:, :, POOL_HALO - POOL_BUF:, :]
    new_conv_p = jnp.stack([cst_p0, cst_p1])[:, :, CONV_HALO - CONV_BUF:, :]
    new_pool_s = jnp.concatenate([pool_tm, jnp.stack([up_s0, up_s1])], axis=1)[:, -POOL_BUF:]
    new_conv_s = jnp.concatenate([conv_tm, jnp.stack([v_s0, v_s1])], axis=1)[:, -CONV_BUF:]
    new_pool_s = jnp.transpose(new_pool_s, (0, 2, 1, 3))
    new_conv_s = jnp.transpose(new_conv_s, (0, 2, 1, 3))
    return (y_p.reshape(batch, seq, D_MODEL), y_s.reshape(dec_batch, dec_seq, D_MODEL),
            new_pool_p, new_conv_p, new_pool_s, new_conv_s)
```
